```python
import math
import jax, jax.numpy as jnp
from jax import lax
import numpy as np

D_MODEL = 1024
BATCH = 8
SEQ = 4096
DEPTH = 2
DEC_BATCH = 32
DEC_SEQ = 1
PAST_LEN = 16384
PAGE_SIZE = 128

D_MIX = D_MODEL
D_SSM = D_MIX // 2
D_ATT = D_MIX - D_SSM
SSM_CH = 16
N_SSM_GROUPS = D_SSM // SSM_CH
SSM_STATE = 64
HEAD_DIM = 64
N_HEADS = D_ATT // HEAD_DIM
N_KV_HEADS = 2
GQA = N_HEADS // N_KV_HEADS
KV_WIDTH = N_KV_HEADS * HEAD_DIM
CMP_STRIDE = 16
CMP_LEN = 2 * CMP_STRIDE
SLC_BLOCK = 64
N_SLC = 16
WINDOW = 512
Q_BLOCK = 64
ROPE_THETA = 10000.0
D_IN = D_SSM + D_ATT + 6 * KV_WIDTH + 3 * N_HEADS
PEER_HEADS = 8
PEER_DK = 256
N_KEYS = 128
N_EXPERTS = N_KEYS * N_KEYS
PEER_TOPK = 16
DN_ALPHA = (2 * DEPTH) ** 0.25
DN_BETA = (8 * DEPTH) ** -0.25
LN_EPS = 1e-5

kernel_name = 'hymba_s5_nsa_peer_step'


def layer_norm(x, g, b):
    xf = x.astype(jnp.float32)
    mu = xf.mean(-1, keepdims=True)
    var = jnp.square(xf - mu).mean(-1, keepdims=True)
    return ((xf - mu) * lax.rsqrt(var + LN_EPS) * g.astype(jnp.float32) + b.astype(jnp.float32)).astype(x.dtype)


def rope(x, pos):
    half = HEAD_DIM // 2
    inv = ROPE_THETA ** (-jnp.arange(half, dtype=jnp.float32) / half)
    ang = pos.astype(jnp.float32)[:, None] * inv
    shape = (1, pos.shape[0]) + (1,) * (x.ndim - 3) + (half,)
    cos, sin = jnp.cos(ang).reshape(shape), jnp.sin(ang).reshape(shape)
    xf = x.astype(jnp.float32)
    x1, x2 = xf[..., :half], xf[..., half:]
    return jnp.concatenate([x1 * cos - x2 * sin, x2 * cos + x1 * sin], -1).astype(x.dtype)


def masked_softmax(s, mask):
    s = jnp.where(mask, s.astype(jnp.float32), -jnp.inf)
    m = jnp.max(s, -1, keepdims=True)
    m = jnp.where(jnp.isfinite(m), m, 0.0)
    e = jnp.where(mask, jnp.exp(s - m), 0.0)
    return e / jnp.maximum(e.sum(-1, keepdims=True), jnp.finfo(jnp.float32).tiny)


def _ssm_combine(l, r):
    lar, lai, lbr, lbi = l
    rar, rai, rbr, rbi = r
    return (lar * rar - lai * rai, lar * rai + lai * rar,
            rar * lbr - rai * lbi + rbr, rar * lbi + rai * lbr + rbi)


def s5_mixer(u, s0, a_re, a_im, log_dt, b_re, b_im, c_re, c_im, d_skip, w_glu):
    f32 = jnp.float32
    bsz, S, _ = u.shape
    uf = u.astype(f32).reshape(bsz, S, N_SSM_GROUPS, SSM_CH)
    dt = jnp.exp(log_dt.astype(f32))[:, None]
    ar, ai = a_re.astype(f32), a_im.astype(f32)
    mag = jnp.exp(dt * ar)
    abar_re, abar_im = mag * jnp.cos(dt * ai), mag * jnp.sin(dt * ai)
    den = ar * ar + ai * ai
    f_re = ((abar_re - 1.0) * ar + abar_im * ai) / den
    f_im = (abar_im * ar - (abar_re - 1.0) * ai) / den
    br, bi = b_re.astype(f32), b_im.astype(f32)
    bb_re = f_re[..., None] * br - f_im[..., None] * bi
    bb_im = f_re[..., None] * bi + f_im[..., None] * br
    bu_re = jnp.einsum('bsgc,gpc->bsgp', uf, bb_re)
    bu_im = jnp.einsum('bsgc,gpc->bsgp', uf, bb_im)
    a_re_t = jnp.broadcast_to(abar_re, bu_re.shape)
    a_im_t = jnp.broadcast_to(abar_im, bu_im.shape)
    p_re, p_im, h_re, h_im = lax.associative_scan(_ssm_combine, (a_re_t, a_im_t, bu_re, bu_im), axis=1)
    s0r = s0[:, 0].astype(f32)[:, None]
    s0i = s0[:, 1].astype(f32)[:, None]
    sr = p_re * s0r - p_im * s0i + h_re
    si = p_re * s0i + p_im * s0r + h_im
    y = jnp.einsum('bsgp,gcp->bsgc', sr, c_re.astype(f32)) - jnp.einsum('bsgp,gcp->bsgc', si, c_im.astype(f32))
    y = y.reshape(bsz, S, D_SSM) + d_skip.astype(f32) * u.astype(f32)
    gl = jax.nn.gelu(y).astype(u.dtype) @ w_glu
    out = gl[..., :D_SSM] * jax.nn.sigmoid(gl[..., D_SSM:])
    new_state = jnp.stack([sr[:, -1], si[:, -1]], axis=1).astype(s0.dtype)
    return out, new_state


def compress_blocks(rows, pe, w1, w2):
    bsz, L = rows.shape[:2]
    sub = rows.reshape(bsz, L // CMP_STRIDE, CMP_STRIDE, N_KV_HEADS, HEAD_DIM)
    pe = pe.reshape(2, CMP_STRIDE, HEAD_DIM)
    w1 = w1.reshape(2, CMP_STRIDE, HEAD_DIM, HEAD_DIM)
    first = jnp.einsum('bnjhd,jde->bnhe', sub + pe[0][:, None, :], w1[0])
    second = jnp.einsum('bnjhd,jde->bnhe', sub + pe[1][:, None, :], w1[1])
    return jnp.einsum('bnhe,ef->bnhf', jax.nn.gelu(first[:, :-1] + second[:, 1:]), w2)


def nsa_attention(q, kv_full, win_kv, gates, q_pos0, cmp_pe, cmp_w1, cmp_w2):
    bsz, S = q.shape[:2]
    L_pad = kv_full.shape[1]
    win_len = win_kv.shape[1] - S
    n_blk = L_pad // SLC_BLOCK
    n_sel = min(N_SLC, n_blk)
    scale = HEAD_DIM ** -0.5
    kc = compress_blocks(kv_full[:, :, 0], cmp_pe[0], cmp_w1[0], cmp_w2[0])
    vc = compress_blocks(kv_full[:, :, 1], cmp_pe[1], cmp_w1[1], cmp_w2[1])
    n_cmp = kc.shape[1]
    cmp_end = jnp.arange(n_cmp, dtype=jnp.int32) * CMP_STRIDE + (CMP_LEN - 1)
    ks = kv_full[:, :, 2].reshape(bsz, n_blk, SLC_BLOCK, N_KV_HEADS, HEAD_DIM).transpose(0, 3, 1, 2, 4)
    vs = kv_full[:, :, 3].reshape(bsz, n_blk, SLC_BLOCK, N_KV_HEADS, HEAD_DIM).transpose(0, 3, 1, 2, 4)
    k_w, v_w = win_kv[:, :, 0], win_kv[:, :, 1]
    qb_len = Q_BLOCK if S % Q_BLOCK == 0 else S
    n_qb = S // qb_len
    q_blk = q.reshape(bsz, n_qb, qb_len, N_KV_HEADS, GQA, HEAD_DIM).transpose(1, 0, 3, 4, 2, 5)
    g_blk = gates.reshape(bsz, n_qb, qb_len, 3, N_KV_HEADS, GQA).transpose(1, 0, 4, 5, 2, 3)
    pos_blk = (q_pos0 + jnp.arange(S, dtype=jnp.int32)).reshape(n_qb, qb_len)
    blk_ids = jnp.arange(n_blk, dtype=jnp.int32)
    key_off = jnp.arange(SLC_BLOCK, dtype=jnp.int32)
    win_off = jnp.arange(win_len + qb_len, dtype=jnp.int32)

    def gather_blocks(blocks, idx):
        return blocks[jnp.arange(N_KV_HEADS)[:, None, None], idx]

    def one_block(args):
        qb, gb, pos, i = args
        s_c = jnp.einsum('bhgqd,bnhd->bhgqn', qb, kc) * scale
        p_c = masked_softmax(s_c, cmp_end[None, :] <= pos[:, None])
        o_c = jnp.einsum('bhgqn,bnhd->bhgqd', p_c.astype(vc.dtype), vc)
        p_grp = p_c.sum(axis=2)
        p_sub = (jnp.pad(p_grp, ((0, 0), (0, 0), (0, 0), (0, 1))) +
                 jnp.pad(p_grp, ((0, 0), (0, 0), (0, 0), (1, 0))))
        p_slc = p_sub.reshape(bsz, N_KV_HEADS, qb_len, n_blk, SLC_BLOCK // CMP_STRIDE).sum(-1)
        cur = pos // SLC_BLOCK
        forced = ((blk_ids[None] == 0) | (blk_ids[None] == cur[:, None]) | (blk_ids[None] == cur[:, None] - 1))
        future = blk_ids[None] * SLC_BLOCK > pos[:, None]
        score = jnp.where(future, -jnp.inf, jnp.where(forced, jnp.inf, p_slc))
        _, sel = lax.top_k(score, n_sel)
        k_sel = jax.vmap(gather_blocks)(ks, sel)
        v_sel = jax.vmap(gather_blocks)(vs, sel)
        key_pos = sel[..., None] * SLC_BLOCK + key_off
        m_s = (key_pos <= pos[:, None, None]).reshape(bsz, N_KV_HEADS, 1, qb_len, n_sel * SLC_BLOCK)
        s_s = jnp.einsum('bhgqd,bhqnkd->bhgqnk', qb, k_sel) * scale
        p_s = masked_softmax(s_s.reshape(bsz, N_KV_HEADS, GQA, qb_len, n_sel * SLC_BLOCK), m_s)
        o_s = jnp.einsum('bhgqnk,bhqnkd->bhgqd', p_s.reshape(s_s.shape).astype(v_sel.dtype), v_sel)
        start = i * qb_len
        kw = lax.dynamic_slice_in_dim(k_w, start, win_len + qb_len, axis=1)
        vw = lax.dynamic_slice_in_dim(v_w, start, win_len + qb_len, axis=1)
        kpos = q_pos0 - win_len + start + win_off
        dpos = pos[:, None] - kpos[None, :]
        m_w = (dpos >= 0) & (dpos < WINDOW) & (kpos[None, :] >= 0)
        s_w = jnp.einsum('bhgqd,bkhd->bhgqk', qb, kw) * scale
        p_w = masked_softmax(s_w, m_w)
        o_w = jnp.einsum('bhgqk,bkhd->bhgqd', p_w.astype(vw.dtype), vw)
        return gb[..., 0:1] * o_c + gb[..., 1:2] * o_s + gb[..., 2:3] * o_w

    out = lax.map(one_block, (q_blk, g_blk, pos_blk, jnp.arange(n_qb, dtype=jnp.int32)))
    return out.transpose(1, 0, 4, 2, 3, 5).reshape(bsz, S, N_HEADS * HEAD_DIM)


def _chunk_size(n):
    for c in (512, 256, 128):
        if n % c == 0:
            return c
    return n


def peer_ffn(x, w_q, keys, u_tab, v_tab):
    bsz, S, D = x.shape
    xt = x.reshape(-1, D)
    n_tok = xt.shape[0]
    c = _chunk_size(n_tok)

    def chunk(xc):
        qh = (xc @ w_q).reshape(c, PEER_HEADS, 2, PEER_DK // 2)
        s = jnp.einsum('thid,hikd->thik', qh, keys).astype(jnp.float32)
        s1, i1 = lax.top_k(s[:, :, 0], PEER_TOPK)
        s2, i2 = lax.top_k(s[:, :, 1], PEER_TOPK)
        cand = (s1[..., :, None] + s2[..., None, :]).reshape(c, PEER_HEADS, PEER_TOPK * PEER_TOPK)
        cidx = (i1[..., :, None] * N_KEYS + i2[..., None, :]).reshape(c, PEER_HEADS, PEER_TOPK * PEER_TOPK)
        top_s, top_pos = lax.top_k(cand, PEER_TOPK)
        eidx = jnp.take_along_axis(cidx, top_pos, axis=-1)
        g = jax.nn.softmax(top_s, axis=-1)
        hidden = jax.nn.gelu(jnp.einsum('thkd,td->thk', u_tab[eidx], xc).astype(jnp.float32))
        return jnp.einsum('thk,thkd->td', (g * hidden).astype(x.dtype), v_tab[eidx])

    out = lax.map(chunk, xt.reshape(n_tok // c, c, D))
    return out.reshape(bsz, S, D)


def hybrid_layer(x, q_pos0, kv_past, win_past, ssm_state, p):
    bsz, S, _ = x.shape
    pos = q_pos0 + jnp.arange(S, dtype=jnp.int32)
    proj = x @ p['w_in']
    o1 = D_SSM
    o2 = o1 + D_ATT
    o3 = o2 + 6 * KV_WIDTH
    u, q, kv_new, g = proj[..., :o1], proj[..., o1:o2], proj[..., o2:o3], proj[..., o3:]
    y_ssm, ssm_new = s5_mixer(u, ssm_state, p['a_re'], p['a_im'], p['log_dt'], p['b_re'], p['b_im'],
                              p['c_re'], p['c_im'], p['d'], p['w_glu'])
    q = rope(q.reshape(bsz, S, N_HEADS, HEAD_DIM), pos)
    kv_new = kv_new.reshape(bsz, S, 6, N_KV_HEADS, HEAD_DIM)
    k_rot = rope(kv_new[:, :, 0::2], pos)
    kv_rows = jnp.stack([k_rot, kv_new[:, :, 1::2]], axis=3).reshape(bsz, S, 6, N_KV_HEADS, HEAD_DIM)
    new_kv = kv_rows[:, :, :4]
    new_win = kv_rows[:, :, 4:]
    L = q_pos0 + S
    L_pad = -(-L // SLC_BLOCK) * SLC_BLOCK
    parts = ([kv_past] if kv_past is not None else []) + [new_kv]
    if L_pad > L:
        parts.append(jnp.zeros((bsz, L_pad - L, 4, N_KV_HEADS, HEAD_DIM), new_kv.dtype))
    kv_full = jnp.concatenate(parts, axis=1)
    win_all = jnp.concatenate([win_past.astype(x.dtype), new_win], axis=1)
    gates = jax.nn.sigmoid(g).reshape(bsz, S, 3, N_HEADS)
    y_att = nsa_attention(q, kv_full, win_all, gates, q_pos0, p['cmp_pe'], p['cmp_w1'], p['cmp_w2'])
    mix = jnp.concatenate([y_ssm, y_att.astype(x.dtype)], axis=-1) @ p['w_out']
    h = layer_norm(DN_ALPHA * x + mix, p['ln1_g'], p['ln1_b'])
    y = layer_norm(DN_ALPHA * h + peer_ffn(h, p['peer_wq'], p['peer_keys'], p['peer_u'], p['peer_v']),
                   p['ln2_g'], p['ln2_b'])
    return y, new_kv, win_all[:, -win_past.shape[1]:], ssm_new


def setup_inputs(seed: int = 0) -> dict:
    key = jax.random.key(seed)
    ks = jax.random.split(key, 32)
    f32 = jnp.float32
    n_pages = PAST_LEN // PAGE_SIZE
    n_used = DEC_BATCH * n_pages
    n_pool = (5 * n_used) // 4
    win_buf = min(WINDOW, PAST_LEN)

    def nrm(k, shape, s):
        return s * jax.random.normal(k, shape, f32)

    G, P = N_SSM_GROUPS, SSM_STATE
    return {
        'x_prompt': nrm(ks[0], (BATCH, SEQ, D_MODEL), 1.0),
        'x_sample': nrm(ks[1], (DEC_BATCH, DEC_SEQ, D_MODEL), 1.0),
        'cache_kv': nrm(ks[2], (DEPTH, n_pool, PAGE_SIZE, 4, N_KV_HEADS, HEAD_DIM), 1.0),
        'cache_win': nrm(ks[3], (DEPTH, DEC_BATCH, win_buf, 2, N_KV_HEADS, HEAD_DIM), 1.0),
        'state_ssm': nrm(ks[4], (DEPTH, DEC_BATCH, 2, G, P), 0.1),
        'page_table': jax.random.permutation(ks[5], n_pool)[:n_used].reshape(DEC_BATCH, n_pages).astype(jnp.int32),
        'w_in': nrm(ks[6], (DEPTH, D_MODEL, D_IN), D_MODEL ** -0.5),
        'ssm_a_re': -0.5 + nrm(ks[7], (DEPTH, G, P), 0.01),
        'ssm_a_im': math.pi * jnp.arange(P, dtype=f32) + nrm(ks[8], (DEPTH, G, P), 0.01),
        'ssm_log_dt': jax.random.uniform(ks[9], (DEPTH, G), f32, math.log(1e-3), math.log(1e-1)),
        'ssm_b_re': nrm(ks[10], (DEPTH, G, P, SSM_CH), (2 * SSM_CH) ** -0.5),
        'ssm_b_im': nrm(ks[11], (DEPTH, G, P, SSM_CH), (2 * SSM_CH) ** -0.5),
        'ssm_c_re': nrm(ks[12], (DEPTH, G, SSM_CH, P), (2 * P) ** -0.5),
        'ssm_c_im': nrm(ks[13], (DEPTH, G, SSM_CH, P), (2 * P) ** -0.5),
        'ssm_d': nrm(ks[14], (DEPTH, D_SSM), 1.0),
        'w_glu': nrm(ks[15], (DEPTH, D_SSM, 2 * D_SSM), D_SSM ** -0.5),
        'cmp_pe': nrm(ks[16], (DEPTH, 2, CMP_LEN, HEAD_DIM), 0.02),
        'cmp_w1': nrm(ks[17], (DEPTH, 2, CMP_LEN, HEAD_DIM, HEAD_DIM), (CMP_LEN * HEAD_DIM) ** -0.5),
        'cmp_w2': nrm(ks[18], (DEPTH, 2, HEAD_DIM, HEAD_DIM), HEAD_DIM ** -0.5),
        'w_out': nrm(ks[19], (DEPTH, D_MIX, D_MODEL), DN_BETA * D_MIX ** -0.5),
        'ln1_g': 1.0 + nrm(ks[20], (DEPTH, D_MODEL), 0.02),
        'ln1_b': nrm(ks[21], (DEPTH, D_MODEL), 0.02),
        'peer_wq': nrm(ks[22], (DEPTH, D_MODEL, PEER_HEADS * PEER_DK), D_MODEL ** -0.5),
        'peer_keys': nrm(ks[23], (DEPTH, PEER_HEADS, 2, N_KEYS, PEER_DK // 2), (PEER_DK // 2) ** -0.5),
        'peer_u': nrm(ks[24], (DEPTH, N_EXPERTS, D_MODEL), D_MODEL ** -0.5),
        'peer_v': nrm(ks[25], (DEPTH, N_EXPERTS, D_MODEL), DN_BETA * PEER_HEADS ** -0.5),
        'ln2_g': 1.0 + nrm(ks[26], (DEPTH, D_MODEL), 0.02),
        'ln2_b': nrm(ks[27], (DEPTH, D_MODEL), 0.02),
    }


def reference(x_prompt, x_sample, cache_kv, cache_win, state_ssm, page_table, w_in, ssm_a_re, ssm_a_im,
              ssm_log_dt, ssm_b_re, ssm_b_im, ssm_c_re, ssm_c_im, ssm_d, w_glu, cmp_pe, cmp_w1, cmp_w2,
              w_out, ln1_g, ln1_b, peer_wq, peer_keys, peer_u, peer_v, ln2_g, ln2_b):
    n_pages = PAST_LEN // PAGE_SIZE
    y_p, y_s = x_prompt, x_sample
    kv_p, win_p, ssm_p, kv_s, win_s, ssm_s = [], [], [], [], [], []
    for l in range(DEPTH):
        p = {'w_in': w_in[l], 'a_re': ssm_a_re[l], 'a_im': ssm_a_im[l], 'log_dt': ssm_log_dt[l],
             'b_re': ssm_b_re[l], 'b_im': ssm_b_im[l], 'c_re': ssm_c_re[l], 'c_im': ssm_c_im[l],
             'd': ssm_d[l], 'w_glu': w_glu[l], 'cmp_pe': cmp_pe[l], 'cmp_w1': cmp_w1[l], 'cmp_w2': cmp_w2[l],
             'w_out': w_out[l], 'ln1_g': ln1_g[l], 'ln1_b': ln1_b[l], 'peer_wq': peer_wq[l],
             'peer_keys': peer_keys[l], 'peer_u': peer_u[l], 'peer_v': peer_v[l],
             'ln2_g': ln2_g[l], 'ln2_b': ln2_b[l]}
        win0 = jnp.zeros((BATCH, WINDOW, 2, N_KV_HEADS, HEAD_DIM), x_prompt.dtype)
        s0 = jnp.zeros((BATCH, 2, N_SSM_GROUPS, SSM_STATE), x_prompt.dtype)
        y_p, kvn, winn, sn = hybrid_layer(y_p, 0, None, win0, s0, p)
        kv_p.append(kvn)
        win_p.append(winn)
        ssm_p.append(sn)
        past = cache_kv[l][page_table].reshape(DEC_BATCH, n_pages * PAGE_SIZE, 4, N_KV_HEADS, HEAD_DIM)
        y_s, kvn, winn, sn = hybrid_layer(y_s, PAST_LEN, past, cache_win[l], state_ssm[l], p)
        kv_s.append(kvn)
        win_s.append(winn)
        ssm_s.append(sn)
    return (y_p, y_s, jnp.stack(kv_p), jnp.stack(win_p), jnp.stack(ssm_p),
            jnp.stack(kv_s), jnp.stack(win_s), jnp.stack(ssm_s))
```

```python
import functools
import math

import jax
import jax.numpy as jnp
from jax import lax
from jax.experimental import pallas as pl
from jax.experimental.pallas import tpu as pltpu

F32 = jnp.float32
BF16 = jnp.bfloat16
I32 = jnp.int32

D_MODEL = 1024
D_SSM = 512
D_ATT = 512
SSM_CH = 16
N_GROUPS = 32
SSM_STATE = 64
N_STATE = N_GROUPS * SSM_STATE
HEAD_DIM = 64
N_HEADS = 8
N_KV_HEADS = 2
GQA = 4
KV_WIDTH = 128
CMP_STRIDE = 16
CMP_LEN = 32
SLC_BLOCK = 64
N_SLC = 16
WINDOW = 512
Q_BLOCK = 64
ROPE_THETA = 10000.0
PEER_HEADS = 8
PEER_DK = 256
N_KEYS = 128
PEER_TOPK = 16
PAGE_SIZE = 128
DN_ALPHA = 4 ** 0.25
LN_EPS = 1e-5
ATT_SCALE = HEAD_DIM ** -0.5
TINY = float(jnp.finfo(jnp.float32).tiny)
NEG_INF = float("-inf")
POS_INF = float("inf")

LANES = 128
PROJ_W = 512 + 6 * KV_WIDTH + LANES
VMEM_LIMIT = 56 * 1024 * 1024
PEER_TB = 128
PEER_SLOTS = 4
PEER_ROWS = PEER_HEADS * PEER_TOPK


def _params(sem, **kw):
    return pltpu.CompilerParams(dimension_semantics=sem, vmem_limit_bytes=VMEM_LIMIT, **kw)


def _dot(a, b):
    return jnp.dot(a, b, preferred_element_type=F32)


def _dot_nt(a, b):
    return lax.dot_general(a, b, (((1,), (1,)), ((), ())), preferred_element_type=F32)


def _dot_exact(a, b):
    return lax.dot_general(a, b, (((1,), (0,)), ((), ())), precision=lax.Precision.HIGHEST,
                           preferred_element_type=F32)


def _layer_norm(x, g, b):
    mu = jnp.mean(x, axis=-1, keepdims=True)
    xc = x - mu
    var = jnp.mean(xc * xc, axis=-1, keepdims=True)
    return xc * lax.rsqrt(var + LN_EPS) * g + b


def _masked_softmax(s, valid):
    s = jnp.where(valid, s, NEG_INF)
    m = jnp.max(s, axis=-1, keepdims=True)
    m = jnp.where(m > NEG_INF, m, 0.0)
    e = jnp.where(valid, jnp.exp(s - m), 0.0)
    den = jnp.maximum(jnp.sum(e, axis=-1, keepdims=True), TINY)
    return e / den


def _mm_kernel(x_ref, w_ref, o_ref):
    o_ref[...] = _dot(x_ref[...].astype(BF16), w_ref[...])


def _matmul(x, w, tm):
    n, k = x.shape
    m = w.shape[1]
    return pl.pallas_call(
        _mm_kernel, grid=(n // tm,),
        in_specs=[pl.BlockSpec((tm, k), lambda i: (i, 0)), pl.BlockSpec((k, m), lambda i: (0, 0))],
        out_specs=pl.BlockSpec((tm, m), lambda i: (i, 0)),
        out_shape=jax.ShapeDtypeStruct((n, m), F32),
        compiler_params=_params(("parallel",)), name="mm")(x, w)


def _proj_kernel(x_ref, w_ref, cos_ref, sin_ref, q_ref, kv_ref, g_ref):
    acc = _dot(x_ref[...].astype(BF16), w_ref[...])
    cos = cos_ref[...]
    sin = sin_ref[...]
    lane = lax.broadcasted_iota(I32, cos.shape, 1)
    first_half = (lane % HEAD_DIM) < (HEAD_DIM // 2)

    def rope(v):
        rot = jnp.where(first_half, pltpu.roll(v, 96, 1), pltpu.roll(v, 32, 1))
        return v * cos + rot * sin

    for j in range(4):
        q_ref[:, j * LANES:(j + 1) * LANES] = rope(acc[:, j * LANES:(j + 1) * LANES])
    for j in range(6):
        blk = acc[:, 512 + j * LANES:512 + (j + 1) * LANES]
        kv_ref[:, j * LANES:(j + 1) * LANES] = rope(blk) if j % 2 == 0 else blk
    g_ref[...] = jax.nn.sigmoid(acc[:, 512 + 6 * LANES:])


def _project(x, w, cos, sin, tm):
    n = x.shape[0]
    tab_blocks = cos.shape[0] // tm
    return pl.pallas_call(
        _proj_kernel, grid=(n // tm,),
        in_specs=[pl.BlockSpec((tm, D_MODEL), lambda i: (i, 0)),
                  pl.BlockSpec((D_MODEL, PROJ_W), lambda i: (0, 0)),
                  pl.BlockSpec((tm, LANES), lambda i: (i % tab_blocks, 0)),
                  pl.BlockSpec((tm, LANES), lambda i: (i % tab_blocks, 0))],
        out_specs=[pl.BlockSpec((tm, 512), lambda i: (i, 0)),
                   pl.BlockSpec((tm, 6 * KV_WIDTH), lambda i: (i, 0)),
                   pl.BlockSpec((tm, LANES), lambda i: (i, 0))],
        out_shape=[jax.ShapeDtypeStruct((n, 512), F32),
                   jax.ShapeDtypeStruct((n, 6 * KV_WIDTH), F32),
                   jax.ShapeDtypeStruct((n, LANES), F32)],
        compiler_params=_params(("parallel",)), name="proj")(x, w, cos, sin)


def _glu_kernel(a_ref, w_ref, o_ref):
    gl = _dot(a_ref[...].astype(BF16), w_ref[...])
    o_ref[...] = gl[:, :D_SSM] * jax.nn.sigmoid(gl[:, D_SSM:])


def _glu(a, w, tm):
    n = a.shape[0]
    return pl.pallas_call(
        _glu_kernel, grid=(n // tm,),
        in_specs=[pl.BlockSpec((tm, D_SSM), lambda i: (i, 0)),
                  pl.BlockSpec((D_SSM, 2 * D_SSM), lambda i: (0, 0))],
        out_specs=pl.BlockSpec((tm, D_SSM), lambda i: (i, 0)),
        out_shape=jax.ShapeDtypeStruct((n, D_SSM), F32),
        compiler_params=_params(("parallel",)), name="glu")(a, w)


def _outproj_kernel(ys_ref, ya_ref, x_ref, w1_ref, w2_ref, g_ref, b_ref, o_ref):
    mix = _dot(ys_ref[...].astype(BF16), w1_ref[...]) + _dot(ya_ref[...].astype(BF16), w2_ref[...])
    o_ref[...] = _layer_norm(DN_ALPHA * x_ref[...] + mix, g_ref[...], b_ref[...])


def _outproj_ln(ys, ya, x, w1, w2, g, b, tm):
    n = x.shape[0]
    row = lambda i: (i, 0)
    fix = lambda i: (0, 0)
    return pl.pallas_call(
        _outproj_kernel, grid=(n // tm,),
        in_specs=[pl.BlockSpec((tm, D_SSM), row), pl.BlockSpec((tm, D_ATT), row),
                  pl.BlockSpec((tm, D_MODEL), row),
                  pl.BlockSpec((D_SSM, D_MODEL), fix), pl.BlockSpec((D_ATT, D_MODEL), fix),
                  pl.BlockSpec((1, D_MODEL), fix), pl.BlockSpec((1, D_MODEL), fix)],
        out_specs=pl.BlockSpec((tm, D_MODEL), row),
        out_shape=jax.ShapeDtypeStruct((n, D_MODEL), F32),
        compiler_params=_params(("parallel",)), name="outproj_ln")(ys, ya, x, w1, w2, g, b)


def _s5_kernel(u_ref, wb_ref, wc_ref, ar_ref, ai_ref, d_ref, s0_ref, gy_ref, sf_ref, bu_ref, st_ref,
               *, n_steps, batch):
    @pl.when(pl.program_id(0) == 0)
    def _():
        st_ref[...] = s0_ref[...]

    u = u_ref[...]
    bu_ref[...] = _dot(u.astype(BF16), wb_ref[...])
    n_chunk = 4
    cw = N_STATE // n_chunk

    def step(t, carry):
        r = pl.multiple_of(t * batch, batch)
        new = []
        for c in range(n_chunk):
            sr, si = carry[c], carry[n_chunk + c]
            ar = ar_ref[:, c * cw:(c + 1) * cw]
            ai = ai_ref[:, c * cw:(c + 1) * cw]
            nr = ar * sr - ai * si + bu_ref[pl.ds(r, batch), c * cw:(c + 1) * cw]
            ni = ar * si + ai * sr + bu_ref[pl.ds(r, batch), N_STATE + c * cw:N_STATE + (c + 1) * cw]
            bu_ref[pl.ds(r, batch), c * cw:(c + 1) * cw] = nr
            bu_ref[pl.ds(r, batch), N_STATE + c * cw:N_STATE + (c + 1) * cw] = ni
            new.append((nr, ni))
        return tuple(p[0] for p in new) + tuple(p[1] for p in new)

    init = tuple(st_ref[:, c * cw:(c + 1) * cw] for c in range(n_chunk)) + \
        tuple(st_ref[:, N_STATE + c * cw:N_STATE + (c + 1) * cw] for c in range(n_chunk))
    fin = lax.fori_loop(0, n_steps, step, init)
    for c in range(n_chunk):
        st_ref[:, c * cw:(c + 1) * cw] = fin[c]
        st_ref[:, N_STATE + c * cw:N_STATE + (c + 1) * cw] = fin[n_chunk + c]
    y = _dot(bu_ref[...].astype(BF16), wc_ref[...]) + d_ref[...] * u
    gy_ref[...] = jax.nn.gelu(y)
    sf_ref[...] = st_ref[...]


def _s5_scan(u_tb, wb, wc, ar, ai, d, s0, n_steps):
    batch = s0.shape[0]
    n = u_tb.shape[0]
    rows = n_steps * batch
    fix = lambda i: (0, 0)
    return pl.pallas_call(
        functools.partial(_s5_kernel, n_steps=n_steps, batch=batch),
        grid=(n // rows,),
        in_specs=[pl.BlockSpec((rows, D_SSM), lambda i: (i, 0)),
                  pl.BlockSpec((D_SSM, 2 * N_STATE), fix), pl.BlockSpec((2 * N_STATE, D_SSM), fix),
                  pl.BlockSpec((batch, N_STATE), fix), pl.BlockSpec((batch, N_STATE), fix),
                  pl.BlockSpec((1, D_SSM), fix), pl.BlockSpec((batch, 2 * N_STATE), fix)],
        out_specs=[pl.BlockSpec((rows, D_SSM), lambda i: (i, 0)),
                   pl.BlockSpec((batch, 2 * N_STATE), fix)],
        out_shape=[jax.ShapeDtypeStruct((n, D_SSM), F32),
                   jax.ShapeDtypeStruct((batch, 2 * N_STATE), F32)],
        scratch_shapes=[pltpu.VMEM((rows, 2 * N_STATE), F32), pltpu.VMEM((batch, 2 * N_STATE), F32)],
        compiler_params=_params(("arbitrary",)), name="s5_scan")(u_tb, wb, wc, ar, ai, d, s0)


def _s5_weights(a_re, a_im, log_dt, b_re, b_im, c_re, c_im, batch):
    dt = jnp.exp(log_dt)[:, None]
    mag = jnp.exp(dt * a_re)
    abar_re, abar_im = mag * jnp.cos(dt * a_im), mag * jnp.sin(dt * a_im)
    den = a_re * a_re + a_im * a_im
    f_re = ((abar_re - 1.0) * a_re + abar_im * a_im) / den
    f_im = (abar_im * a_re - (abar_re - 1.0) * a_im) / den
    bb_re = f_re[..., None] * b_re - f_im[..., None] * b_im
    bb_im = f_re[..., None] * b_im + f_im[..., None] * b_re
    eye = jnp.eye(N_GROUPS, dtype=F32)
    wb_re = jnp.einsum('gpc,gh->gchp', bb_re, eye).reshape(D_SSM, N_STATE)
    wb_im = jnp.einsum('gpc,gh->gchp', bb_im, eye).reshape(D_SSM, N_STATE)
    wb = jnp.concatenate([wb_re, wb_im], axis=1).astype(BF16)
    wc_re = jnp.einsum('gcp,gh->gphc', c_re, eye).reshape(N_STATE, D_SSM)
    wc_im = jnp.einsum('gcp,gh->gphc', c_im, eye).reshape(N_STATE, D_SSM)
    wc = jnp.concatenate([wc_re, -wc_im], axis=0).astype(BF16)
    ar = jnp.broadcast_to(abar_re.reshape(1, N_STATE), (batch, N_STATE))
    ai = jnp.broadcast_to(abar_im.reshape(1, N_STATE), (batch, N_STATE))
    return wb, wc, ar, ai


def _compress_kernel(x_ref, pe_ref, w1a_ref, w1b_ref, w2_ref, o_ref):
    x = x_ref[...]
    n = x.shape[0]
    first = _dot((x + pe_ref[0:1, :]).astype(BF16), w1a_ref[...])
    second = _dot((x + pe_ref[1:2, :]).astype(BF16), w1b_ref[...])
    nxt = pltpu.roll(second, n - 1, 0)
    o_ref[...] = _dot(jax.nn.gelu(first + nxt).astype(BF16), w2_ref[...])


def _compress(x, pe, w1a, w1b, w2):
    bsz, n, width = x.shape
    fix = lambda b: (0, 0)
    return pl.pallas_call(
        _compress_kernel, grid=(bsz,),
        in_specs=[pl.BlockSpec((None, n, width), lambda b: (b, 0, 0)),
                  pl.BlockSpec((2, width), fix), pl.BlockSpec((width, KV_WIDTH), fix),
                  pl.BlockSpec((width, KV_WIDTH), fix), pl.BlockSpec((KV_WIDTH, KV_WIDTH), fix)],
        out_specs=pl.BlockSpec((None, n, KV_WIDTH), lambda b: (b, 0, 0)),
        out_shape=jax.ShapeDtypeStruct((bsz, n, KV_WIDTH), F32),
        compiler_params=_params(("parallel",)), name="compress")(x, pe, w1a, w1b, w2)


def _compress_weights(pe, w1, w2):
    eye = jnp.eye(N_KV_HEADS, dtype=F32)
    w1 = w1.reshape(2, CMP_STRIDE, HEAD_DIM, HEAD_DIM)
    pe = pe.reshape(2, CMP_STRIDE, HEAD_DIM)
    w1a = jnp.einsum('jde,hk->jhdke', w1[0], eye).reshape(CMP_STRIDE * KV_WIDTH, KV_WIDTH).astype(BF16)
    w1b = jnp.einsum('jde,hk->jhdke', w1[1], eye).reshape(CMP_STRIDE * KV_WIDTH, KV_WIDTH).astype(BF16)
    w2d = jnp.einsum('de,hk->hdke', w2, eye).reshape(KV_WIDTH, KV_WIDTH).astype(BF16)
    pe2 = jnp.broadcast_to(pe[:, :, None, :], (2, CMP_STRIDE, N_KV_HEADS, HEAD_DIM)).reshape(2, CMP_STRIDE * KV_WIDTH)
    return pe2, w1a, w1b, w2d


def _sel_matrix(n_cmp_rows, n_blk_cols):
    n = jnp.arange(n_cmp_rows)[:, None]
    j = jnp.arange(n_blk_cols)[None, :]
    per = SLC_BLOCK // CMP_STRIDE
    m = (n // per == j).astype(F32) + ((n + 1) // per == j).astype(F32)
    return jnp.where(n < n_cmp_rows - 1, m, 0.0)


def _block_ranks(score, n_blk):
    blk = lax.broadcasted_iota(I32, score.shape, 1)
    rank = jnp.zeros(score.shape, I32)
    for i in range(n_blk):
        col = score[:, i:i + 1]
        before = (col > score) | ((col == score) & (blk > i))
        rank = rank + before.astype(I32)
    return rank


def _attn_kernel(q_ref, g_ref, kc_ref, vc_ref, ks_ref, vs_ref, kw_ref, vw_ref, msel_ref, e3_ref, eg_ref,
                 o_ref, *, seq, ck, wk):
    qb = pl.program_id(1)
    q0 = qb * Q_BLOCK
    n_cmp = seq // CMP_STRIDE
    n_blk = seq // SLC_BLOCK
    n_sel = min(N_SLC, n_blk)
    rows = GQA * Q_BLOCK
    pos = q0 + lax.broadcasted_iota(I32, (Q_BLOCK, 1), 0)
    pos4 = jnp.concatenate([pos] * GQA, axis=0)
    lane_r = lax.broadcasted_iota(I32, (rows, LANES), 1)
    kc = kc_ref[...].astype(BF16)
    vc = vc_ref[...].astype(BF16)
    n_chunks = (q0 + Q_BLOCK + ck - 1) // ck
    kstart = pl.multiple_of(jnp.maximum(q0 + Q_BLOCK - wk, 0), Q_BLOCK)
    kw = kw_ref[pl.ds(kstart, wk), :].astype(BF16)
    vw = vw_ref[pl.ds(kstart, wk), :].astype(BF16)
    heads_c, heads_s, heads_w = [], [], []
    for hkv in range(N_KV_HEADS):
        parts = []
        for g in range(GQA):
            hd = hkv * GQA + g
            slab = q_ref[:, (hd // 2) * LANES:(hd // 2 + 1) * LANES]
            if hd % 2 != hkv:
                slab = pltpu.roll(slab, HEAD_DIM, 1)
            parts.append(slab)
        qh = jnp.concatenate(parts, axis=0)
        qh = jnp.where(lane_r // HEAD_DIM == hkv, qh * ATT_SCALE, 0.0).astype(BF16)

        s_c = _dot_nt(qh, kc)
        n_id = lax.broadcasted_iota(I32, s_c.shape, 1)
        p_c = _masked_softmax(s_c, n_id * CMP_STRIDE + (CMP_LEN - 1) <= pos4)
        o_c = _dot(p_c.astype(BF16), vc)
        p_grp = p_c[0:Q_BLOCK]
        for g in range(1, GQA):
            p_grp = p_grp + p_c[g * Q_BLOCK:(g + 1) * Q_BLOCK]
        p_slc = _dot_exact(p_grp, msel_ref[...])

        blk = lax.broadcasted_iota(I32, p_slc.shape, 1)
        cur = pos // SLC_BLOCK
        forced = (blk == 0) | (blk == cur) | (blk == cur - 1)
        future = blk * SLC_BLOCK > pos
        score = jnp.where(future, NEG_INF, jnp.where(forced, POS_INF, p_slc))
        sel = (_block_ranks(score, n_blk) < n_sel).astype(BF16)

        def chunk(c, carry):
            m, l, acc = carry
            base = pl.multiple_of(c * ck, ck)
            k = ks_ref[pl.ds(base, ck), :].astype(BF16)
            v = vs_ref[pl.ds(base, ck), :].astype(BF16)
            s = _dot_nt(qh, k)
            hit = _dot(sel, e3_ref[c])
            kpos = base + lax.broadcasted_iota(I32, hit.shape, 1)
            ok = (hit > 0.5) & (kpos <= pos)
            ok4 = jnp.concatenate([ok] * GQA, axis=0)
            s = jnp.where(ok4, s, NEG_INF)
            m_new = jnp.maximum(m, jnp.max(s, axis=-1, keepdims=True))
            m_use = jnp.where(m_new > NEG_INF, m_new, 0.0)
            alpha = jnp.exp(m - m_use)
            e = jnp.where(ok4, jnp.exp(s - m_use), 0.0)
            l = alpha * l + jnp.sum(e, axis=-1, keepdims=True)
            acc = alpha * acc + _dot(e.astype(BF16), v)
            return m_new, l, acc

        init = (jnp.full((rows, 1), NEG_INF, F32), jnp.zeros((rows, 1), F32), jnp.zeros((rows, LANES), F32))
        _, l_s, acc_s = lax.fori_loop(0, n_chunks, chunk, init)
        o_s = acc_s / jnp.maximum(l_s, TINY)

        s_w = _dot_nt(qh, kw)
        dpos = pos4 - (kstart + lax.broadcasted_iota(I32, s_w.shape, 1))
        p_w = _masked_softmax(s_w, (dpos >= 0) & (dpos < WINDOW))
        o_w = _dot(p_w.astype(BF16), vw)

        for g in range(GQA):
            sl = (slice(g * Q_BLOCK, (g + 1) * Q_BLOCK), slice(hkv * HEAD_DIM, (hkv + 1) * HEAD_DIM))
            heads_c.append(o_c[sl])
            heads_s.append(o_s[sl])
            heads_w.append(o_w[sl])

    gates = g_ref[...]
    out = jnp.zeros((Q_BLOCK, D_ATT), F32)
    for t, heads in enumerate((heads_c, heads_s, heads_w)):
        out = out + _dot_exact(gates, eg_ref[t]) * jnp.concatenate(heads, axis=1)
    o_ref[...] = out


def _gate_expand():
    c = jnp.arange(LANES)[None, :, None]
    t = jnp.arange(3)[:, None, None]
    h = (jnp.arange(D_ATT) // HEAD_DIM)[None, None, :]
    return (c == t * N_HEADS + h).astype(F32)


def _attention(q, gates, kv, kc, vc, bsz, seq):
    ck = min(512, seq)
    wk = min(WINDOW + Q_BLOCK, seq)
    n_qb = seq // Q_BLOCK
    n_cmp = seq // CMP_STRIDE
    n_blk = seq // SLC_BLOCK
    msel = _sel_matrix(n_cmp, n_blk)
    key_blk = (jnp.arange(seq) // SLC_BLOCK).reshape(seq // ck, 1, ck)
    e3 = (key_blk == jnp.arange(n_blk)[None, :, None]).astype(BF16)
    eg = _gate_expand()
    fix2 = lambda b, i: (0, 0)
    fix3 = lambda b, i: (0, 0, 0)
    kv_spec = lambda col: pl.BlockSpec((seq, KV_WIDTH), lambda b, i: (b, col))
    return pl.pallas_call(
        functools.partial(_attn_kernel, seq=seq, ck=ck, wk=wk),
        grid=(bsz, n_qb),
        in_specs=[pl.BlockSpec((Q_BLOCK, D_ATT), lambda b, i: (b * n_qb + i, 0)),
                  pl.BlockSpec((Q_BLOCK, LANES), lambda b, i: (b * n_qb + i, 0)),
                  pl.BlockSpec((None, n_cmp, KV_WIDTH), lambda b, i: (b, 0, 0)),
                  pl.BlockSpec((None, n_cmp, KV_WIDTH), lambda b, i: (b, 0, 0)),
                  kv_spec(2), kv_spec(3), kv_spec(4), kv_spec(5),
                  pl.BlockSpec((n_cmp, n_blk), fix2),
                  pl.BlockSpec((seq // ck, n_blk, ck), fix3),
                  pl.BlockSpec((3, LANES, D_ATT), fix3)],
        out_specs=pl.BlockSpec((Q_BLOCK, D_ATT), lambda b, i: (b * n_qb + i, 0)),
        out_shape=jax.ShapeDtypeStruct((bsz * seq, D_ATT), F32),
        compiler_params=_params(("parallel", "arbitrary")), name="nsa_prompt",
    )(q, gates, kc, vc, kv, kv, kv, kv, msel, e3, eg)


def _page_gather_kernel(pt_ref, page_ref, xk_ref, xv_ref, ks_ref, vs_ref):
    del pt_ref
    per_page = PAGE_SIZE // CMP_STRIDE
    for j in range(CMP_STRIDE):
        xk_ref[:, j * KV_WIDTH:(j + 1) * KV_WIDTH] = page_ref[pl.ds(4 * j, per_page, stride=4 * CMP_STRIDE), :]
        xv_ref[:, j * KV_WIDTH:(j + 1) * KV_WIDTH] = page_ref[pl.ds(4 * j + 1, per_page, stride=4 * CMP_STRIDE), :]
    ks_ref[...] = page_ref[pl.ds(2, PAGE_SIZE, stride=4), :]
    vs_ref[...] = page_ref[pl.ds(3, PAGE_SIZE, stride=4), :]


def _page_gather(cache, page_table, layer):
    bsz, n_pages = page_table.shape
    past = n_pages * PAGE_SIZE
    per_page = PAGE_SIZE // CMP_STRIDE
    width = CMP_STRIDE * KV_WIDTH
    grid_spec = pltpu.PrefetchScalarGridSpec(
        num_scalar_prefetch=1, grid=(bsz, n_pages),
        in_specs=[pl.BlockSpec((None, None, 4 * PAGE_SIZE, KV_WIDTH), lambda b, p, pt: (layer, pt[b, p], 0, 0))],
        out_specs=[pl.BlockSpec((None, per_page, width), lambda b, p, pt: (b, p, 0)),
                   pl.BlockSpec((None, per_page, width), lambda b, p, pt: (b, p, 0)),
                   pl.BlockSpec((None, PAGE_SIZE, KV_WIDTH), lambda b, p, pt: (b, p, 0)),
                   pl.BlockSpec((None, PAGE_SIZE, KV_WIDTH), lambda b, p, pt: (b, p, 0))])
    return pl.pallas_call(
        _page_gather_kernel, grid_spec=grid_spec,
        out_shape=[jax.ShapeDtypeStruct((bsz, past // CMP_STRIDE, width), F32),
                   jax.ShapeDtypeStruct((bsz, past // CMP_STRIDE, width), F32),
                   jax.ShapeDtypeStruct((bsz, past, KV_WIDTH), F32),
                   jax.ShapeDtypeStruct((bsz, past, KV_WIDTH), F32)],
        compiler_params=_params(("parallel", "arbitrary")), name="page_gather")(page_table, cache)


def _dec_attn_kernel(q_ref, g_ref, nkv_ref, kc_ref, vc_ref, ks_ref, vs_ref, win_ref, msel_ref, e_ref, eg_ref,
                     o_ref, *, past, cb):
    pos = past
    n_blk = past // SLC_BLOCK + 1
    n_sel = min(N_SLC, n_blk)
    n_chunks = (past // SLC_BLOCK) // cb
    ckeys = cb * SLC_BLOCK
    win_len = win_ref.shape[0]
    row = lax.broadcasted_iota(I32, (N_HEADS, LANES), 0)
    lane = lax.broadcasted_iota(I32, (N_HEADS, LANES), 1)

    qrow = jnp.broadcast_to(q_ref[...], (N_HEADS, D_ATT))
    x = jnp.zeros((N_HEADS, LANES), F32)
    for j in range(4):
        x = jnp.where(row // 2 == j, qrow[:, j * LANES:(j + 1) * LANES], x)
    x = jnp.where((row % 2) != (row // GQA), pltpu.roll(x, HEAD_DIM, 1), x)
    q8 = jnp.where(lane // HEAD_DIM == row // GQA, x * ATT_SCALE, 0.0).astype(BF16)
    nkv = nkv_ref[...]

    def new_rows(col):
        return jnp.broadcast_to(nkv[:, col * KV_WIDTH:(col + 1) * KV_WIDTH], (N_HEADS, KV_WIDTH)).astype(BF16)

    s_c = _dot_nt(q8, kc_ref[...].astype(BF16))
    n_id = lax.broadcasted_iota(I32, s_c.shape, 1)
    p_c = _masked_softmax(s_c, n_id * CMP_STRIDE + (CMP_LEN - 1) <= pos)
    o_c = _dot(p_c.astype(BF16), vc_ref[...].astype(BF16))
    grp0 = jnp.sum(p_c[0:GQA], axis=0, keepdims=True)
    grp1 = jnp.sum(p_c[GQA:2 * GQA], axis=0, keepdims=True)
    p_grp = jnp.where(lax.broadcasted_iota(I32, p_c.shape, 0) < GQA, grp0, grp1)
    p_slc = _dot_exact(p_grp, msel_ref[...])
    blk = lax.broadcasted_iota(I32, p_slc.shape, 1)
    cur = pos // SLC_BLOCK
    forced = (blk == 0) | (blk == cur) | (blk == cur - 1)
    future = (blk * SLC_BLOCK > pos) | (blk >= n_blk)
    score = jnp.where(future, NEG_INF, jnp.where(forced, POS_INF, p_slc))
    sel = (_block_ranks(score, n_blk) < n_sel).astype(F32)

    k_new = new_rows(2)
    v_new = new_rows(3)
    ok_new = sel[:, cur:cur + 1] > 0.5
    s_new = jnp.where(ok_new, _dot_nt(q8, k_new)[:, 0:1], NEG_INF)
    scores, oks = [], []
    m = s_new
    for c in range(n_chunks):
        s = _dot_nt(q8, ks_ref[c * ckeys:(c + 1) * ckeys, :].astype(BF16))
        ok = _dot(sel[:, c * cb:(c + 1) * cb].astype(BF16), e_ref[...]) > 0.5
        s = jnp.where(ok, s, NEG_INF)
        m = jnp.maximum(m, jnp.max(s, axis=-1, keepdims=True))
        scores.append(s)
        oks.append(ok)
    m = jnp.where(m > NEG_INF, m, 0.0)
    e_new = jnp.where(ok_new, jnp.exp(s_new - m), 0.0)
    den = e_new
    acc = e_new.astype(BF16).astype(F32) * v_new.astype(F32)
    for c in range(n_chunks):
        e = jnp.where(oks[c], jnp.exp(scores[c] - m), 0.0)
        den = den + jnp.sum(e, axis=-1, keepdims=True)
        acc = acc + _dot(e.astype(BF16), vs_ref[c * ckeys:(c + 1) * ckeys, :].astype(BF16))
    o_s = acc / jnp.maximum(den, TINY)

    kw = win_ref[:, 0:KV_WIDTH].astype(BF16)
    vw = win_ref[:, KV_WIDTH:2 * KV_WIDTH].astype(BF16)
    s_w = _dot_nt(q8, kw)
    dist = win_len - lax.broadcasted_iota(I32, s_w.shape, 1)
    ok_w = (dist < WINDOW) & (pos - dist >= 0)
    s_w = jnp.where(ok_w, s_w, NEG_INF)
    s_wn = _dot_nt(q8, new_rows(4))[:, 0:1]
    m_w = jnp.maximum(jnp.max(s_w, axis=-1, keepdims=True), s_wn)
    e_w = jnp.where(ok_w, jnp.exp(s_w - m_w), 0.0)
    e_wn = jnp.exp(s_wn - m_w)
    den_w = jnp.sum(e_w, axis=-1, keepdims=True) + e_wn
    o_w = (_dot(e_w.astype(BF16), vw) + e_wn.astype(BF16).astype(F32) * new_rows(5).astype(F32)) / den_w

    gates = g_ref[...]
    out = jnp.zeros((1, D_ATT), F32)
    for t, o8 in enumerate((o_c, o_s, o_w)):
        pieces = [o8[hd:hd + 1, (hd // GQA) * HEAD_DIM:(hd // GQA + 1) * HEAD_DIM] for hd in range(N_HEADS)]
        g8 = jnp.broadcast_to(gates, (N_HEADS, LANES))
        out = out + _dot_exact(g8, eg_ref[t])[0:1] * jnp.concatenate(pieces, axis=1)
    o_ref[...] = out


def _decode_attention(q, gates, nkv, kc, vc, ks, vs, win, past):
    bsz = q.shape[0]
    n_cmp = kc.shape[1]
    n_past_blk = past // SLC_BLOCK
    cb = min(64, n_past_blk)
    n_blk_pad = -(-(n_past_blk + 1) // LANES) * LANES
    msel = _sel_matrix(n_cmp, n_blk_pad)
    e = (jnp.arange(cb * SLC_BLOCK)[None, :] // SLC_BLOCK == jnp.arange(cb)[:, None]).astype(BF16)
    eg = _gate_expand()
    win_len = win.shape[1]
    per_b = lambda b: (b, 0, 0)
    fix2 = lambda b: (0, 0)
    fix3 = lambda b: (0, 0, 0)
    once = pl.Buffered(1)
    return pl.pallas_call(
        functools.partial(_dec_attn_kernel, past=past, cb=cb),
        grid=(bsz,),
        in_specs=[pl.BlockSpec((None, 1, D_ATT), per_b), pl.BlockSpec((None, 1, LANES), per_b),
                  pl.BlockSpec((None, 1, 6 * KV_WIDTH), per_b),
                  pl.BlockSpec((None, n_cmp, KV_WIDTH), per_b), pl.BlockSpec((None, n_cmp, KV_WIDTH), per_b),
                  pl.BlockSpec((None, past, KV_WIDTH), per_b, pipeline_mode=once),
                  pl.BlockSpec((None, past, KV_WIDTH), per_b, pipeline_mode=once),
                  pl.BlockSpec((None, win_len, 2 * KV_WIDTH), per_b),
                  pl.BlockSpec((n_cmp, n_blk_pad), fix2), pl.BlockSpec((cb, cb * SLC_BLOCK), fix2),
                  pl.BlockSpec((3, LANES, D_ATT), fix3)],
        out_specs=pl.BlockSpec((None, 1, D_ATT), per_b),
        out_shape=jax.ShapeDtypeStruct((bsz, 1, D_ATT), F32),
        compiler_params=_params(("parallel",)), name="nsa_decode",
    )(q, gates, nkv, kc, vc, ks, vs, win, msel, e, eg)


def _top16(s, payload=None):
    n = s.shape[0]
    rid = lax.broadcasted_iota(I32, s.shape, 0)
    vals, picks = [], []
    for _ in range(PEER_TOPK):
        m = jnp.max(s, axis=0, keepdims=True)
        idx = jnp.min(jnp.where(s == m, rid, n), axis=0, keepdims=True)
        hit = rid == idx
        vals.append(m)
        picks.append(idx if payload is None else jnp.sum(jnp.where(hit, payload, 0), axis=0, keepdims=True))
        s = jnp.where(hit, NEG_INF, s)
    return jnp.concatenate(vals, axis=0), jnp.concatenate(picks, axis=0)


def _peer_topk_kernel(h_ref, wq_ref, keys_ref, eidx_ref, gw_ref):
    q = _dot(h_ref[...].astype(BF16), wq_ref[...])
    half = PEER_DK // 2
    for h in range(PEER_HEADS):
        tops = []
        for i in range(2):
            col = (h * 2 + i) * half
            s_t = _dot_nt(keys_ref[h * 2 + i], q[:, col:col + half].astype(BF16))
            tops.append(_top16(s_t))
        (s1, i1), (s2, i2) = tops
        cand = jnp.concatenate([s1[a:a + 1] + s2 for a in range(PEER_TOPK)], axis=0)
        cidx = jnp.concatenate([i1[a:a + 1] * N_KEYS + i2 for a in range(PEER_TOPK)], axis=0)
        top_s, eidx = _top16(cand, cidx)
        e = jnp.exp(top_s - top_s[0:1])
        gw_ref[h * PEER_TOPK:(h + 1) * PEER_TOPK, :] = e / jnp.sum(e, axis=0, keepdims=True)
        eidx_ref[h * PEER_TOPK:(h + 1) * PEER_TOPK, :] = eidx


def _peer_topk(h, wq, keys, tm):
    n = h.shape[0]
    half = PEER_DK // 2
    return pl.pallas_call(
        _peer_topk_kernel, grid=(n // tm,),
        in_specs=[pl.BlockSpec((tm, D_MODEL), lambda i: (i, 0)),
                  pl.BlockSpec((D_MODEL, PEER_HEADS * PEER_DK), lambda i: (0, 0)),
                  pl.BlockSpec((PEER_HEADS * 2, N_KEYS, half), lambda i: (0, 0, 0))],
        out_specs=[pl.BlockSpec((PEER_ROWS, tm), lambda i: (0, i)),
                   pl.BlockSpec((PEER_ROWS, tm), lambda i: (0, i))],
        out_shape=[jax.ShapeDtypeStruct((PEER_ROWS, n), I32), jax.ShapeDtypeStruct((PEER_ROWS, n), F32)],
        compiler_params=_params(("parallel",)), name="peer_topk")(h, wq, keys)


def _peer_gather_kernel(idx_ref, gw_ref, x_ref, lng_ref, lnb_ref, u_hbm, v_hbm, o_ref,
                        idx_smem, ubuf, vbuf, ffn_ref, sem_idx, sem_u, sem_v):
    idx_copy = pltpu.make_async_copy(idx_ref, idx_smem, sem_idx)
    idx_copy.start()
    idx_copy.wait()

    def row_copies(t, slot, e):
        i = idx_smem[e, t]
        return (pltpu.make_async_copy(u_hbm.at[pl.ds(i, 1)], ubuf.at[slot, pl.ds(e, 1)], sem_u.at[slot]),
                pltpu.make_async_copy(v_hbm.at[pl.ds(i, 1)], vbuf.at[slot, pl.ds(e, 1)], sem_v.at[slot]))

    def issue(t, slot):
        for e in range(PEER_ROWS):
            cu, cv = row_copies(t, slot, e)
            cu.start()
            cv.start()

    def wait(slot):
        pltpu.make_async_copy(u_hbm.at[pl.ds(0, PEER_ROWS)], ubuf.at[slot], sem_u.at[slot]).wait()
        pltpu.make_async_copy(v_hbm.at[pl.ds(0, PEER_ROWS)], vbuf.at[slot], sem_v.at[slot]).wait()

    ahead = PEER_SLOTS - 1
    for t0 in range(ahead):
        issue(t0, t0)
    lane_t = lax.broadcasted_iota(I32, (PEER_ROWS, PEER_TB), 1)

    def token(t, carry):
        slot = t % PEER_SLOTS

        @pl.when(t + ahead < PEER_TB)
        def _():
            issue(t + ahead, (t + ahead) % PEER_SLOTS)

        wait(slot)
        x_t = x_ref[pl.ds(t, 1), :]
        hidden = jnp.sum(ubuf[slot] * x_t, axis=1, keepdims=True)
        g_col = jnp.sum(jnp.where(lane_t == t, gw_ref[...], 0.0), axis=1, keepdims=True)
        coef = g_col * jax.nn.gelu(hidden)
        ffn_ref[pl.ds(t, 1), :] = jnp.sum(coef * vbuf[slot], axis=0, keepdims=True)
        return carry

    lax.fori_loop(0, PEER_TB, token, 0)
    o_ref[...] = _layer_norm(DN_ALPHA * x_ref[...] + ffn_ref[...], lng_ref[...], lnb_ref[...])


def _peer_gather_ln(eidx_t, gw_t, x, u_tab, v_tab, g, b):
    n = x.shape[0]
    fix = lambda i: (0, 0)
    return pl.pallas_call(
        _peer_gather_kernel, grid=(n // PEER_TB,),
        in_specs=[pl.BlockSpec((PEER_ROWS, PEER_TB), lambda i: (0, i)),
                  pl.BlockSpec((PEER_ROWS, PEER_TB), lambda i: (0, i)),
                  pl.BlockSpec((PEER_TB, D_MODEL), lambda i: (i, 0)),
                  pl.BlockSpec((1, D_MODEL), fix), pl.BlockSpec((1, D_MODEL), fix),
                  pl.BlockSpec(memory_space=pl.ANY), pl.BlockSpec(memory_space=pl.ANY)],
        out_specs=pl.BlockSpec((PEER_TB, D_MODEL), lambda i: (i, 0)),
        out_shape=jax.ShapeDtypeStruct((n, D_MODEL), F32),
        scratch_shapes=[pltpu.SMEM((PEER_ROWS, PEER_TB), I32),
                        pltpu.VMEM((PEER_SLOTS, PEER_ROWS, D_MODEL), F32),
                        pltpu.VMEM((PEER_SLOTS, PEER_ROWS, D_MODEL), F32),
                        pltpu.VMEM((PEER_TB, D_MODEL), F32),
                        pltpu.SemaphoreType.DMA(()),
                        pltpu.SemaphoreType.DMA((PEER_SLOTS,)),
                        pltpu.SemaphoreType.DMA((PEER_SLOTS,))],
        compiler_params=_params(("arbitrary",)), name="peer_gather",
    )(eidx_t, gw_t, x, g, b, u_tab, v_tab)


def _rope_tables(pos):
    half = HEAD_DIM // 2
    inv = ROPE_THETA ** (-jnp.arange(half, dtype=F32) / half)
    ang = pos.astype(F32)[:, None] * inv
    cos, sin = jnp.cos(ang), jnp.sin(ang)
    return jnp.tile(cos, (1, 4)), jnp.tile(jnp.concatenate([-sin, sin], axis=1), (1, 2))


def _layer_weights(p):
    w_in = p['w_in']
    w_u = w_in[:, :D_SSM].astype(BF16)
    pad = PROJ_W - (w_in.shape[1] - D_SSM)
    w_rest = jnp.pad(w_in[:, D_SSM:], ((0, 0), (0, pad))).astype(BF16)
    cmp_k = _compress_weights(p['cmp_pe'][0], p['cmp_w1'][0], p['cmp_w2'][0])
    cmp_v = _compress_weights(p['cmp_pe'][1], p['cmp_w1'][1], p['cmp_w2'][1])
    return dict(
        w_u=w_u, w_rest=w_rest, w_glu=p['w_glu'].astype(BF16), d=p['d'].reshape(1, D_SSM),
        wo_ssm=p['w_out'][:D_SSM].astype(BF16), wo_att=p['w_out'][D_SSM:].astype(BF16),
        ln1_g=p['ln1_g'].reshape(1, D_MODEL), ln1_b=p['ln1_b'].reshape(1, D_MODEL),
        ln2_g=p['ln2_g'].reshape(1, D_MODEL), ln2_b=p['ln2_b'].reshape(1, D_MODEL),
        peer_wq=p['peer_wq'].astype(BF16),
        peer_keys=p['peer_keys'].reshape(PEER_HEADS * 2, N_KEYS, PEER_DK // 2).astype(BF16),
        peer_u=p['peer_u'], peer_v=p['peer_v'], cmp_k=cmp_k, cmp_v=cmp_v)


def _token_tail(x, y_ssm, y_att, w, tm):
    h = _outproj_ln(y_ssm, y_att, x, w['wo_ssm'], w['wo_att'], w['ln1_g'], w['ln1_b'], tm)
    eidx_t, gw_t = _peer_topk(h, w['peer_wq'], w['peer_keys'], min(tm, 256))
    return _peer_gather_ln(eidx_t, gw_t, h, w['peer_u'], w['peer_v'], w['ln2_g'], w['ln2_b'])


def _prompt_layer(x, bsz, seq, w, p):
    n = bsz * seq
    tm = min(512, seq)
    cos, sin = _rope_tables(jnp.arange(seq, dtype=I32))
    q, kv, gates = _project(x, w['w_rest'], cos, sin, tm)
    x_tb = x.reshape(bsz, seq, D_MODEL).transpose(1, 0, 2).reshape(n, D_MODEL)
    u_tb = _matmul(x_tb, w['w_u'], tm)
    wb, wc, ar, ai = _s5_weights(p['a_re'], p['a_im'], p['log_dt'], p['b_re'], p['b_im'], p['c_re'], p['c_im'], bsz)
    s0 = jnp.zeros((bsz, 2 * N_STATE), F32)
    gy_tb, s_fin = _s5_scan(u_tb, wb, wc, ar, ai, w['d'], s0, min(64, seq))
    y_ssm_tb = _glu(gy_tb, w['w_glu'], tm)
    y_ssm = y_ssm_tb.reshape(seq, bsz, D_SSM).transpose(1, 0, 2).reshape(n, D_SSM)
    n_cmp = seq // CMP_STRIDE
    xk = kv[:, 0:KV_WIDTH].reshape(bsz, n_cmp, CMP_STRIDE * KV_WIDTH)
    xv = kv[:, KV_WIDTH:2 * KV_WIDTH].reshape(bsz, n_cmp, CMP_STRIDE * KV_WIDTH)
    kc = _compress(xk, *w['cmp_k'])
    vc = _compress(xv, *w['cmp_v'])
    y_att = _attention(q, gates, kv, kc, vc, bsz, seq)
    y = _token_tail(x, y_ssm, y_att, w, tm)
    new_kv = kv[:, :4 * KV_WIDTH].reshape(bsz, seq, 4, N_KV_HEADS, HEAD_DIM)
    new_win = kv[:, 4 * KV_WIDTH:].reshape(bsz, seq, 2, N_KV_HEADS, HEAD_DIM)
    win = jnp.concatenate([jnp.zeros((bsz, WINDOW, 2, N_KV_HEADS, HEAD_DIM), F32), new_win], axis=1)[:, -WINDOW:]
    return y, new_kv, win, s_fin.reshape(bsz, 2, N_GROUPS, SSM_STATE)


def _sample_layer(x, bsz, past, cache, page_table, layer, win_past, ssm_state, w, p):
    rows = x.shape[0]
    cos, sin = _rope_tables(jnp.full((rows,), past, I32))
    q, kv, gates = _project(x, w['w_rest'], cos, sin, rows)
    u = _matmul(x, w['w_u'], rows)
    wb, wc, ar, ai = _s5_weights(p['a_re'], p['a_im'], p['log_dt'], p['b_re'], p['b_im'], p['c_re'], p['c_im'], bsz)
    gy, s_fin = _s5_scan(u[:bsz], wb, wc, ar, ai, w['d'], ssm_state.reshape(bsz, 2 * N_STATE), 1)
    y_ssm = jnp.pad(_glu(gy, w['w_glu'], bsz), ((0, rows - bsz), (0, 0)))
    xk, xv, ks, vs = _page_gather(cache, page_table, layer)
    kc = _compress(xk, *w['cmp_k'])
    vc = _compress(xv, *w['cmp_v'])
    win_len = win_past.shape[1]
    win_rows = win_past.reshape(bsz, win_len, 2 * KV_WIDTH)
    y_att = _decode_attention(q[:bsz].reshape(bsz, 1, D_ATT), gates[:bsz].reshape(bsz, 1, LANES),
                              kv[:bsz].reshape(bsz, 1, 6 * KV_WIDTH), kc, vc, ks, vs, win_rows, past)
    y_att = jnp.pad(y_att.reshape(bsz, D_ATT), ((0, rows - bsz), (0, 0)))
    y = _token_tail(x, y_ssm, y_att, w, rows)
    new_kv = kv[:bsz, :4 * KV_WIDTH].reshape(bsz, 1, 4, N_KV_HEADS, HEAD_DIM)
    new_win = kv[:bsz, 4 * KV_WIDTH:].reshape(bsz, 1, 2, N_KV_HEADS, HEAD_DIM)
    win = jnp.concatenate([win_past, new_win], axis=1)[:, -win_len:]
    return y, new_kv, win, s_fin.reshape(bsz, 2, N_GROUPS, SSM_STATE)


def kernel(x_prompt, x_sample, cache_kv, cache_win, state_ssm, page_table, w_in, ssm_a_re, ssm_a_im, ssm_log_dt, ssm_b_re, ssm_b_im, ssm_c_re, ssm_c_im, ssm_d, w_glu, cmp_pe, cmp_w1, cmp_w2, w_out, ln1_g, ln1_b, peer_wq, peer_keys, peer_u, peer_v, ln2_g, ln2_b):
    bsz, seq, _ = x_prompt.shape
    dec_bsz = x_sample.shape[0]
    depth = w_in.shape[0]
    past = page_table.shape[1] * PAGE_SIZE
    cache = cache_kv.reshape(depth, cache_kv.shape[1], 4 * PAGE_SIZE, KV_WIDTH)
    dec_rows = -(-dec_bsz // PEER_TB) * PEER_TB
    y_p = x_prompt.reshape(bsz * seq, D_MODEL)
    y_s = jnp.pad(x_sample.reshape(dec_bsz, D_MODEL), ((0, dec_rows - dec_bsz), (0, 0)))
    outs = [[] for _ in range(6)]
    for l in range(depth):
        p = {'w_in': w_in[l], 'a_re': ssm_a_re[l], 'a_im': ssm_a_im[l], 'log_dt': ssm_log_dt[l],
             'b_re': ssm_b_re[l], 'b_im': ssm_b_im[l], 'c_re': ssm_c_re[l], 'c_im': ssm_c_im[l],
             'd': ssm_d[l], 'w_glu': w_glu[l], 'cmp_pe': cmp_pe[l], 'cmp_w1': cmp_w1[l], 'cmp_w2': cmp_w2[l],
             'w_out': w_out[l], 'ln1_g': ln1_g[l], 'ln1_b': ln1_b[l], 'peer_wq': peer_wq[l],
             'peer_keys': peer_keys[l], 'peer_u': peer_u[l], 'peer_v': peer_v[l],
             'ln2_g': ln2_g[l], 'ln2_b': ln2_b[l]}
        w = _layer_weights(p)
        y_p, kvn, winn, sn = _prompt_layer(y_p, bsz, seq, w, p)
        outs[0].append(kvn)
        outs[1].append(winn)
        outs[2].append(sn)
        y_s, kvn, winn, sn = _sample_layer(y_s, dec_bsz, past, cache, page_table, l, cache_win[l], state_ssm[l], w, p)
        outs[3].append(kvn)
        outs[4].append(winn)
        outs[5].append(sn)
    return (y_p.reshape(bsz, seq, D_MODEL), y_s[:dec_bsz].reshape(dec_bsz, 1, D_MODEL),
            jnp.stack(outs[0]), jnp.stack(outs[1]), jnp.stack(outs[2]),
            jnp.stack(outs[3]), jnp.stack(outs[4]), jnp.stack(outs[5]))
```

```python
import functools
import math

import jax
import jax.numpy as jnp
from jax import lax
from jax.experimental import pallas as pl
from jax.experimental.pallas import tpu as pltpu

F32 = jnp.float32
BF16 = jnp.bfloat16
I32 = jnp.int32

D_MODEL = 1024
D_SSM = 512
D_ATT = 512
SSM_CH = 16
N_GROUPS = 32
SSM_STATE = 64
N_STATE = N_GROUPS * SSM_STATE
HEAD_DIM = 64
N_HEADS = 8
N_KV_HEADS = 2
GQA = 4
KV_WIDTH = 128
CMP_STRIDE = 16
CMP_LEN = 32
SLC_BLOCK = 64
N_SLC = 16
WINDOW = 512
Q_BLOCK = 64
ROPE_THETA = 10000.0
PEER_HEADS = 8
PEER_DK = 256
N_KEYS = 128
PEER_TOPK = 16
PAGE_SIZE = 128
DN_ALPHA = 4 ** 0.25
LN_EPS = 1e-5
ATT_SCALE = HEAD_DIM ** -0.5
TINY = float(jnp.finfo(jnp.float32).tiny)
NEG_INF = float("-inf")
POS_INF = float("inf")

LANES = 128
PROJ_W = 512 + 6 * KV_WIDTH + LANES
VMEM_LIMIT = 56 * 1024 * 1024
PEER_TB = 128
PEER_SLOTS = 4
PEER_ROWS = PEER_HEADS * PEER_TOPK
PEER_FEAT_TILES = D_MODEL // LANES
PAGES_PER_STEP = 4


def _params(sem, **kw):
    return pltpu.CompilerParams(dimension_semantics=sem, vmem_limit_bytes=VMEM_LIMIT, **kw)


def _dot(a, b):
    return jnp.dot(a, b, preferred_element_type=F32)


def _dot_nt(a, b):
    return lax.dot_general(a, b, (((1,), (1,)), ((), ())), preferred_element_type=F32)


def _dot_exact(a, b):
    return lax.dot_general(a, b, (((1,), (0,)), ((), ())), precision=lax.Precision.HIGHEST,
                           preferred_element_type=F32)


def _layer_norm(x, g, b):
    mu = jnp.mean(x, axis=-1, keepdims=True)
    xc = x - mu
    var = jnp.mean(xc * xc, axis=-1, keepdims=True)
    return xc * lax.rsqrt(var + LN_EPS) * g + b


def _masked_softmax(s, valid):
    s = jnp.where(valid, s, NEG_INF)
    m = jnp.max(s, axis=-1, keepdims=True)
    m = jnp.where(m > NEG_INF, m, 0.0)
    e = jnp.where(valid, jnp.exp(s - m), 0.0)
    den = jnp.maximum(jnp.sum(e, axis=-1, keepdims=True), TINY)
    return e / den


def _mm_kernel(x_ref, w_ref, o_ref):
    o_ref[...] = _dot(x_ref[...].astype(BF16), w_ref[...])


def _matmul(x, w, tm):
    n, k = x.shape
    m = w.shape[1]
    return pl.pallas_call(
        _mm_kernel, grid=(n // tm,),
        in_specs=[pl.BlockSpec((tm, k), lambda i: (i, 0)), pl.BlockSpec((k, m), lambda i: (0, 0))],
        out_specs=pl.BlockSpec((tm, m), lambda i: (i, 0)),
        out_shape=jax.ShapeDtypeStruct((n, m), F32),
        compiler_params=_params(("parallel",)), name="mm")(x, w)


def _proj_kernel(x_ref, w_ref, cos_ref, sin_ref, q_ref, kv_ref, g_ref):
    acc = _dot(x_ref[...].astype(BF16), w_ref[...])
    cos = cos_ref[...]
    sin = sin_ref[...]
    lane = lax.broadcasted_iota(I32, cos.shape, 1)
    first_half = (lane % HEAD_DIM) < (HEAD_DIM // 2)

    def rope(v):
        rot = jnp.where(first_half, pltpu.roll(v, 96, 1), pltpu.roll(v, 32, 1))
        return v * cos + rot * sin

    for j in range(4):
        q_ref[:, j * LANES:(j + 1) * LANES] = rope(acc[:, j * LANES:(j + 1) * LANES])
    for j in range(6):
        blk = acc[:, 512 + j * LANES:512 + (j + 1) * LANES]
        kv_ref[:, j * LANES:(j + 1) * LANES] = rope(blk) if j % 2 == 0 else blk
    g_ref[...] = jax.nn.sigmoid(acc[:, 512 + 6 * LANES:])


def _project(x, w, cos, sin, tm):
    n = x.shape[0]
    tab_blocks = cos.shape[0] // tm
    return pl.pallas_call(
        _proj_kernel, grid=(n // tm,),
        in_specs=[pl.BlockSpec((tm, D_MODEL), lambda i: (i, 0)),
                  pl.BlockSpec((D_MODEL, PROJ_W), lambda i: (0, 0)),
                  pl.BlockSpec((tm, LANES), lambda i: (i % tab_blocks, 0)),
                  pl.BlockSpec((tm, LANES), lambda i: (i % tab_blocks, 0))],
        out_specs=[pl.BlockSpec((tm, 512), lambda i: (i, 0)),
                   pl.BlockSpec((tm, 6 * KV_WIDTH), lambda i: (i, 0)),
                   pl.BlockSpec((tm, LANES), lambda i: (i, 0))],
        out_shape=[jax.ShapeDtypeStruct((n, 512), F32),
                   jax.ShapeDtypeStruct((n, 6 * KV_WIDTH), F32),
                   jax.ShapeDtypeStruct((n, LANES), F32)],
        compiler_params=_params(("parallel",)), name="proj")(x, w, cos, sin)


def _glu_kernel(a_ref, w_ref, o_ref):
    gl = _dot(a_ref[...].astype(BF16), w_ref[...])
    o_ref[...] = gl[:, :D_SSM] * jax.nn.sigmoid(gl[:, D_SSM:])


def _glu(a, w, tm):
    n = a.shape[0]
    return pl.pallas_call(
        _glu_kernel, grid=(n // tm,),
        in_specs=[pl.BlockSpec((tm, D_SSM), lambda i: (i, 0)),
                  pl.BlockSpec((D_SSM, 2 * D_SSM), lambda i: (0, 0))],
        out_specs=pl.BlockSpec((tm, D_SSM), lambda i: (i, 0)),
        out_shape=jax.ShapeDtypeStruct((n, D_SSM), F32),
        compiler_params=_params(("parallel",)), name="glu")(a, w)


def _outproj_kernel(ys_ref, ya_ref, x_ref, w1_ref, w2_ref, g_ref, b_ref, o_ref):
    mix = _dot(ys_ref[...].astype(BF16), w1_ref[...]) + _dot(ya_ref[...].astype(BF16), w2_ref[...])
    o_ref[...] = _layer_norm(DN_ALPHA * x_ref[...] + mix, g_ref[...], b_ref[...])


def _outproj_ln(ys, ya, x, w1, w2, g, b, tm):
    n = x.shape[0]
    row = lambda i: (i, 0)
    fix = lambda i: (0, 0)
    return pl.pallas_call(
        _outproj_kernel, grid=(n // tm,),
        in_specs=[pl.BlockSpec((tm, D_SSM), row), pl.BlockSpec((tm, D_ATT), row),
                  pl.BlockSpec((tm, D_MODEL), row),
                  pl.BlockSpec((D_SSM, D_MODEL), fix), pl.BlockSpec((D_ATT, D_MODEL), fix),
                  pl.BlockSpec((1, D_MODEL), fix), pl.BlockSpec((1, D_MODEL), fix)],
        out_specs=pl.BlockSpec((tm, D_MODEL), row),
        out_shape=jax.ShapeDtypeStruct((n, D_MODEL), F32),
        compiler_params=_params(("parallel",)), name="outproj_ln")(ys, ya, x, w1, w2, g, b)


def _s5_kernel(u_ref, wb_ref, wc_ref, ar_ref, ai_ref, d_ref, s0_ref, gy_ref, sf_ref, bu_ref, st_ref,
               *, n_steps, batch):
    @pl.when(pl.program_id(0) == 0)
    def _():
        st_ref[...] = s0_ref[...]

    u = u_ref[...]
    bu_ref[...] = _dot(u.astype(BF16), wb_ref[...])
    n_chunk = 4
    cw = N_STATE // n_chunk

    def step(t, carry):
        r = pl.multiple_of(t * batch, batch)
        new = []
        for c in range(n_chunk):
            sr, si = carry[c], carry[n_chunk + c]
            ar = ar_ref[:, c * cw:(c + 1) * cw]
            ai = ai_ref[:, c * cw:(c + 1) * cw]
            nr = ar * sr - ai * si + bu_ref[pl.ds(r, batch), c * cw:(c + 1) * cw]
            ni = ar * si + ai * sr + bu_ref[pl.ds(r, batch), N_STATE + c * cw:N_STATE + (c + 1) * cw]
            bu_ref[pl.ds(r, batch), c * cw:(c + 1) * cw] = nr
            bu_ref[pl.ds(r, batch), N_STATE + c * cw:N_STATE + (c + 1) * cw] = ni
            new.append((nr, ni))
        return tuple(p[0] for p in new) + tuple(p[1] for p in new)

    init = tuple(st_ref[:, c * cw:(c + 1) * cw] for c in range(n_chunk)) + \
        tuple(st_ref[:, N_STATE + c * cw:N_STATE + (c + 1) * cw] for c in range(n_chunk))
    fin = lax.fori_loop(0, n_steps, step, init)
    for c in range(n_chunk):
        st_ref[:, c * cw:(c + 1) * cw] = fin[c]
        st_ref[:, N_STATE + c * cw:N_STATE + (c + 1) * cw] = fin[n_chunk + c]
    y = _dot(bu_ref[...].astype(BF16), wc_ref[...]) + d_ref[...] * u
    gy_ref[...] = jax.nn.gelu(y)
    sf_ref[...] = st_ref[...]


def _s5_scan(u_tb, wb, wc, ar, ai, d, s0, n_steps):
    batch = s0.shape[0]
    n = u_tb.shape[0]
    rows = n_steps * batch
    fix = lambda i: (0, 0)
    return pl.pallas_call(
        functools.partial(_s5_kernel, n_steps=n_steps, batch=batch),
        grid=(n // rows,),
        in_specs=[pl.BlockSpec((rows, D_SSM), lambda i: (i, 0)),
                  pl.BlockSpec((D_SSM, 2 * N_STATE), fix), pl.BlockSpec((2 * N_STATE, D_SSM), fix),
                  pl.BlockSpec((batch, N_STATE), fix), pl.BlockSpec((batch, N_STATE), fix),
                  pl.BlockSpec((1, D_SSM), fix), pl.BlockSpec((batch, 2 * N_STATE), fix)],
        out_specs=[pl.BlockSpec((rows, D_SSM), lambda i: (i, 0)),
                   pl.BlockSpec((batch, 2 * N_STATE), fix)],
        out_shape=[jax.ShapeDtypeStruct((n, D_SSM), F32),
                   jax.ShapeDtypeStruct((batch, 2 * N_STATE), F32)],
        scratch_shapes=[pltpu.VMEM((rows, 2 * N_STATE), F32), pltpu.VMEM((batch, 2 * N_STATE), F32)],
        compiler_params=_params(("arbitrary",)), name="s5_scan")(u_tb, wb, wc, ar, ai, d, s0)


def _s5_weights(a_re, a_im, log_dt, b_re, b_im, c_re, c_im, batch):
    dt = jnp.exp(log_dt)[:, None]
    mag = jnp.exp(dt * a_re)
    abar_re, abar_im = mag * jnp.cos(dt * a_im), mag * jnp.sin(dt * a_im)
    den = a_re * a_re + a_im * a_im
    f_re = ((abar_re - 1.0) * a_re + abar_im * a_im) / den
    f_im = (abar_im * a_re - (abar_re - 1.0) * a_im) / den
    bb_re = f_re[..., None] * b_re - f_im[..., None] * b_im
    bb_im = f_re[..., None] * b_im + f_im[..., None] * b_re
    eye = jnp.eye(N_GROUPS, dtype=F32)
    wb_re = jnp.einsum('gpc,gh->gchp', bb_re, eye).reshape(D_SSM, N_STATE)
    wb_im = jnp.einsum('gpc,gh->gchp', bb_im, eye).reshape(D_SSM, N_STATE)
    wb = jnp.concatenate([wb_re, wb_im], axis=1).astype(BF16)
    wc_re = jnp.einsum('gcp,gh->gphc', c_re, eye).reshape(N_STATE, D_SSM)
    wc_im = jnp.einsum('gcp,gh->gphc', c_im, eye).reshape(N_STATE, D_SSM)
    wc = jnp.concatenate([wc_re, -wc_im], axis=0).astype(BF16)
    ar = jnp.broadcast_to(abar_re.reshape(1, N_STATE), (batch, N_STATE))
    ai = jnp.broadcast_to(abar_im.reshape(1, N_STATE), (batch, N_STATE))
    return wb, wc, ar, ai


def _compress_kernel(x_ref, pe_ref, w1a_ref, w1b_ref, w2_ref, o_ref):
    x = x_ref[...]
    n = x.shape[0]
    first = _dot((x + pe_ref[0:1, :]).astype(BF16), w1a_ref[...])
    second = _dot((x + pe_ref[1:2, :]).astype(BF16), w1b_ref[...])
    nxt = pltpu.roll(second, n - 1, 0)
    o_ref[...] = _dot(jax.nn.gelu(first + nxt).astype(BF16), w2_ref[...])


def _compress(x, pe, w1a, w1b, w2):
    bsz, n, width = x.shape
    fix = lambda b: (0, 0)
    return pl.pallas_call(
        _compress_kernel, grid=(bsz,),
        in_specs=[pl.BlockSpec((None, n, width), lambda b: (b, 0, 0)),
                  pl.BlockSpec((2, width), fix), pl.BlockSpec((width, KV_WIDTH), fix),
                  pl.BlockSpec((width, KV_WIDTH), fix), pl.BlockSpec((KV_WIDTH, KV_WIDTH), fix)],
        out_specs=pl.BlockSpec((None, n, KV_WIDTH), lambda b: (b, 0, 0)),
        out_shape=jax.ShapeDtypeStruct((bsz, n, KV_WIDTH), F32),
        compiler_params=_params(("parallel",)), name="compress")(x, pe, w1a, w1b, w2)


def _compress_weights(pe, w1, w2):
    eye = jnp.eye(N_KV_HEADS, dtype=F32)
    w1 = w1.reshape(2, CMP_STRIDE, HEAD_DIM, HEAD_DIM)
    pe = pe.reshape(2, CMP_STRIDE, HEAD_DIM)
    w1a = jnp.einsum('jde,hk->jhdke', w1[0], eye).reshape(CMP_STRIDE * KV_WIDTH, KV_WIDTH).astype(BF16)
    w1b = jnp.einsum('jde,hk->jhdke', w1[1], eye).reshape(CMP_STRIDE * KV_WIDTH, KV_WIDTH).astype(BF16)
    w2d = jnp.einsum('de,hk->hdke', w2, eye).reshape(KV_WIDTH, KV_WIDTH).astype(BF16)
    pe2 = jnp.broadcast_to(pe[:, :, None, :], (2, CMP_STRIDE, N_KV_HEADS, HEAD_DIM)).reshape(2, CMP_STRIDE * KV_WIDTH)
    return pe2, w1a, w1b, w2d


def _sel_matrix(n_cmp_rows, n_blk_cols):
    n = jnp.arange(n_cmp_rows)[:, None]
    j = jnp.arange(n_blk_cols)[None, :]
    per = SLC_BLOCK // CMP_STRIDE
    m = (n // per == j).astype(F32) + ((n + 1) // per == j).astype(F32)
    return jnp.where(n < n_cmp_rows - 1, m, 0.0)


def _block_ranks(score, n_blk):
    blk = lax.broadcasted_iota(I32, score.shape, 1)
    rank = jnp.zeros(score.shape, I32)
    for i in range(n_blk):
        col = score[:, i:i + 1]
        before = (col > score) | ((col == score) & (blk > i))
        rank = rank + before.astype(I32)
    return rank


def _attn_kernel(q_ref, g_ref, kc_ref, vc_ref, ks_ref, vs_ref, kw_ref, vw_ref, msel_ref, e3_ref, eg_ref,
                 o_ref, *, seq, ck, wk):
    qb = pl.program_id(1)
    q0 = qb * Q_BLOCK
    n_cmp = seq // CMP_STRIDE
    n_blk = seq // SLC_BLOCK
    n_sel = min(N_SLC, n_blk)
    rows = GQA * Q_BLOCK
    pos = q0 + lax.broadcasted_iota(I32, (Q_BLOCK, 1), 0)
    pos4 = jnp.concatenate([pos] * GQA, axis=0)
    lane_r = lax.broadcasted_iota(I32, (rows, LANES), 1)
    kc = kc_ref[...].astype(BF16)
    vc = vc_ref[...].astype(BF16)
    n_chunks = (q0 + Q_BLOCK + ck - 1) // ck
    kstart = pl.multiple_of(jnp.maximum(q0 + Q_BLOCK - wk, 0), Q_BLOCK)
    kw = kw_ref[pl.ds(kstart, wk), :].astype(BF16)
    vw = vw_ref[pl.ds(kstart, wk), :].astype(BF16)
    heads_c, heads_s, heads_w = [], [], []
    for hkv in range(N_KV_HEADS):
        parts = []
        for g in range(GQA):
            hd = hkv * GQA + g
            slab = q_ref[:, (hd // 2) * LANES:(hd // 2 + 1) * LANES]
            if hd % 2 != hkv:
                slab = pltpu.roll(slab, HEAD_DIM, 1)
            parts.append(slab)
        qh = jnp.concatenate(parts, axis=0)
        qh = jnp.where(lane_r // HEAD_DIM == hkv, qh * ATT_SCALE, 0.0).astype(BF16)

        s_c = _dot_nt(qh, kc)
        n_id = lax.broadcasted_iota(I32, s_c.shape, 1)
        p_c = _masked_softmax(s_c, n_id * CMP_STRIDE + (CMP_LEN - 1) <= pos4)
        o_c = _dot(p_c.astype(BF16), vc)
        p_grp = p_c[0:Q_BLOCK]
        for g in range(1, GQA):
            p_grp = p_grp + p_c[g * Q_BLOCK:(g + 1) * Q_BLOCK]
        p_slc = _dot_exact(p_grp, msel_ref[...])

        blk = lax.broadcasted_iota(I32, p_slc.shape, 1)
        cur = pos // SLC_BLOCK
        forced = (blk == 0) | (blk == cur) | (blk == cur - 1)
        future = blk * SLC_BLOCK > pos
        score = jnp.where(future, NEG_INF, jnp.where(forced, POS_INF, p_slc))
        sel = (_block_ranks(score, n_blk) < n_sel).astype(BF16)

        def chunk(c, carry):
            m, l, acc = carry
            base = pl.multiple_of(c * ck, ck)
            k = ks_ref[pl.ds(base, ck), :].astype(BF16)
            v = vs_ref[pl.ds(base, ck), :].astype(BF16)
            s = _dot_nt(qh, k)
            hit = _dot(sel, e3_ref[c])
            kpos = base + lax.broadcasted_iota(I32, hit.shape, 1)
            ok = (hit > 0.5) & (kpos <= pos)
            ok4 = jnp.concatenate([ok] * GQA, axis=0)
            s = jnp.where(ok4, s, NEG_INF)
            m_new = jnp.maximum(m, jnp.max(s, axis=-1, keepdims=True))
            m_use = jnp.where(m_new > NEG_INF, m_new, 0.0)
            alpha = jnp.exp(m - m_use)
            e = jnp.where(ok4, jnp.exp(s - m_use), 0.0)
            l = alpha * l + jnp.sum(e, axis=-1, keepdims=True)
            acc = alpha * acc + _dot(e.astype(BF16), v)
            return m_new, l, acc

        init = (jnp.full((rows, 1), NEG_INF, F32), jnp.zeros((rows, 1), F32), jnp.zeros((rows, LANES), F32))
        _, l_s, acc_s = lax.fori_loop(0, n_chunks, chunk, init)
        o_s = acc_s / jnp.maximum(l_s, TINY)

        s_w = _dot_nt(qh, kw)
        dpos = pos4 - (kstart + lax.broadcasted_iota(I32, s_w.shape, 1))
        p_w = _masked_softmax(s_w, (dpos >= 0) & (dpos < WINDOW))
        o_w = _dot(p_w.astype(BF16), vw)

        for g in range(GQA):
            sl = (slice(g * Q_BLOCK, (g + 1) * Q_BLOCK), slice(hkv * HEAD_DIM, (hkv + 1) * HEAD_DIM))
            heads_c.append(o_c[sl])
            heads_s.append(o_s[sl])
            heads_w.append(o_w[sl])

    gates = g_ref[...]
    out = jnp.zeros((Q_BLOCK, D_ATT), F32)
    for t, heads in enumerate((heads_c, heads_s, heads_w)):
        out = out + _dot_exact(gates, eg_ref[t]) * jnp.concatenate(heads, axis=1)
    o_ref[...] = out


def _gate_expand():
    c = jnp.arange(LANES)[None, :, None]
    t = jnp.arange(3)[:, None, None]
    h = (jnp.arange(D_ATT) // HEAD_DIM)[None, None, :]
    return (c == t * N_HEADS + h).astype(F32)


def _attention(q, gates, kv, kc, vc, bsz, seq):
    ck = min(512, seq)
    wk = min(WINDOW + Q_BLOCK, seq)
    n_qb = seq // Q_BLOCK
    n_cmp = seq // CMP_STRIDE
    n_blk = seq // SLC_BLOCK
    msel = _sel_matrix(n_cmp, n_blk)
    key_blk = (jnp.arange(seq) // SLC_BLOCK).reshape(seq // ck, 1, ck)
    e3 = (key_blk == jnp.arange(n_blk)[None, :, None]).astype(BF16)
    eg = _gate_expand()
    fix2 = lambda b, i: (0, 0)
    fix3 = lambda b, i: (0, 0, 0)
    kv_spec = lambda col: pl.BlockSpec((seq, KV_WIDTH), lambda b, i: (b, col))
    return pl.pallas_call(
        functools.partial(_attn_kernel, seq=seq, ck=ck, wk=wk),
        grid=(bsz, n_qb),
        in_specs=[pl.BlockSpec((Q_BLOCK, D_ATT), lambda b, i: (b * n_qb + i, 0)),
                  pl.BlockSpec((Q_BLOCK, LANES), lambda b, i: (b * n_qb + i, 0)),
                  pl.BlockSpec((None, n_cmp, KV_WIDTH), lambda b, i: (b, 0, 0)),
                  pl.BlockSpec((None, n_cmp, KV_WIDTH), lambda b, i: (b, 0, 0)),
                  kv_spec(2), kv_spec(3), kv_spec(4), kv_spec(5),
                  pl.BlockSpec((n_cmp, n_blk), fix2),
                  pl.BlockSpec((seq // ck, n_blk, ck), fix3),
                  pl.BlockSpec((3, LANES, D_ATT), fix3)],
        out_specs=pl.BlockSpec((Q_BLOCK, D_ATT), lambda b, i: (b * n_qb + i, 0)),
        out_shape=jax.ShapeDtypeStruct((bsz * seq, D_ATT), F32),
        compiler_params=_params(("parallel", "arbitrary")), name="nsa_prompt",
    )(q, gates, kc, vc, kv, kv, kv, kv, msel, e3, eg)


def _page_gather_kernel(pt_ref, *refs):
    del pt_ref
    pages = refs[:PAGES_PER_STEP]
    xk_ref, xv_ref, ks_ref, vs_ref = refs[PAGES_PER_STEP:]
    per_page = PAGE_SIZE // CMP_STRIDE
    for i, page_ref in enumerate(pages):
        rows = slice(i * per_page, (i + 1) * per_page)
        for j in range(CMP_STRIDE):
            cols = slice(j * KV_WIDTH, (j + 1) * KV_WIDTH)
            xk_ref[rows, cols] = page_ref[pl.ds(4 * j, per_page, stride=4 * CMP_STRIDE), :]
            xv_ref[rows, cols] = page_ref[pl.ds(4 * j + 1, per_page, stride=4 * CMP_STRIDE), :]
        ks_ref[i * PAGE_SIZE:(i + 1) * PAGE_SIZE, :] = page_ref[pl.ds(2, PAGE_SIZE, stride=4), :]
        vs_ref[i * PAGE_SIZE:(i + 1) * PAGE_SIZE, :] = page_ref[pl.ds(3, PAGE_SIZE, stride=4), :]


def _page_gather(cache, page_table, layer):
    bsz, n_pages = page_table.shape
    past = n_pages * PAGE_SIZE
    per_step = PAGES_PER_STEP * (PAGE_SIZE // CMP_STRIDE)
    width = CMP_STRIDE * KV_WIDTH

    def page_spec(i):
        return pl.BlockSpec((None, None, 4 * PAGE_SIZE, KV_WIDTH),
                            lambda b, p, pt: (layer, pt[b, p * PAGES_PER_STEP + i], 0, 0))

    grid_spec = pltpu.PrefetchScalarGridSpec(
        num_scalar_prefetch=1, grid=(bsz, n_pages // PAGES_PER_STEP),
        in_specs=[page_spec(i) for i in range(PAGES_PER_STEP)],
        out_specs=[pl.BlockSpec((None, per_step, width), lambda b, p, pt: (b, p, 0)),
                   pl.BlockSpec((None, per_step, width), lambda b, p, pt: (b, p, 0)),
                   pl.BlockSpec((None, PAGES_PER_STEP * PAGE_SIZE, KV_WIDTH), lambda b, p, pt: (b, p, 0)),
                   pl.BlockSpec((None, PAGES_PER_STEP * PAGE_SIZE, KV_WIDTH), lambda b, p, pt: (b, p, 0))])
    return pl.pallas_call(
        _page_gather_kernel, grid_spec=grid_spec,
        out_shape=[jax.ShapeDtypeStruct((bsz, past // CMP_STRIDE, width), F32),
                   jax.ShapeDtypeStruct((bsz, past // CMP_STRIDE, width), F32),
                   jax.ShapeDtypeStruct((bsz, past, KV_WIDTH), F32),
                   jax.ShapeDtypeStruct((bsz, past, KV_WIDTH), F32)],
        compiler_params=_params(("parallel", "arbitrary")), name="page_gather",
    )(page_table, *([cache] * PAGES_PER_STEP))


def _dec_attn_kernel(q_ref, g_ref, nkv_ref, kc_ref, vc_ref, ks_ref, vs_ref, win_ref, msel_ref, e_ref, eg_ref,
                     o_ref, *, past, cb):
    pos = past
    n_blk = past // SLC_BLOCK + 1
    n_sel = min(N_SLC, n_blk)
    n_chunks = (past // SLC_BLOCK) // cb
    ckeys = cb * SLC_BLOCK
    win_len = win_ref.shape[0]
    row = lax.broadcasted_iota(I32, (N_HEADS, LANES), 0)
    lane = lax.broadcasted_iota(I32, (N_HEADS, LANES), 1)

    qrow = jnp.broadcast_to(q_ref[...], (N_HEADS, D_ATT))
    x = jnp.zeros((N_HEADS, LANES), F32)
    for j in range(4):
        x = jnp.where(row // 2 == j, qrow[:, j * LANES:(j + 1) * LANES], x)
    x = jnp.where((row % 2) != (row // GQA), pltpu.roll(x, HEAD_DIM, 1), x)
    q8 = jnp.where(lane // HEAD_DIM == row // GQA, x * ATT_SCALE, 0.0).astype(BF16)
    nkv = nkv_ref[...]

    def new_rows(col):
        return jnp.broadcast_to(nkv[:, col * KV_WIDTH:(col + 1) * KV_WIDTH], (N_HEADS, KV_WIDTH)).astype(BF16)

    s_c = _dot_nt(q8, kc_ref[...].astype(BF16))
    n_id = lax.broadcasted_iota(I32, s_c.shape, 1)
    p_c = _masked_softmax(s_c, n_id * CMP_STRIDE + (CMP_LEN - 1) <= pos)
    o_c = _dot(p_c.astype(BF16), vc_ref[...].astype(BF16))
    grp0 = jnp.sum(p_c[0:GQA], axis=0, keepdims=True)
    grp1 = jnp.sum(p_c[GQA:2 * GQA], axis=0, keepdims=True)
    p_grp = jnp.where(lax.broadcasted_iota(I32, p_c.shape, 0) < GQA, grp0, grp1)
    p_slc = _dot_exact(p_grp, msel_ref[...])
    blk = lax.broadcasted_iota(I32, p_slc.shape, 1)
    cur = pos // SLC_BLOCK
    forced = (blk == 0) | (blk == cur) | (blk == cur - 1)
    future = (blk * SLC_BLOCK > pos) | (blk >= n_blk)
    score = jnp.where(future, NEG_INF, jnp.where(forced, POS_INF, p_slc))
    sel = (_block_ranks(score, n_blk) < n_sel).astype(F32)

    k_new = new_rows(2)
    v_new = new_rows(3)
    ok_new = sel[:, cur:cur + 1] > 0.5
    s_new = jnp.where(ok_new, _dot_nt(q8, k_new)[:, 0:1], NEG_INF)
    scores, oks = [], []
    m = s_new
    for c in range(n_chunks):
        s = _dot_nt(q8, ks_ref[c * ckeys:(c + 1) * ckeys, :].astype(BF16))
        ok = _dot(sel[:, c * cb:(c + 1) * cb].astype(BF16), e_ref[...]) > 0.5
        s = jnp.where(ok, s, NEG_INF)
        m = jnp.maximum(m, jnp.max(s, axis=-1, keepdims=True))
        scores.append(s)
        oks.append(ok)
    m = jnp.where(m > NEG_INF, m, 0.0)
    e_new = jnp.where(ok_new, jnp.exp(s_new - m), 0.0)
    den = e_new
    acc = e_new.astype(BF16).astype(F32) * v_new.astype(F32)
    for c in range(n_chunks):
        e = jnp.where(oks[c], jnp.exp(scores[c] - m), 0.0)
        den = den + jnp.sum(e, axis=-1, keepdims=True)
        acc = acc + _dot(e.astype(BF16), vs_ref[c * ckeys:(c + 1) * ckeys, :].astype(BF16))
    o_s = acc / jnp.maximum(den, TINY)

    kw = win_ref[:, 0:KV_WIDTH].astype(BF16)
    vw = win_ref[:, KV_WIDTH:2 * KV_WIDTH].astype(BF16)
    s_w = _dot_nt(q8, kw)
    dist = win_len - lax.broadcasted_iota(I32, s_w.shape, 1)
    ok_w = (dist < WINDOW) & (pos - dist >= 0)
    s_w = jnp.where(ok_w, s_w, NEG_INF)
    s_wn = _dot_nt(q8, new_rows(4))[:, 0:1]
    m_w = jnp.maximum(jnp.max(s_w, axis=-1, keepdims=True), s_wn)
    e_w = jnp.where(ok_w, jnp.exp(s_w - m_w), 0.0)
    e_wn = jnp.exp(s_wn - m_w)
    den_w = jnp.sum(e_w, axis=-1, keepdims=True) + e_wn
    o_w = (_dot(e_w.astype(BF16), vw) + e_wn.astype(BF16).astype(F32) * new_rows(5).astype(F32)) / den_w

    gates = g_ref[...]
    out = jnp.zeros((1, D_ATT), F32)
    for t, o8 in enumerate((o_c, o_s, o_w)):
        pieces = [o8[hd:hd + 1, (hd // GQA) * HEAD_DIM:(hd // GQA + 1) * HEAD_DIM] for hd in range(N_HEADS)]
        g8 = jnp.broadcast_to(gates, (N_HEADS, LANES))
        out = out + _dot_exact(g8, eg_ref[t])[0:1] * jnp.concatenate(pieces, axis=1)
    o_ref[...] = out


def _decode_attention(q, gates, nkv, kc, vc, ks, vs, win, past):
    bsz = q.shape[0]
    n_cmp = kc.shape[1]
    n_past_blk = past // SLC_BLOCK
    cb = min(64, n_past_blk)
    n_blk_pad = -(-(n_past_blk + 1) // LANES) * LANES
    msel = _sel_matrix(n_cmp, n_blk_pad)
    e = (jnp.arange(cb * SLC_BLOCK)[None, :] // SLC_BLOCK == jnp.arange(cb)[:, None]).astype(BF16)
    eg = _gate_expand()
    win_len = win.shape[1]
    per_b = lambda b: (b, 0, 0)
    fix2 = lambda b: (0, 0)
    fix3 = lambda b: (0, 0, 0)
    once = pl.Buffered(1)
    return pl.pallas_call(
        functools.partial(_dec_attn_kernel, past=past, cb=cb),
        grid=(bsz,),
        in_specs=[pl.BlockSpec((None, 1, D_ATT), per_b), pl.BlockSpec((None, 1, LANES), per_b),
                  pl.BlockSpec((None, 1, 6 * KV_WIDTH), per_b),
                  pl.BlockSpec((None, n_cmp, KV_WIDTH), per_b), pl.BlockSpec((None, n_cmp, KV_WIDTH), per_b),
                  pl.BlockSpec((None, past, KV_WIDTH), per_b, pipeline_mode=once),
                  pl.BlockSpec((None, past, KV_WIDTH), per_b, pipeline_mode=once),
                  pl.BlockSpec((None, win_len, 2 * KV_WIDTH), per_b),
                  pl.BlockSpec((n_cmp, n_blk_pad), fix2), pl.BlockSpec((cb, cb * SLC_BLOCK), fix2),
                  pl.BlockSpec((3, LANES, D_ATT), fix3)],
        out_specs=pl.BlockSpec((None, 1, D_ATT), per_b),
        out_shape=jax.ShapeDtypeStruct((bsz, 1, D_ATT), F32),
        compiler_params=_params(("parallel",)), name="nsa_decode",
    )(q, gates, nkv, kc, vc, ks, vs, win, msel, e, eg)


def _top16(s, payload=None):
    n = s.shape[0]
    rid = lax.broadcasted_iota(I32, s.shape, 0)
    vals, picks = [], []
    for _ in range(PEER_TOPK):
        m = jnp.max(s, axis=0, keepdims=True)
        idx = jnp.min(jnp.where(s == m, rid, n), axis=0, keepdims=True)
        hit = rid == idx
        vals.append(m)
        picks.append(idx if payload is None else jnp.sum(jnp.where(hit, payload, 0), axis=0, keepdims=True))
        s = jnp.where(hit, NEG_INF, s)
    return jnp.concatenate(vals, axis=0), jnp.concatenate(picks, axis=0)


def _candidate_cells():
    return [(a, b) for a in range(PEER_TOPK) for b in range(PEER_TOPK) if (a + 1) * (b + 1) <= PEER_TOPK]


def _peer_topk_kernel(h_ref, wq_ref, keys_ref, eidx_ref, gw_ref):
    q = _dot(h_ref[...].astype(BF16), wq_ref[...])
    half = PEER_DK // 2
    cells = _candidate_cells()
    for h in range(PEER_HEADS):
        tops = []
        for i in range(2):
            col = (h * 2 + i) * half
            s_t = _dot_nt(keys_ref[h * 2 + i], q[:, col:col + half].astype(BF16))
            tops.append(_top16(s_t))
        (s1, i1), (s2, i2) = tops
        cand = jnp.concatenate([s1[a:a + 1] + s2[b:b + 1] for a, b in cells], axis=0)
        cidx = jnp.concatenate([i1[a:a + 1] * N_KEYS + i2[b:b + 1] for a, b in cells], axis=0)
        top_s, eidx = _top16(cand, cidx)
        e = jnp.exp(top_s - top_s[0:1])
        gw_ref[h * PEER_TOPK:(h + 1) * PEER_TOPK, :] = e / jnp.sum(e, axis=0, keepdims=True)
        eidx_ref[h * PEER_TOPK:(h + 1) * PEER_TOPK, :] = eidx


def _peer_topk(h, wq, keys, tm):
    n = h.shape[0]
    half = PEER_DK // 2
    return pl.pallas_call(
        _peer_topk_kernel, grid=(n // tm,),
        in_specs=[pl.BlockSpec((tm, D_MODEL), lambda i: (i, 0)),
                  pl.BlockSpec((D_MODEL, PEER_HEADS * PEER_DK), lambda i: (0, 0)),
                  pl.BlockSpec((PEER_HEADS * 2, N_KEYS, half), lambda i: (0, 0, 0))],
        out_specs=[pl.BlockSpec((PEER_ROWS, tm), lambda i: (0, i)),
                   pl.BlockSpec((PEER_ROWS, tm), lambda i: (0, i))],
        out_shape=[jax.ShapeDtypeStruct((PEER_ROWS, n), I32), jax.ShapeDtypeStruct((PEER_ROWS, n), F32)],
        compiler_params=_params(("parallel",)), name="peer_topk")(h, wq, keys)


def _pack_expert_tables(u_tab, v_tab):
    ub = lax.bitcast_convert_type(u_tab.astype(jnp.bfloat16), jnp.uint16).astype(jnp.uint32)
    vb = lax.bitcast_convert_type(v_tab.astype(jnp.bfloat16), jnp.uint16).astype(jnp.uint32)
    words = lax.bitcast_convert_type((vb << 16) | ub, I32)
    return words.reshape(words.shape[0] * PEER_FEAT_TILES, LANES)


def _peer_gather_kernel(idx_ref, gw_ref, x_ref, lng_ref, lnb_ref, uv_hbm, o_ref,
                        idx_smem, buf, ffn_ref, sem_idx, sem):
    idx_copy = pltpu.make_async_copy(idx_ref, idx_smem, sem_idx)
    idx_copy.start()
    idx_copy.wait()

    tile = PEER_FEAT_TILES

    def issue(t, slot):
        for e in range(PEER_ROWS):
            row = pl.multiple_of(idx_smem[t, e] * tile, tile)
            dst = (slot * PEER_ROWS + e) * tile
            pltpu.make_async_copy(uv_hbm.at[pl.ds(row, tile)], buf.at[pl.ds(dst, tile)], sem.at[slot]).start()

    def wait(slot):
        n = PEER_ROWS * tile
        pltpu.make_async_copy(uv_hbm.at[pl.ds(0, n)], buf.at[pl.ds(slot * n, n)], sem.at[slot]).wait()

    lane_t = lax.broadcasted_iota(I32, (PEER_ROWS, PEER_TB), 1)

    def feature_block(slot, c):
        return buf[pl.ds(slot * PEER_ROWS * tile + c, PEER_ROWS, stride=tile), :]

    def compute(t, slot):
        x_t = x_ref[pl.ds(t, 1), :]
        acc = None
        for c in range(tile):
            u = lax.bitcast_convert_type(feature_block(slot, c) << 16, F32)
            part = u * x_t[:, c * LANES:(c + 1) * LANES]
            acc = part if acc is None else acc + part
        hidden = jnp.sum(acc, axis=1, keepdims=True)
        g_col = jnp.sum(jnp.where(lane_t == t, gw_ref[...], 0.0), axis=1, keepdims=True)
        coef = g_col * jax.nn.gelu(hidden)
        outs = []
        for c in range(tile):
            v = lax.bitcast_convert_type(feature_block(slot, c) & jnp.int32(-65536), F32)
            outs.append(jnp.sum(coef * v, axis=0, keepdims=True))
        ffn_ref[pl.ds(t, 1), :] = jnp.concatenate(outs, axis=1)

    ahead = PEER_SLOTS - 1
    for t0 in range(ahead):
        issue(t0, t0)

    def group(g, carry):
        for s in range(PEER_SLOTS):
            t = g * PEER_SLOTS + s
            wait(s)
            issue(t + ahead, (s + ahead) % PEER_SLOTS)
            compute(t, s)
        return carry

    n_groups = PEER_TB // PEER_SLOTS
    lax.fori_loop(0, n_groups - 1, group, 0)
    for s in range(PEER_SLOTS):
        t = (n_groups - 1) * PEER_SLOTS + s
        wait(s)
        if t + ahead < PEER_TB:
            issue(t + ahead, (s + ahead) % PEER_SLOTS)
        compute(t, s)
    o_ref[...] = _layer_norm(DN_ALPHA * x_ref[...] + ffn_ref[...], lng_ref[...], lnb_ref[...])


def _peer_gather_ln(eidx, gw_t, x, uv_tab, g, b):
    n = x.shape[0]
    fix = lambda i: (0, 0)
    return pl.pallas_call(
        _peer_gather_kernel, grid=(n // PEER_TB,),
        in_specs=[pl.BlockSpec((PEER_TB, PEER_ROWS), lambda i: (i, 0)),
                  pl.BlockSpec((PEER_ROWS, PEER_TB), lambda i: (0, i)),
                  pl.BlockSpec((PEER_TB, D_MODEL), lambda i: (i, 0)),
                  pl.BlockSpec((1, D_MODEL), fix), pl.BlockSpec((1, D_MODEL), fix),
                  pl.BlockSpec(memory_space=pl.ANY)],
        out_specs=pl.BlockSpec((PEER_TB, D_MODEL), lambda i: (i, 0)),
        out_shape=jax.ShapeDtypeStruct((n, D_MODEL), F32),
        scratch_shapes=[pltpu.SMEM((PEER_TB, PEER_ROWS), I32),
                        pltpu.VMEM((PEER_SLOTS * PEER_ROWS * PEER_FEAT_TILES, LANES), I32),
                        pltpu.VMEM((PEER_TB, D_MODEL), F32),
                        pltpu.SemaphoreType.DMA(()),
                        pltpu.SemaphoreType.DMA((PEER_SLOTS,))],
        compiler_params=_params(("arbitrary",)), name="peer_gather",
    )(eidx, gw_t, x, g, b, uv_tab)


def _rope_tables(pos):
    half = HEAD_DIM // 2
    inv = ROPE_THETA ** (-jnp.arange(half, dtype=F32) / half)
    ang = pos.astype(F32)[:, None] * inv
    cos, sin = jnp.cos(ang), jnp.sin(ang)
    return jnp.tile(cos, (1, 4)), jnp.tile(jnp.concatenate([-sin, sin], axis=1), (1, 2))


def _layer_weights(p):
    w_in = p['w_in']
    w_u = w_in[:, :D_SSM].astype(BF16)
    pad = PROJ_W - (w_in.shape[1] - D_SSM)
    w_rest = jnp.pad(w_in[:, D_SSM:], ((0, 0), (0, pad))).astype(BF16)
    cmp_k = _compress_weights(p['cmp_pe'][0], p['cmp_w1'][0], p['cmp_w2'][0])
    cmp_v = _compress_weights(p['cmp_pe'][1], p['cmp_w1'][1], p['cmp_w2'][1])
    return dict(
        w_u=w_u, w_rest=w_rest, w_glu=p['w_glu'].astype(BF16), d=p['d'].reshape(1, D_SSM),
        wo_ssm=p['w_out'][:D_SSM].astype(BF16), wo_att=p['w_out'][D_SSM:].astype(BF16),
        ln1_g=p['ln1_g'].reshape(1, D_MODEL), ln1_b=p['ln1_b'].reshape(1, D_MODEL),
        ln2_g=p['ln2_g'].reshape(1, D_MODEL), ln2_b=p['ln2_b'].reshape(1, D_MODEL),
        peer_wq=p['peer_wq'].astype(BF16),
        peer_keys=p['peer_keys'].reshape(PEER_HEADS * 2, N_KEYS, PEER_DK // 2).astype(BF16),
        peer_uv=_pack_expert_tables(p['peer_u'], p['peer_v']), cmp_k=cmp_k, cmp_v=cmp_v)


def _token_tail(x, y_ssm, y_att, w, tm):
    h = _outproj_ln(y_ssm, y_att, x, w['wo_ssm'], w['wo_att'], w['ln1_g'], w['ln1_b'], tm)
    eidx_t, gw_t = _peer_topk(h, w['peer_wq'], w['peer_keys'], min(tm, 256))
    return _peer_gather_ln(eidx_t.T, gw_t, h, w['peer_uv'], w['ln2_g'], w['ln2_b'])


def _prompt_layer(x, bsz, seq, w, p):
    n = bsz * seq
    tm = min(512, seq)
    cos, sin = _rope_tables(jnp.arange(seq, dtype=I32))
    q, kv, gates = _project(x, w['w_rest'], cos, sin, tm)
    x_tb = x.reshape(bsz, seq, D_MODEL).transpose(1, 0, 2).reshape(n, D_MODEL)
    u_tb = _matmul(x_tb, w['w_u'], tm)
    wb, wc, ar, ai = _s5_weights(p['a_re'], p['a_im'], p['log_dt'], p['b_re'], p['b_im'], p['c_re'], p['c_im'], bsz)
    s0 = jnp.zeros((bsz, 2 * N_STATE), F32)
    gy_tb, s_fin = _s5_scan(u_tb, wb, wc, ar, ai, w['d'], s0, min(64, seq))
    y_ssm_tb = _glu(gy_tb, w['w_glu'], tm)
    y_ssm = y_ssm_tb.reshape(seq, bsz, D_SSM).transpose(1, 0, 2).reshape(n, D_SSM)
    n_cmp = seq // CMP_STRIDE
    xk = kv[:, 0:KV_WIDTH].reshape(bsz, n_cmp, CMP_STRIDE * KV_WIDTH)
    xv = kv[:, KV_WIDTH:2 * KV_WIDTH].reshape(bsz, n_cmp, CMP_STRIDE * KV_WIDTH)
    kc = _compress(xk, *w['cmp_k'])
    vc = _compress(xv, *w['cmp_v'])
    y_att = _attention(q, gates, kv, kc, vc, bsz, seq)
    y = _token_tail(x, y_ssm, y_att, w, tm)
    new_kv = kv[:, :4 * KV_WIDTH].reshape(bsz, seq, 4, N_KV_HEADS, HEAD_DIM)
    new_win = kv[:, 4 * KV_WIDTH:].reshape(bsz, seq, 2, N_KV_HEADS, HEAD_DIM)
    win = jnp.concatenate([jnp.zeros((bsz, WINDOW, 2, N_KV_HEADS, HEAD_DIM), F32), new_win], axis=1)[:, -WINDOW:]
    return y, new_kv, win, s_fin.reshape(bsz, 2, N_GROUPS, SSM_STATE)


def _sample_layer(x, bsz, past, cache, page_table, layer, win_past, ssm_state, w, p):
    rows = x.shape[0]
    cos, sin = _rope_tables(jnp.full((rows,), past, I32))
    q, kv, gates = _project(x, w['w_rest'], cos, sin, rows)
    u = _matmul(x, w['w_u'], rows)
    wb, wc, ar, ai = _s5_weights(p['a_re'], p['a_im'], p['log_dt'], p['b_re'], p['b_im'], p['c_re'], p['c_im'], bsz)
    gy, s_fin = _s5_scan(u[:bsz], wb, wc, ar, ai, w['d'], ssm_state.reshape(bsz, 2 * N_STATE), 1)
    y_ssm = jnp.pad(_glu(gy, w['w_glu'], bsz), ((0, rows - bsz), (0, 0)))
    xk, xv, ks, vs = _page_gather(cache, page_table, layer)
    kc = _compress(xk, *w['cmp_k'])
    vc = _compress(xv, *w['cmp_v'])
    win_len = win_past.shape[1]
    win_rows = win_past.reshape(bsz, win_len, 2 * KV_WIDTH)
    y_att = _decode_attention(q[:bsz].reshape(bsz, 1, D_ATT), gates[:bsz].reshape(bsz, 1, LANES),
                              kv[:bsz].reshape(bsz, 1, 6 * KV_WIDTH), kc, vc, ks, vs, win_rows, past)
    y_att = jnp.pad(y_att.reshape(bsz, D_ATT), ((0, rows - bsz), (0, 0)))
    y = _token_tail(x, y_ssm, y_att, w, rows)
    new_kv = kv[:bsz, :4 * KV_WIDTH].reshape(bsz, 1, 4, N_KV_HEADS, HEAD_DIM)
    new_win = kv[:bsz, 4 * KV_WIDTH:].reshape(bsz, 1, 2, N_KV_HEADS, HEAD_DIM)
    win = jnp.concatenate([win_past, new_win], axis=1)[:, -win_len:]
    return y, new_kv, win, s_fin.reshape(bsz, 2, N_GROUPS, SSM_STATE)


def kernel(x_prompt, x_sample, cache_kv, cache_win, state_ssm, page_table, w_in, ssm_a_re, ssm_a_im, ssm_log_dt, ssm_b_re, ssm_b_im, ssm_c_re, ssm_c_im, ssm_d, w_glu, cmp_pe, cmp_w1, cmp_w2, w_out, ln1_g, ln1_b, peer_wq, peer_keys, peer_u, peer_v, ln2_g, ln2_b):
    bsz, seq, _ = x_prompt.shape
    dec_bsz = x_sample.shape[0]
    depth = w_in.shape[0]
    past = page_table.shape[1] * PAGE_SIZE
    cache = cache_kv.reshape(depth, cache_kv.shape[1], 4 * PAGE_SIZE, KV_WIDTH)
    dec_rows = -(-dec_bsz // PEER_TB) * PEER_TB
    y_p = x_prompt.reshape(bsz * seq, D_MODEL)
    y_s = jnp.pad(x_sample.reshape(dec_bsz, D_MODEL), ((0, dec_rows - dec_bsz), (0, 0)))
    outs = [[] for _ in range(6)]
    for l in range(depth):
        p = {'w_in': w_in[l], 'a_re': ssm_a_re[l], 'a_im': ssm_a_im[l], 'log_dt': ssm_log_dt[l],
             'b_re': ssm_b_re[l], 'b_im': ssm_b_im[l], 'c_re': ssm_c_re[l], 'c_im': ssm_c_im[l],
             'd': ssm_d[l], 'w_glu': w_glu[l], 'cmp_pe': cmp_pe[l], 'cmp_w1': cmp_w1[l], 'cmp_w2': cmp_w2[l],
             'w_out': w_out[l], 'ln1_g': ln1_g[l], 'ln1_b': ln1_b[l], 'peer_wq': peer_wq[l],
             'peer_keys': peer_keys[l], 'peer_u': peer_u[l], 'peer_v': peer_v[l],
             'ln2_g': ln2_g[l], 'ln2_b': ln2_b[l]}
        w = _layer_weights(p)
        y_s, kvn, winn, sn = _sample_layer(y_s, dec_bsz, past, cache, page_table, l, cache_win[l], state_ssm[l], w, p)
        outs[3].append(kvn)
        outs[4].append(winn)
        outs[5].append(sn)
        y_p, kvn, winn, sn = _prompt_layer(y_p, bsz, seq, w, p)
        outs[0].append(kvn)
        outs[1].append(winn)
        outs[2].append(sn)
    return (y_p.reshape(bsz, seq, D_MODEL), y_s[:dec_bsz].reshape(dec_bsz, 1, D_MODEL),
            jnp.stack(outs[0]), jnp.stack(outs[1]), jnp.stack(outs[2]),
            jnp.stack(outs[3]), jnp.stack(outs[4]), jnp.stack(outs[5]))
```

```python
import functools
import math

import jax
import jax.numpy as jnp
from jax import lax
from jax.experimental import pallas as pl
from jax.experimental.pallas import tpu as pltpu

F32 = jnp.float32
BF16 = jnp.bfloat16
I32 = jnp.int32

D_MODEL = 1024
D_SSM = 512
D_ATT = 512
SSM_CH = 16
N_GROUPS = 32
SSM_STATE = 64
N_STATE = N_GROUPS * SSM_STATE
HEAD_DIM = 64
N_HEADS = 8
N_KV_HEADS = 2
GQA = 4
KV_WIDTH = 128
CMP_STRIDE = 16
CMP_LEN = 32
SLC_BLOCK = 64
N_SLC = 16
WINDOW = 512
Q_BLOCK = 64
ROPE_THETA = 10000.0
PEER_HEADS = 8
PEER_DK = 256
N_KEYS = 128
PEER_TOPK = 16
PAGE_SIZE = 128
DN_ALPHA = 4 ** 0.25
LN_EPS = 1e-5
ATT_SCALE = HEAD_DIM ** -0.5
TINY = float(jnp.finfo(jnp.float32).tiny)
NEG_INF = float("-inf")
POS_INF = float("inf")

LANES = 128
PROJ_W = 512 + 6 * KV_WIDTH + LANES
VMEM_LIMIT = 56 * 1024 * 1024
PEER_TB = 128
PEER_SLOTS = 4
PEER_ROWS = PEER_HEADS * PEER_TOPK
PEER_FEAT_TILES = D_MODEL // LANES
PAGES_PER_STEP = 4


def _params(sem, **kw):
    return pltpu.CompilerParams(dimension_semantics=sem, vmem_limit_bytes=VMEM_LIMIT, **kw)


def _dot(a, b):
    return jnp.dot(a, b, preferred_element_type=F32)


def _dot_nt(a, b):
    return lax.dot_general(a, b, (((1,), (1,)), ((), ())), preferred_element_type=F32)


def _dot_exact(a, b):
    return lax.dot_general(a, b, (((1,), (0,)), ((), ())), precision=lax.Precision.HIGHEST,
                           preferred_element_type=F32)


def _layer_norm(x, g, b):
    mu = jnp.mean(x, axis=-1, keepdims=True)
    xc = x - mu
    var = jnp.mean(xc * xc, axis=-1, keepdims=True)
    return xc * lax.rsqrt(var + LN_EPS) * g + b


def _masked_softmax(s, valid):
    s = jnp.where(valid, s, NEG_INF)
    m = jnp.max(s, axis=-1, keepdims=True)
    m = jnp.where(m > NEG_INF, m, 0.0)
    e = jnp.where(valid, jnp.exp(s - m), 0.0)
    den = jnp.maximum(jnp.sum(e, axis=-1, keepdims=True), TINY)
    return e / den


def _mm_kernel(x_ref, w_ref, o_ref):
    o_ref[...] = _dot(x_ref[...].astype(BF16), w_ref[...])


def _matmul(x, w, tm):
    n, k = x.shape
    m = w.shape[1]
    return pl.pallas_call(
        _mm_kernel, grid=(n // tm,),
        in_specs=[pl.BlockSpec((tm, k), lambda i: (i, 0)), pl.BlockSpec((k, m), lambda i: (0, 0))],
        out_specs=pl.BlockSpec((tm, m), lambda i: (i, 0)),
        out_shape=jax.ShapeDtypeStruct((n, m), F32),
        compiler_params=_params(("parallel",)), name="mm")(x, w)


def _proj_kernel(x_ref, w_ref, cos_ref, sin_ref, q_ref, kv_ref, g_ref):
    acc = _dot(x_ref[...].astype(BF16), w_ref[...])
    cos = cos_ref[...]
    sin = sin_ref[...]
    lane = lax.broadcasted_iota(I32, cos.shape, 1)
    first_half = (lane % HEAD_DIM) < (HEAD_DIM // 2)

    def rope(v):
        rot = jnp.where(first_half, pltpu.roll(v, 96, 1), pltpu.roll(v, 32, 1))
        return v * cos + rot * sin

    for j in range(4):
        q_ref[:, j * LANES:(j + 1) * LANES] = rope(acc[:, j * LANES:(j + 1) * LANES])
    for j in range(6):
        blk = acc[:, 512 + j * LANES:512 + (j + 1) * LANES]
        kv_ref[:, j * LANES:(j + 1) * LANES] = rope(blk) if j % 2 == 0 else blk
    g_ref[...] = jax.nn.sigmoid(acc[:, 512 + 6 * LANES:])


def _project(x, w, cos, sin, tm):
    n = x.shape[0]
    tab_blocks = cos.shape[0] // tm
    return pl.pallas_call(
        _proj_kernel, grid=(n // tm,),
        in_specs=[pl.BlockSpec((tm, D_MODEL), lambda i: (i, 0)),
                  pl.BlockSpec((D_MODEL, PROJ_W), lambda i: (0, 0)),
                  pl.BlockSpec((tm, LANES), lambda i: (i % tab_blocks, 0)),
                  pl.BlockSpec((tm, LANES), lambda i: (i % tab_blocks, 0))],
        out_specs=[pl.BlockSpec((tm, 512), lambda i: (i, 0)),
                   pl.BlockSpec((tm, 6 * KV_WIDTH), lambda i: (i, 0)),
                   pl.BlockSpec((tm, LANES), lambda i: (i, 0))],
        out_shape=[jax.ShapeDtypeStruct((n, 512), F32),
                   jax.ShapeDtypeStruct((n, 6 * KV_WIDTH), F32),
                   jax.ShapeDtypeStruct((n, LANES), F32)],
        compiler_params=_params(("parallel",)), name="proj")(x, w, cos, sin)


def _glu_kernel(a_ref, w_ref, o_ref):
    gl = _dot(a_ref[...].astype(BF16), w_ref[...])
    o_ref[...] = gl[:, :D_SSM] * jax.nn.sigmoid(gl[:, D_SSM:])


def _glu(a, w, tm):
    n = a.shape[0]
    return pl.pallas_call(
        _glu_kernel, grid=(n // tm,),
        in_specs=[pl.BlockSpec((tm, D_SSM), lambda i: (i, 0)),
                  pl.BlockSpec((D_SSM, 2 * D_SSM), lambda i: (0, 0))],
        out_specs=pl.BlockSpec((tm, D_SSM), lambda i: (i, 0)),
        out_shape=jax.ShapeDtypeStruct((n, D_SSM), F32),
        compiler_params=_params(("parallel",)), name="glu")(a, w)


def _outproj_kernel(ys_ref, ya_ref, x_ref, w1_ref, w2_ref, g_ref, b_ref, o_ref):
    mix = _dot(ys_ref[...].astype(BF16), w1_ref[...]) + _dot(ya_ref[...].astype(BF16), w2_ref[...])
    o_ref[...] = _layer_norm(DN_ALPHA * x_ref[...] + mix, g_ref[...], b_ref[...])


def _outproj_ln(ys, ya, x, w1, w2, g, b, tm):
    n = x.shape[0]
    row = lambda i: (i, 0)
    fix = lambda i: (0, 0)
    return pl.pallas_call(
        _outproj_kernel, grid=(n // tm,),
        in_specs=[pl.BlockSpec((tm, D_SSM), row), pl.BlockSpec((tm, D_ATT), row),
                  pl.BlockSpec((tm, D_MODEL), row),
                  pl.BlockSpec((D_SSM, D_MODEL), fix), pl.BlockSpec((D_ATT, D_MODEL), fix),
                  pl.BlockSpec((1, D_MODEL), fix), pl.BlockSpec((1, D_MODEL), fix)],
        out_specs=pl.BlockSpec((tm, D_MODEL), row),
        out_shape=jax.ShapeDtypeStruct((n, D_MODEL), F32),
        compiler_params=_params(("parallel",)), name="outproj_ln")(ys, ya, x, w1, w2, g, b)


def _s5_kernel(u_ref, wb_ref, wc_ref, ar_ref, ai_ref, d_ref, s0_ref, gy_ref, sf_ref, bu_ref, st_ref,
               *, n_steps, batch):
    @pl.when(pl.program_id(0) == 0)
    def _():
        st_ref[...] = s0_ref[...]

    u = u_ref[...]
    bu_ref[...] = _dot(u.astype(BF16), wb_ref[...])
    n_chunk = 4
    cw = N_STATE // n_chunk

    def step(t, carry):
        r = pl.multiple_of(t * batch, batch)
        new = []
        for c in range(n_chunk):
            sr, si = carry[c], carry[n_chunk + c]
            ar = ar_ref[:, c * cw:(c + 1) * cw]
            ai = ai_ref[:, c * cw:(c + 1) * cw]
            nr = ar * sr - ai * si + bu_ref[pl.ds(r, batch), c * cw:(c + 1) * cw]
            ni = ar * si + ai * sr + bu_ref[pl.ds(r, batch), N_STATE + c * cw:N_STATE + (c + 1) * cw]
            bu_ref[pl.ds(r, batch), c * cw:(c + 1) * cw] = nr
            bu_ref[pl.ds(r, batch), N_STATE + c * cw:N_STATE + (c + 1) * cw] = ni
            new.append((nr, ni))
        return tuple(p[0] for p in new) + tuple(p[1] for p in new)

    init = tuple(st_ref[:, c * cw:(c + 1) * cw] for c in range(n_chunk)) + \
        tuple(st_ref[:, N_STATE + c * cw:N_STATE + (c + 1) * cw] for c in range(n_chunk))
    fin = lax.fori_loop(0, n_steps, step, init)
    for c in range(n_chunk):
        st_ref[:, c * cw:(c + 1) * cw] = fin[c]
        st_ref[:, N_STATE + c * cw:N_STATE + (c + 1) * cw] = fin[n_chunk + c]
    y = _dot(bu_ref[...].astype(BF16), wc_ref[...]) + d_ref[...] * u
    gy_ref[...] = jax.nn.gelu(y)
    sf_ref[...] = st_ref[...]


def _s5_scan(u_tb, wb, wc, ar, ai, d, s0, n_steps):
    batch = s0.shape[0]
    n = u_tb.shape[0]
    rows = n_steps * batch
    fix = lambda i: (0, 0)
    return pl.pallas_call(
        functools.partial(_s5_kernel, n_steps=n_steps, batch=batch),
        grid=(n // rows,),
        in_specs=[pl.BlockSpec((rows, D_SSM), lambda i: (i, 0)),
                  pl.BlockSpec((D_SSM, 2 * N_STATE), fix), pl.BlockSpec((2 * N_STATE, D_SSM), fix),
                  pl.BlockSpec((batch, N_STATE), fix), pl.BlockSpec((batch, N_STATE), fix),
                  pl.BlockSpec((1, D_SSM), fix), pl.BlockSpec((batch, 2 * N_STATE), fix)],
        out_specs=[pl.BlockSpec((rows, D_SSM), lambda i: (i, 0)),
                   pl.BlockSpec((batch, 2 * N_STATE), fix)],
        out_shape=[jax.ShapeDtypeStruct((n, D_SSM), F32),
                   jax.ShapeDtypeStruct((batch, 2 * N_STATE), F32)],
        scratch_shapes=[pltpu.VMEM((rows, 2 * N_STATE), F32), pltpu.VMEM((batch, 2 * N_STATE), F32)],
        compiler_params=_params(("arbitrary",)), name="s5_scan")(u_tb, wb, wc, ar, ai, d, s0)


def _s5_weights(a_re, a_im, log_dt, b_re, b_im, c_re, c_im, batch):
    dt = jnp.exp(log_dt)[:, None]
    mag = jnp.exp(dt * a_re)
    abar_re, abar_im = mag * jnp.cos(dt * a_im), mag * jnp.sin(dt * a_im)
    den = a_re * a_re + a_im * a_im
    f_re = ((abar_re - 1.0) * a_re + abar_im * a_im) / den
    f_im = (abar_im * a_re - (abar_re - 1.0) * a_im) / den
    bb_re = f_re[..., None] * b_re - f_im[..., None] * b_im
    bb_im = f_re[..., None] * b_im + f_im[..., None] * b_re
    eye = jnp.eye(N_GROUPS, dtype=F32)
    wb_re = jnp.einsum('gpc,gh->gchp', bb_re, eye).reshape(D_SSM, N_STATE)
    wb_im = jnp.einsum('gpc,gh->gchp', bb_im, eye).reshape(D_SSM, N_STATE)
    wb = jnp.concatenate([wb_re, wb_im], axis=1).astype(BF16)
    wc_re = jnp.einsum('gcp,gh->gphc', c_re, eye).reshape(N_STATE, D_SSM)
    wc_im = jnp.einsum('gcp,gh->gphc', c_im, eye).reshape(N_STATE, D_SSM)
    wc = jnp.concatenate([wc_re, -wc_im], axis=0).astype(BF16)
    ar = jnp.broadcast_to(abar_re.reshape(1, N_STATE), (batch, N_STATE))
    ai = jnp.broadcast_to(abar_im.reshape(1, N_STATE), (batch, N_STATE))
    return wb, wc, ar, ai


def _compress_kernel(x_ref, pe_ref, w1a_ref, w1b_ref, w2_ref, o_ref):
    x = x_ref[...]
    n = x.shape[0]
    first = _dot((x + pe_ref[0:1, :]).astype(BF16), w1a_ref[...])
    second = _dot((x + pe_ref[1:2, :]).astype(BF16), w1b_ref[...])
    nxt = pltpu.roll(second, n - 1, 0)
    o_ref[...] = _dot(jax.nn.gelu(first + nxt).astype(BF16), w2_ref[...])


def _compress(x, pe, w1a, w1b, w2):
    bsz, n, width = x.shape
    fix = lambda b: (0, 0)
    return pl.pallas_call(
        _compress_kernel, grid=(bsz,),
        in_specs=[pl.BlockSpec((None, n, width), lambda b: (b, 0, 0)),
                  pl.BlockSpec((2, width), fix), pl.BlockSpec((width, KV_WIDTH), fix),
                  pl.BlockSpec((width, KV_WIDTH), fix), pl.BlockSpec((KV_WIDTH, KV_WIDTH), fix)],
        out_specs=pl.BlockSpec((None, n, KV_WIDTH), lambda b: (b, 0, 0)),
        out_shape=jax.ShapeDtypeStruct((bsz, n, KV_WIDTH), F32),
        compiler_params=_params(("parallel",)), name="compress")(x, pe, w1a, w1b, w2)


def _compress_weights(pe, w1, w2):
    eye = jnp.eye(N_KV_HEADS, dtype=F32)
    w1 = w1.reshape(2, CMP_STRIDE, HEAD_DIM, HEAD_DIM)
    pe = pe.reshape(2, CMP_STRIDE, HEAD_DIM)
    w1a = jnp.einsum('jde,hk->jhdke', w1[0], eye).reshape(CMP_STRIDE * KV_WIDTH, KV_WIDTH).astype(BF16)
    w1b = jnp.einsum('jde,hk->jhdke', w1[1], eye).reshape(CMP_STRIDE * KV_WIDTH, KV_WIDTH).astype(BF16)
    w2d = jnp.einsum('de,hk->hdke', w2, eye).reshape(KV_WIDTH, KV_WIDTH).astype(BF16)
    pe2 = jnp.broadcast_to(pe[:, :, None, :], (2, CMP_STRIDE, N_KV_HEADS, HEAD_DIM)).reshape(2, CMP_STRIDE * KV_WIDTH)
    return pe2, w1a, w1b, w2d


def _sel_matrix(n_cmp_rows, n_blk_cols):
    n = jnp.arange(n_cmp_rows)[:, None]
    j = jnp.arange(n_blk_cols)[None, :]
    per = SLC_BLOCK // CMP_STRIDE
    m = (n // per == j).astype(F32) + ((n + 1) // per == j).astype(F32)
    return jnp.where(n < n_cmp_rows - 1, m, 0.0)


def _block_ranks(score, n_blk):
    blk = lax.broadcasted_iota(I32, score.shape, 1)
    rank = jnp.zeros(score.shape, I32)
    for i in range(n_blk):
        col = score[:, i:i + 1]
        before = (col > score) | ((col == score) & (blk > i))
        rank = rank + before.astype(I32)
    return rank


def _attn_kernel(q_ref, g_ref, kc_ref, vc_ref, ks_ref, vs_ref, kw_ref, vw_ref, msel_ref, e3_ref, eg_ref,
                 o_ref, *, seq, ck, wk):
    qb = pl.program_id(1)
    q0 = qb * Q_BLOCK
    n_cmp = seq // CMP_STRIDE
    n_blk = seq // SLC_BLOCK
    n_sel = min(N_SLC, n_blk)
    rows = GQA * Q_BLOCK
    pos = q0 + lax.broadcasted_iota(I32, (Q_BLOCK, 1), 0)
    pos4 = jnp.concatenate([pos] * GQA, axis=0)
    lane_r = lax.broadcasted_iota(I32, (rows, LANES), 1)
    kc = kc_ref[...].astype(BF16)
    vc = vc_ref[...].astype(BF16)
    n_chunks = (q0 + Q_BLOCK + ck - 1) // ck
    kstart = pl.multiple_of(jnp.maximum(q0 + Q_BLOCK - wk, 0), Q_BLOCK)
    kw = kw_ref[pl.ds(kstart, wk), :].astype(BF16)
    vw = vw_ref[pl.ds(kstart, wk), :].astype(BF16)
    heads_c, heads_s, heads_w = [], [], []
    for hkv in range(N_KV_HEADS):
        parts = []
        for g in range(GQA):
            hd = hkv * GQA + g
            slab = q_ref[:, (hd // 2) * LANES:(hd // 2 + 1) * LANES]
            if hd % 2 != hkv:
                slab = pltpu.roll(slab, HEAD_DIM, 1)
            parts.append(slab)
        qh = jnp.concatenate(parts, axis=0)
        qh = jnp.where(lane_r // HEAD_DIM == hkv, qh * ATT_SCALE, 0.0).astype(BF16)

        s_c = _dot_nt(qh, kc)
        n_id = lax.broadcasted_iota(I32, s_c.shape, 1)
        p_c = _masked_softmax(s_c, n_id * CMP_STRIDE + (CMP_LEN - 1) <= pos4)
        o_c = _dot(p_c.astype(BF16), vc)
        p_grp = p_c[0:Q_BLOCK]
        for g in range(1, GQA):
            p_grp = p_grp + p_c[g * Q_BLOCK:(g + 1) * Q_BLOCK]
        p_slc = _dot_exact(p_grp, msel_ref[...])

        blk = lax.broadcasted_iota(I32, p_slc.shape, 1)
        cur = pos // SLC_BLOCK
        forced = (blk == 0) | (blk == cur) | (blk == cur - 1)
        future = blk * SLC_BLOCK > pos
        score = jnp.where(future, NEG_INF, jnp.where(forced, POS_INF, p_slc))
        sel = (_block_ranks(score, n_blk) < n_sel).astype(BF16)

        def chunk(c, carry):
            m, l, acc = carry
            base = pl.multiple_of(c * ck, ck)
            k = ks_ref[pl.ds(base, ck), :].astype(BF16)
            v = vs_ref[pl.ds(base, ck), :].astype(BF16)
            s = _dot_nt(qh, k)
            hit = _dot(sel, e3_ref[c])
            kpos = base + lax.broadcasted_iota(I32, hit.shape, 1)
            ok = (hit > 0.5) & (kpos <= pos)
            ok4 = jnp.concatenate([ok] * GQA, axis=0)
            s = jnp.where(ok4, s, NEG_INF)
            m_new = jnp.maximum(m, jnp.max(s, axis=-1, keepdims=True))
            m_use = jnp.where(m_new > NEG_INF, m_new, 0.0)
            alpha = jnp.exp(m - m_use)
            e = jnp.where(ok4, jnp.exp(s - m_use), 0.0)
            l = alpha * l + jnp.sum(e, axis=-1, keepdims=True)
            acc = alpha * acc + _dot(e.astype(BF16), v)
            return m_new, l, acc

        init = (jnp.full((rows, 1), NEG_INF, F32), jnp.zeros((rows, 1), F32), jnp.zeros((rows, LANES), F32))
        _, l_s, acc_s = lax.fori_loop(0, n_chunks, chunk, init)
        o_s = acc_s / jnp.maximum(l_s, TINY)

        s_w = _dot_nt(qh, kw)
        dpos = pos4 - (kstart + lax.broadcasted_iota(I32, s_w.shape, 1))
        p_w = _masked_softmax(s_w, (dpos >= 0) & (dpos < WINDOW))
        o_w = _dot(p_w.astype(BF16), vw)

        for g in range(GQA):
            sl = (slice(g * Q_BLOCK, (g + 1) * Q_BLOCK), slice(hkv * HEAD_DIM, (hkv + 1) * HEAD_DIM))
            heads_c.append(o_c[sl])
            heads_s.append(o_s[sl])
            heads_w.append(o_w[sl])

    gates = g_ref[...]
    out = jnp.zeros((Q_BLOCK, D_ATT), F32)
    for t, heads in enumerate((heads_c, heads_s, heads_w)):
        out = out + _dot_exact(gates, eg_ref[t]) * jnp.concatenate(heads, axis=1)
    o_ref[...] = out


def _gate_expand():
    c = jnp.arange(LANES)[None, :, None]
    t = jnp.arange(3)[:, None, None]
    h = (jnp.arange(D_ATT) // HEAD_DIM)[None, None, :]
    return (c == t * N_HEADS + h).astype(F32)


def _attention(q, gates, kv, kc, vc, bsz, seq):
    ck = min(512, seq)
    wk = min(WINDOW + Q_BLOCK, seq)
    n_qb = seq // Q_BLOCK
    n_cmp = seq // CMP_STRIDE
    n_blk = seq // SLC_BLOCK
    msel = _sel_matrix(n_cmp, n_blk)
    key_blk = (jnp.arange(seq) // SLC_BLOCK).reshape(seq // ck, 1, ck)
    e3 = (key_blk == jnp.arange(n_blk)[None, :, None]).astype(BF16)
    eg = _gate_expand()
    fix2 = lambda b, i: (0, 0)
    fix3 = lambda b, i: (0, 0, 0)
    kv_spec = lambda col: pl.BlockSpec((seq, KV_WIDTH), lambda b, i: (b, col))
    return pl.pallas_call(
        functools.partial(_attn_kernel, seq=seq, ck=ck, wk=wk),
        grid=(bsz, n_qb),
        in_specs=[pl.BlockSpec((Q_BLOCK, D_ATT), lambda b, i: (b * n_qb + i, 0)),
                  pl.BlockSpec((Q_BLOCK, LANES), lambda b, i: (b * n_qb + i, 0)),
                  pl.BlockSpec((None, n_cmp, KV_WIDTH), lambda b, i: (b, 0, 0)),
                  pl.BlockSpec((None, n_cmp, KV_WIDTH), lambda b, i: (b, 0, 0)),
                  kv_spec(2), kv_spec(3), kv_spec(4), kv_spec(5),
                  pl.BlockSpec((n_cmp, n_blk), fix2),
                  pl.BlockSpec((seq // ck, n_blk, ck), fix3),
                  pl.BlockSpec((3, LANES, D_ATT), fix3)],
        out_specs=pl.BlockSpec((Q_BLOCK, D_ATT), lambda b, i: (b * n_qb + i, 0)),
        out_shape=jax.ShapeDtypeStruct((bsz * seq, D_ATT), F32),
        compiler_params=_params(("parallel", "arbitrary")), name="nsa_prompt",
    )(q, gates, kc, vc, kv, kv, kv, kv, msel, e3, eg)


def _page_gather_kernel(pt_ref, *refs):
    del pt_ref
    pages = refs[:PAGES_PER_STEP]
    xk_ref, xv_ref, ks_ref, vs_ref = refs[PAGES_PER_STEP:]
    per_page = PAGE_SIZE // CMP_STRIDE
    for i, page_ref in enumerate(pages):
        rows = slice(i * per_page, (i + 1) * per_page)
        for j in range(CMP_STRIDE):
            cols = slice(j * KV_WIDTH, (j + 1) * KV_WIDTH)
            xk_ref[rows, cols] = page_ref[pl.ds(4 * j, per_page, stride=4 * CMP_STRIDE), :]
            xv_ref[rows, cols] = page_ref[pl.ds(4 * j + 1, per_page, stride=4 * CMP_STRIDE), :]
        ks_ref[i * PAGE_SIZE:(i + 1) * PAGE_SIZE, :] = page_ref[pl.ds(2, PAGE_SIZE, stride=4), :]
        vs_ref[i * PAGE_SIZE:(i + 1) * PAGE_SIZE, :] = page_ref[pl.ds(3, PAGE_SIZE, stride=4), :]


def _page_gather(cache, page_table, layer):
    bsz, n_pages = page_table.shape
    past = n_pages * PAGE_SIZE
    per_step = PAGES_PER_STEP * (PAGE_SIZE // CMP_STRIDE)
    width = CMP_STRIDE * KV_WIDTH

    def page_spec(i):
        return pl.BlockSpec((None, None, 4 * PAGE_SIZE, KV_WIDTH),
                            lambda b, p, pt: (layer, pt[b, p * PAGES_PER_STEP + i], 0, 0))

    grid_spec = pltpu.PrefetchScalarGridSpec(
        num_scalar_prefetch=1, grid=(bsz, n_pages // PAGES_PER_STEP),
        in_specs=[page_spec(i) for i in range(PAGES_PER_STEP)],
        out_specs=[pl.BlockSpec((None, per_step, width), lambda b, p, pt: (b, p, 0)),
                   pl.BlockSpec((None, per_step, width), lambda b, p, pt: (b, p, 0)),
                   pl.BlockSpec((None, PAGES_PER_STEP * PAGE_SIZE, KV_WIDTH), lambda b, p, pt: (b, p, 0)),
                   pl.BlockSpec((None, PAGES_PER_STEP * PAGE_SIZE, KV_WIDTH), lambda b, p, pt: (b, p, 0))])
    return pl.pallas_call(
        _page_gather_kernel, grid_spec=grid_spec,
        out_shape=[jax.ShapeDtypeStruct((bsz, past // CMP_STRIDE, width), F32),
                   jax.ShapeDtypeStruct((bsz, past // CMP_STRIDE, width), F32),
                   jax.ShapeDtypeStruct((bsz, past, KV_WIDTH), F32),
                   jax.ShapeDtypeStruct((bsz, past, KV_WIDTH), F32)],
        compiler_params=_params(("parallel", "arbitrary")), name="page_gather",
    )(page_table, *([cache] * PAGES_PER_STEP))


def _dec_attn_kernel(q_ref, g_ref, nkv_ref, kc_ref, vc_ref, ks_ref, vs_ref, win_ref, msel_ref, e_ref, eg_ref,
                     o_ref, *, past, cb):
    pos = past
    n_blk = past // SLC_BLOCK + 1
    n_sel = min(N_SLC, n_blk)
    n_chunks = (past // SLC_BLOCK) // cb
    ckeys = cb * SLC_BLOCK
    win_len = win_ref.shape[0]
    row = lax.broadcasted_iota(I32, (N_HEADS, LANES), 0)
    lane = lax.broadcasted_iota(I32, (N_HEADS, LANES), 1)

    qrow = jnp.broadcast_to(q_ref[...], (N_HEADS, D_ATT))
    x = jnp.zeros((N_HEADS, LANES), F32)
    for j in range(4):
        x = jnp.where(row // 2 == j, qrow[:, j * LANES:(j + 1) * LANES], x)
    x = jnp.where((row % 2) != (row // GQA), pltpu.roll(x, HEAD_DIM, 1), x)
    q8 = jnp.where(lane // HEAD_DIM == row // GQA, x * ATT_SCALE, 0.0).astype(BF16)
    nkv = nkv_ref[...]

    def new_rows(col):
        return jnp.broadcast_to(nkv[:, col * KV_WIDTH:(col + 1) * KV_WIDTH], (N_HEADS, KV_WIDTH)).astype(BF16)

    s_c = _dot_nt(q8, kc_ref[...].astype(BF16))
    n_id = lax.broadcasted_iota(I32, s_c.shape, 1)
    p_c = _masked_softmax(s_c, n_id * CMP_STRIDE + (CMP_LEN - 1) <= pos)
    o_c = _dot(p_c.astype(BF16), vc_ref[...].astype(BF16))
    grp0 = jnp.sum(p_c[0:GQA], axis=0, keepdims=True)
    grp1 = jnp.sum(p_c[GQA:2 * GQA], axis=0, keepdims=True)
    p_grp = jnp.where(lax.broadcasted_iota(I32, p_c.shape, 0) < GQA, grp0, grp1)
    p_slc = _dot_exact(p_grp, msel_ref[...])
    blk = lax.broadcasted_iota(I32, p_slc.shape, 1)
    cur = pos // SLC_BLOCK
    forced = (blk == 0) | (blk == cur) | (blk == cur - 1)
    future = (blk * SLC_BLOCK > pos) | (blk >= n_blk)
    score = jnp.where(future, NEG_INF, jnp.where(forced, POS_INF, p_slc))
    sel = (_block_ranks(score, n_blk) < n_sel).astype(F32)

    k_new = new_rows(2)
    v_new = new_rows(3)
    ok_new = sel[:, cur:cur + 1] > 0.5
    s_new = jnp.where(ok_new, _dot_nt(q8, k_new)[:, 0:1], NEG_INF)
    scores, oks = [], []
    m = s_new
    for c in range(n_chunks):
        s = _dot_nt(q8, ks_ref[c * ckeys:(c + 1) * ckeys, :].astype(BF16))
        ok = _dot(sel[:, c * cb:(c + 1) * cb].astype(BF16), e_ref[...]) > 0.5
        s = jnp.where(ok, s, NEG_INF)
        m = jnp.maximum(m, jnp.max(s, axis=-1, keepdims=True))
        scores.append(s)
        oks.append(ok)
    m = jnp.where(m > NEG_INF, m, 0.0)
    e_new = jnp.where(ok_new, jnp.exp(s_new - m), 0.0)
    den = e_new
    acc = e_new.astype(BF16).astype(F32) * v_new.astype(F32)
    for c in range(n_chunks):
        e = jnp.where(oks[c], jnp.exp(scores[c] - m), 0.0)
        den = den + jnp.sum(e, axis=-1, keepdims=True)
        acc = acc + _dot(e.astype(BF16), vs_ref[c * ckeys:(c + 1) * ckeys, :].astype(BF16))
    o_s = acc / jnp.maximum(den, TINY)

    kw = win_ref[:, 0:KV_WIDTH].astype(BF16)
    vw = win_ref[:, KV_WIDTH:2 * KV_WIDTH].astype(BF16)
    s_w = _dot_nt(q8, kw)
    dist = win_len - lax.broadcasted_iota(I32, s_w.shape, 1)
    ok_w = (dist < WINDOW) & (pos - dist >= 0)
    s_w = jnp.where(ok_w, s_w, NEG_INF)
    s_wn = _dot_nt(q8, new_rows(4))[:, 0:1]
    m_w = jnp.maximum(jnp.max(s_w, axis=-1, keepdims=True), s_wn)
    e_w = jnp.where(ok_w, jnp.exp(s_w - m_w), 0.0)
    e_wn = jnp.exp(s_wn - m_w)
    den_w = jnp.sum(e_w, axis=-1, keepdims=True) + e_wn
    o_w = (_dot(e_w.astype(BF16), vw) + e_wn.astype(BF16).astype(F32) * new_rows(5).astype(F32)) / den_w

    gates = g_ref[...]
    out = jnp.zeros((1, D_ATT), F32)
    for t, o8 in enumerate((o_c, o_s, o_w)):
        pieces = [o8[hd:hd + 1, (hd // GQA) * HEAD_DIM:(hd // GQA + 1) * HEAD_DIM] for hd in range(N_HEADS)]
        g8 = jnp.broadcast_to(gates, (N_HEADS, LANES))
        out = out + _dot_exact(g8, eg_ref[t])[0:1] * jnp.concatenate(pieces, axis=1)
    o_ref[...] = out


def _decode_attention(q, gates, nkv, kc, vc, ks, vs, win, past):
    bsz = q.shape[0]
    n_cmp = kc.shape[1]
    n_past_blk = past // SLC_BLOCK
    cb = min(64, n_past_blk)
    n_blk_pad = -(-(n_past_blk + 1) // LANES) * LANES
    msel = _sel_matrix(n_cmp, n_blk_pad)
    e = (jnp.arange(cb * SLC_BLOCK)[None, :] // SLC_BLOCK == jnp.arange(cb)[:, None]).astype(BF16)
    eg = _gate_expand()
    win_len = win.shape[1]
    per_b = lambda b: (b, 0, 0)
    fix2 = lambda b: (0, 0)
    fix3 = lambda b: (0, 0, 0)
    once = pl.Buffered(1)
    return pl.pallas_call(
        functools.partial(_dec_attn_kernel, past=past, cb=cb),
        grid=(bsz,),
        in_specs=[pl.BlockSpec((None, 1, D_ATT), per_b), pl.BlockSpec((None, 1, LANES), per_b),
                  pl.BlockSpec((None, 1, 6 * KV_WIDTH), per_b),
                  pl.BlockSpec((None, n_cmp, KV_WIDTH), per_b), pl.BlockSpec((None, n_cmp, KV_WIDTH), per_b),
                  pl.BlockSpec((None, past, KV_WIDTH), per_b, pipeline_mode=once),
                  pl.BlockSpec((None, past, KV_WIDTH), per_b, pipeline_mode=once),
                  pl.BlockSpec((None, win_len, 2 * KV_WIDTH), per_b),
                  pl.BlockSpec((n_cmp, n_blk_pad), fix2), pl.BlockSpec((cb, cb * SLC_BLOCK), fix2),
                  pl.BlockSpec((3, LANES, D_ATT), fix3)],
        out_specs=pl.BlockSpec((None, 1, D_ATT), per_b),
        out_shape=jax.ShapeDtypeStruct((bsz, 1, D_ATT), F32),
        compiler_params=_params(("parallel",)), name="nsa_decode",
    )(q, gates, nkv, kc, vc, ks, vs, win, msel, e, eg)


def _top16(s, payload=None):
    n = s.shape[0]
    rid = lax.broadcasted_iota(I32, s.shape, 0)
    vals, picks = [], []
    for _ in range(PEER_TOPK):
        m = jnp.max(s, axis=0, keepdims=True)
        idx = jnp.min(jnp.where(s == m, rid, n), axis=0, keepdims=True)
        hit = rid == idx
        vals.append(m)
        picks.append(idx if payload is None else jnp.sum(jnp.where(hit, payload, 0), axis=0, keepdims=True))
        s = jnp.where(hit, NEG_INF, s)
    return jnp.concatenate(vals, axis=0), jnp.concatenate(picks, axis=0)


def _candidate_cells():
    return [(a, b) for a in range(PEER_TOPK) for b in range(PEER_TOPK) if (a + 1) * (b + 1) <= PEER_TOPK]


def _peer_topk_kernel(h_ref, wq_ref, keys_ref, eidx_ref, gw_ref):
    q = _dot(h_ref[...].astype(BF16), wq_ref[...])
    half = PEER_DK // 2
    cells = _candidate_cells()
    for h in range(PEER_HEADS):
        tops = []
        for i in range(2):
            col = (h * 2 + i) * half
            s_t = _dot_nt(keys_ref[h * 2 + i], q[:, col:col + half].astype(BF16))
            tops.append(_top16(s_t))
        (s1, i1), (s2, i2) = tops
        cand = jnp.concatenate([s1[a:a + 1] + s2[b:b + 1] for a, b in cells], axis=0)
        cidx = jnp.concatenate([i1[a:a + 1] * N_KEYS + i2[b:b + 1] for a, b in cells], axis=0)
        top_s, eidx = _top16(cand, cidx)
        e = jnp.exp(top_s - top_s[0:1])
        gw_ref[h * PEER_TOPK:(h + 1) * PEER_TOPK, :] = e / jnp.sum(e, axis=0, keepdims=True)
        eidx_ref[h * PEER_TOPK:(h + 1) * PEER_TOPK, :] = eidx


def _peer_topk(h, wq, keys, tm):
    n = h.shape[0]
    half = PEER_DK // 2
    return pl.pallas_call(
        _peer_topk_kernel, grid=(n // tm,),
        in_specs=[pl.BlockSpec((tm, D_MODEL), lambda i: (i, 0)),
                  pl.BlockSpec((D_MODEL, PEER_HEADS * PEER_DK), lambda i: (0, 0)),
                  pl.BlockSpec((PEER_HEADS * 2, N_KEYS, half), lambda i: (0, 0, 0))],
        out_specs=[pl.BlockSpec((PEER_ROWS, tm), lambda i: (0, i)),
                   pl.BlockSpec((PEER_ROWS, tm), lambda i: (0, i))],
        out_shape=[jax.ShapeDtypeStruct((PEER_ROWS, n), I32), jax.ShapeDtypeStruct((PEER_ROWS, n), F32)],
        compiler_params=_params(("parallel",)), name="peer_topk")(h, wq, keys)


def _pack_expert_tables(u_tab, v_tab):
    ub = lax.bitcast_convert_type(u_tab.astype(jnp.bfloat16), jnp.uint16).astype(jnp.uint32)
    vb = lax.bitcast_convert_type(v_tab.astype(jnp.bfloat16), jnp.uint16).astype(jnp.uint32)
    words = lax.bitcast_convert_type((vb << 16) | ub, I32)
    return words.reshape(words.shape[0] * PEER_FEAT_TILES, LANES)


def _peer_gather_kernel(idx_ref, gw_ref, x_ref, lng_ref, lnb_ref, uv_hbm, o_ref, idx_smem, *scratch):
    bufs = scratch[:PEER_SLOTS]
    ffn_ref, sem_idx, sem = scratch[PEER_SLOTS:]
    idx_copy = pltpu.make_async_copy(idx_ref, idx_smem, sem_idx)
    idx_copy.start()
    idx_copy.wait()

    tile = PEER_FEAT_TILES

    def issue(t, slot):
        for e in range(PEER_ROWS):
            row = pl.multiple_of(idx_smem[t, e], tile)
            copy = pltpu.make_async_copy(uv_hbm.at[pl.ds(row, tile)], bufs[slot].at[pl.ds(e * tile, tile)],
                                         sem.at[slot])
            copy.start(priority=e % 2)

    def wait(slot):
        pltpu.make_async_copy(uv_hbm.at[pl.ds(0, PEER_ROWS * tile)], bufs[slot], sem.at[slot]).wait()

    lane_t = lax.broadcasted_iota(I32, (PEER_ROWS, PEER_TB), 1)

    def feature_block(slot, c):
        return bufs[slot][pl.ds(c, PEER_ROWS, stride=tile), :]

    def compute(t, slot):
        x_t = x_ref[pl.ds(t, 1), :]
        acc = None
        for c in range(tile):
            u = lax.bitcast_convert_type(feature_block(slot, c) << 16, F32)
            part = u * x_t[:, c * LANES:(c + 1) * LANES]
            acc = part if acc is None else acc + part
        hidden = jnp.sum(acc, axis=1, keepdims=True)
        g_col = jnp.sum(jnp.where(lane_t == t, gw_ref[...], 0.0), axis=1, keepdims=True)
        coef = g_col * jax.nn.gelu(hidden)
        outs = []
        for c in range(tile):
            v = lax.bitcast_convert_type(feature_block(slot, c) & jnp.int32(-65536), F32)
            outs.append(jnp.sum(coef * v, axis=0, keepdims=True))
        ffn_ref[pl.ds(t, 1), :] = jnp.concatenate(outs, axis=1)

    ahead = PEER_SLOTS - 1
    for t0 in range(ahead):
        issue(t0, t0)

    def group(g, carry):
        for s in range(PEER_SLOTS):
            t = g * PEER_SLOTS + s
            wait(s)
            issue(t + ahead, (s + ahead) % PEER_SLOTS)
            compute(t, s)
        return carry

    n_groups = PEER_TB // PEER_SLOTS
    lax.fori_loop(0, n_groups - 1, group, 0)
    for s in range(PEER_SLOTS):
        t = (n_groups - 1) * PEER_SLOTS + s
        wait(s)
        if t + ahead < PEER_TB:
            issue(t + ahead, (s + ahead) % PEER_SLOTS)
        compute(t, s)
    o_ref[...] = _layer_norm(DN_ALPHA * x_ref[...] + ffn_ref[...], lng_ref[...], lnb_ref[...])


def _peer_gather_ln(eidx, gw_t, x, uv_tab, g, b):
    n = x.shape[0]
    fix = lambda i: (0, 0)
    return pl.pallas_call(
        _peer_gather_kernel, grid=(n // PEER_TB,),
        in_specs=[pl.BlockSpec((PEER_TB, PEER_ROWS), lambda i: (i, 0)),
                  pl.BlockSpec((PEER_ROWS, PEER_TB), lambda i: (0, i)),
                  pl.BlockSpec((PEER_TB, D_MODEL), lambda i: (i, 0)),
                  pl.BlockSpec((1, D_MODEL), fix), pl.BlockSpec((1, D_MODEL), fix),
                  pl.BlockSpec(memory_space=pl.ANY)],
        out_specs=pl.BlockSpec((PEER_TB, D_MODEL), lambda i: (i, 0)),
        out_shape=jax.ShapeDtypeStruct((n, D_MODEL), F32),
        scratch_shapes=[pltpu.SMEM((PEER_TB, PEER_ROWS), I32)] +
                       [pltpu.VMEM((PEER_ROWS * PEER_FEAT_TILES, LANES), I32) for _ in range(PEER_SLOTS)] +
                       [pltpu.VMEM((PEER_TB, D_MODEL), F32),
                        pltpu.SemaphoreType.DMA(()),
                        pltpu.SemaphoreType.DMA((PEER_SLOTS,))],
        compiler_params=_params(("arbitrary",)), name="peer_gather",
    )(eidx, gw_t, x, g, b, uv_tab)


def _rope_tables(pos):
    half = HEAD_DIM // 2
    inv = ROPE_THETA ** (-jnp.arange(half, dtype=F32) / half)
    ang = pos.astype(F32)[:, None] * inv
    cos, sin = jnp.cos(ang), jnp.sin(ang)
    return jnp.tile(cos, (1, 4)), jnp.tile(jnp.concatenate([-sin, sin], axis=1), (1, 2))


def _layer_weights(p):
    w_in = p['w_in']
    w_u = w_in[:, :D_SSM].astype(BF16)
    pad = PROJ_W - (w_in.shape[1] - D_SSM)
    w_rest = jnp.pad(w_in[:, D_SSM:], ((0, 0), (0, pad))).astype(BF16)
    cmp_k = _compress_weights(p['cmp_pe'][0], p['cmp_w1'][0], p['cmp_w2'][0])
    cmp_v = _compress_weights(p['cmp_pe'][1], p['cmp_w1'][1], p['cmp_w2'][1])
    return dict(
        w_u=w_u, w_rest=w_rest, w_glu=p['w_glu'].astype(BF16), d=p['d'].reshape(1, D_SSM),
        wo_ssm=p['w_out'][:D_SSM].astype(BF16), wo_att=p['w_out'][D_SSM:].astype(BF16),
        ln1_g=p['ln1_g'].reshape(1, D_MODEL), ln1_b=p['ln1_b'].reshape(1, D_MODEL),
        ln2_g=p['ln2_g'].reshape(1, D_MODEL), ln2_b=p['ln2_b'].reshape(1, D_MODEL),
        peer_wq=p['peer_wq'].astype(BF16),
        peer_keys=p['peer_keys'].reshape(PEER_HEADS * 2, N_KEYS, PEER_DK // 2).astype(BF16),
        peer_uv=_pack_expert_tables(p['peer_u'], p['peer_v']), cmp_k=cmp_k, cmp_v=cmp_v)


def _token_tail(x, y_ssm, y_att, w, tm):
    h = _outproj_ln(y_ssm, y_att, x, w['wo_ssm'], w['wo_att'], w['ln1_g'], w['ln1_b'], tm)
    eidx_t, gw_t = _peer_topk(h, w['peer_wq'], w['peer_keys'], min(tm, 256))
    rows = eidx_t.T * PEER_FEAT_TILES
    return _peer_gather_ln(rows, gw_t, h, w['peer_uv'], w['ln2_g'], w['ln2_b'])


def _prompt_layer(x, bsz, seq, w, p):
    n = bsz * seq
    tm = min(512, seq)
    cos, sin = _rope_tables(jnp.arange(seq, dtype=I32))
    q, kv, gates = _project(x, w['w_rest'], cos, sin, tm)
    x_tb = x.reshape(bsz, seq, D_MODEL).transpose(1, 0, 2).reshape(n, D_MODEL)
    u_tb = _matmul(x_tb, w['w_u'], tm)
    wb, wc, ar, ai = _s5_weights(p['a_re'], p['a_im'], p['log_dt'], p['b_re'], p['b_im'], p['c_re'], p['c_im'], bsz)
    s0 = jnp.zeros((bsz, 2 * N_STATE), F32)
    gy_tb, s_fin = _s5_scan(u_tb, wb, wc, ar, ai, w['d'], s0, min(64, seq))
    y_ssm_tb = _glu(gy_tb, w['w_glu'], tm)
    y_ssm = y_ssm_tb.reshape(seq, bsz, D_SSM).transpose(1, 0, 2).reshape(n, D_SSM)
    n_cmp = seq // CMP_STRIDE
    xk = kv[:, 0:KV_WIDTH].reshape(bsz, n_cmp, CMP_STRIDE * KV_WIDTH)
    xv = kv[:, KV_WIDTH:2 * KV_WIDTH].reshape(bsz, n_cmp, CMP_STRIDE * KV_WIDTH)
    kc = _compress(xk, *w['cmp_k'])
    vc = _compress(xv, *w['cmp_v'])
    y_att = _attention(q, gates, kv, kc, vc, bsz, seq)
    y = _token_tail(x, y_ssm, y_att, w, tm)
    new_kv = kv[:, :4 * KV_WIDTH].reshape(bsz, seq, 4, N_KV_HEADS, HEAD_DIM)
    new_win = kv[:, 4 * KV_WIDTH:].reshape(bsz, seq, 2, N_KV_HEADS, HEAD_DIM)
    win = jnp.concatenate([jnp.zeros((bsz, WINDOW, 2, N_KV_HEADS, HEAD_DIM), F32), new_win], axis=1)[:, -WINDOW:]
    return y, new_kv, win, s_fin.reshape(bsz, 2, N_GROUPS, SSM_STATE)


def _sample_layer(x, bsz, past, cache, page_table, layer, win_past, ssm_state, w, p):
    rows = x.shape[0]
    cos, sin = _rope_tables(jnp.full((rows,), past, I32))
    q, kv, gates = _project(x, w['w_rest'], cos, sin, rows)
    u = _matmul(x, w['w_u'], rows)
    wb, wc, ar, ai = _s5_weights(p['a_re'], p['a_im'], p['log_dt'], p['b_re'], p['b_im'], p['c_re'], p['c_im'], bsz)
    gy, s_fin = _s5_scan(u[:bsz], wb, wc, ar, ai, w['d'], ssm_state.reshape(bsz, 2 * N_STATE), 1)
    y_ssm = jnp.pad(_glu(gy, w['w_glu'], bsz), ((0, rows - bsz), (0, 0)))
    xk, xv, ks, vs = _page_gather(cache, page_table, layer)
    kc = _compress(xk, *w['cmp_k'])
    vc = _compress(xv, *w['cmp_v'])
    win_len = win_past.shape[1]
    win_rows = win_past.reshape(bsz, win_len, 2 * KV_WIDTH)
    y_att = _decode_attention(q[:bsz].reshape(bsz, 1, D_ATT), gates[:bsz].reshape(bsz, 1, LANES),
                              kv[:bsz].reshape(bsz, 1, 6 * KV_WIDTH), kc, vc, ks, vs, win_rows, past)
    y_att = jnp.pad(y_att.reshape(bsz, D_ATT), ((0, rows - bsz), (0, 0)))
    y = _token_tail(x, y_ssm, y_att, w, rows)
    new_kv = kv[:bsz, :4 * KV_WIDTH].reshape(bsz, 1, 4, N_KV_HEADS, HEAD_DIM)
    new_win = kv[:bsz, 4 * KV_WIDTH:].reshape(bsz, 1, 2, N_KV_HEADS, HEAD_DIM)
    win = jnp.concatenate([win_past, new_win], axis=1)[:, -win_len:]
    return y, new_kv, win, s_fin.reshape(bsz, 2, N_GROUPS, SSM_STATE)


def kernel(x_prompt, x_sample, cache_kv, cache_win, state_ssm, page_table, w_in, ssm_a_re, ssm_a_im, ssm_log_dt, ssm_b_re, ssm_b_im, ssm_c_re, ssm_c_im, ssm_d, w_glu, cmp_pe, cmp_w1, cmp_w2, w_out, ln1_g, ln1_b, peer_wq, peer_keys, peer_u, peer_v, ln2_g, ln2_b):
    bsz, seq, _ = x_prompt.shape
    dec_bsz = x_sample.shape[0]
    depth = w_in.shape[0]
    past = page_table.shape[1] * PAGE_SIZE
    cache = cache_kv.reshape(depth, cache_kv.shape[1], 4 * PAGE_SIZE, KV_WIDTH)
    dec_rows = -(-dec_bsz // PEER_TB) * PEER_TB
    y_p = x_prompt.reshape(bsz * seq, D_MODEL)
    y_s = jnp.pad(x_sample.reshape(dec_bsz, D_MODEL), ((0, dec_rows - dec_bsz), (0, 0)))
    outs = [[] for _ in range(6)]
    for l in range(depth):
        p = {'w_in': w_in[l], 'a_re': ssm_a_re[l], 'a_im': ssm_a_im[l], 'log_dt': ssm_log_dt[l],
             'b_re': ssm_b_re[l], 'b_im': ssm_b_im[l], 'c_re': ssm_c_re[l], 'c_im': ssm_c_im[l],
             'd': ssm_d[l], 'w_glu': w_glu[l], 'cmp_pe': cmp_pe[l], 'cmp_w1': cmp_w1[l], 'cmp_w2': cmp_w2[l],
             'w_out': w_out[l], 'ln1_g': ln1_g[l], 'ln1_b': ln1_b[l], 'peer_wq': peer_wq[l],
             'peer_keys': peer_keys[l], 'peer_u': peer_u[l], 'peer_v': peer_v[l],
             'ln2_g': ln2_g[l], 'ln2_b': ln2_b[l]}
        w = _layer_weights(p)
        y_s, kvn, winn, sn = _sample_layer(y_s, dec_bsz, past, cache, page_table, l, cache_win[l], state_ssm[l], w, p)
        outs[3].append(kvn)
        outs[4].append(winn)
        outs[5].append(sn)
        y_p, kvn, winn, sn = _prompt_layer(y_p, bsz, seq, w, p)
        outs[0].append(kvn)
        outs[1].append(winn)
        outs[2].append(sn)
    return (y_p.reshape(bsz, seq, D_MODEL), y_s[:dec_bsz].reshape(dec_bsz, 1, D_MODEL),
            jnp.stack(outs[0]), jnp.stack(outs[1]), jnp.stack(outs[2]),
            jnp.stack(outs[3]), jnp.stack(outs[4]), jnp.stack(outs[5]))
```

```python
import functools
import math

import jax
import jax.numpy as jnp
from jax import lax
from jax.experimental import pallas as pl
from jax.experimental.pallas import tpu as pltpu

F32 = jnp.float32
BF16 = jnp.bfloat16
I32 = jnp.int32

D_MODEL = 1024
D_SSM = 512
D_ATT = 512
SSM_CH = 16
N_GROUPS = 32
SSM_STATE = 64
N_STATE = N_GROUPS * SSM_STATE
HEAD_DIM = 64
N_HEADS = 8
N_KV_HEADS = 2
GQA = 4
KV_WIDTH = 128
CMP_STRIDE = 16
CMP_LEN = 32
SLC_BLOCK = 64
N_SLC = 16
WINDOW = 512
Q_BLOCK = 64
ROPE_THETA = 10000.0
PEER_HEADS = 8
PEER_DK = 256
N_KEYS = 128
PEER_TOPK = 16
PAGE_SIZE = 128
DN_ALPHA = 4 ** 0.25
LN_EPS = 1e-5
ATT_SCALE = HEAD_DIM ** -0.5
TINY = float(jnp.finfo(jnp.float32).tiny)
NEG_INF = float("-inf")
POS_INF = float("inf")

LANES = 128
PROJ_W = 512 + 6 * KV_WIDTH + LANES
VMEM_LIMIT = 56 * 1024 * 1024
PEER_TB = 128
PEER_SLOTS = 4
PEER_ROWS = PEER_HEADS * PEER_TOPK
PEER_FEAT_TILES = D_MODEL // LANES
PAGES_PER_STEP = 4


def _params(sem, **kw):
    return pltpu.CompilerParams(dimension_semantics=sem, vmem_limit_bytes=VMEM_LIMIT, **kw)


def _dot(a, b):
    return jnp.dot(a, b, preferred_element_type=F32)


def _dot_nt(a, b):
    return lax.dot_general(a, b, (((1,), (1,)), ((), ())), preferred_element_type=F32)


def _dot_exact(a, b):
    return lax.dot_general(a, b, (((1,), (0,)), ((), ())), precision=lax.Precision.HIGHEST,
                           preferred_element_type=F32)


def _layer_norm(x, g, b):
    mu = jnp.mean(x, axis=-1, keepdims=True)
    xc = x - mu
    var = jnp.mean(xc * xc, axis=-1, keepdims=True)
    return xc * lax.rsqrt(var + LN_EPS) * g + b


def _masked_softmax(s, valid):
    s = jnp.where(valid, s, NEG_INF)
    m = jnp.max(s, axis=-1, keepdims=True)
    m = jnp.where(m > NEG_INF, m, 0.0)
    e = jnp.where(valid, jnp.exp(s - m), 0.0)
    den = jnp.maximum(jnp.sum(e, axis=-1, keepdims=True), TINY)
    return e / den


def _mm_kernel(x_ref, w_ref, o_ref):
    o_ref[...] = _dot(x_ref[...].astype(BF16), w_ref[...])


def _matmul(x, w, tm):
    n, k = x.shape
    m = w.shape[1]
    return pl.pallas_call(
        _mm_kernel, grid=(n // tm,),
        in_specs=[pl.BlockSpec((tm, k), lambda i: (i, 0)), pl.BlockSpec((k, m), lambda i: (0, 0))],
        out_specs=pl.BlockSpec((tm, m), lambda i: (i, 0)),
        out_shape=jax.ShapeDtypeStruct((n, m), F32),
        compiler_params=_params(("parallel",)), name="mm")(x, w)


def _proj_kernel(x_ref, w_ref, cos_ref, sin_ref, q_ref, kv_ref, g_ref):
    acc = _dot(x_ref[...].astype(BF16), w_ref[...])
    cos = cos_ref[...]
    sin = sin_ref[...]
    lane = lax.broadcasted_iota(I32, cos.shape, 1)
    first_half = (lane % HEAD_DIM) < (HEAD_DIM // 2)

    def rope(v):
        rot = jnp.where(first_half, pltpu.roll(v, 96, 1), pltpu.roll(v, 32, 1))
        return v * cos + rot * sin

    for j in range(4):
        q_ref[:, j * LANES:(j + 1) * LANES] = rope(acc[:, j * LANES:(j + 1) * LANES])
    for j in range(6):
        blk = acc[:, 512 + j * LANES:512 + (j + 1) * LANES]
        kv_ref[:, j * LANES:(j + 1) * LANES] = rope(blk) if j % 2 == 0 else blk
    g_ref[...] = jax.nn.sigmoid(acc[:, 512 + 6 * LANES:])


def _project(x, w, cos, sin, tm):
    n = x.shape[0]
    tab_blocks = cos.shape[0] // tm
    return pl.pallas_call(
        _proj_kernel, grid=(n // tm,),
        in_specs=[pl.BlockSpec((tm, D_MODEL), lambda i: (i, 0)),
                  pl.BlockSpec((D_MODEL, PROJ_W), lambda i: (0, 0)),
                  pl.BlockSpec((tm, LANES), lambda i: (i % tab_blocks, 0)),
                  pl.BlockSpec((tm, LANES), lambda i: (i % tab_blocks, 0))],
        out_specs=[pl.BlockSpec((tm, 512), lambda i: (i, 0)),
                   pl.BlockSpec((tm, 6 * KV_WIDTH), lambda i: (i, 0)),
                   pl.BlockSpec((tm, LANES), lambda i: (i, 0))],
        out_shape=[jax.ShapeDtypeStruct((n, 512), F32),
                   jax.ShapeDtypeStruct((n, 6 * KV_WIDTH), F32),
                   jax.ShapeDtypeStruct((n, LANES), F32)],
        compiler_params=_params(("parallel",)), name="proj")(x, w, cos, sin)


def _glu_kernel(a_ref, w_ref, o_ref):
    gl = _dot(a_ref[...].astype(BF16), w_ref[...])
    o_ref[...] = gl[:, :D_SSM] * jax.nn.sigmoid(gl[:, D_SSM:])


def _glu(a, w, tm):
    n = a.shape[0]
    return pl.pallas_call(
        _glu_kernel, grid=(n // tm,),
        in_specs=[pl.BlockSpec((tm, D_SSM), lambda i: (i, 0)),
                  pl.BlockSpec((D_SSM, 2 * D_SSM), lambda i: (0, 0))],
        out_specs=pl.BlockSpec((tm, D_SSM), lambda i: (i, 0)),
        out_shape=jax.ShapeDtypeStruct((n, D_SSM), F32),
        compiler_params=_params(("parallel",)), name="glu")(a, w)


def _outproj_kernel(ys_ref, ya_ref, x_ref, w1_ref, w2_ref, g_ref, b_ref, o_ref):
    mix = _dot(ys_ref[...].astype(BF16), w1_ref[...]) + _dot(ya_ref[...].astype(BF16), w2_ref[...])
    o_ref[...] = _layer_norm(DN_ALPHA * x_ref[...] + mix, g_ref[...], b_ref[...])


def _outproj_ln(ys, ya, x, w1, w2, g, b, tm):
    n = x.shape[0]
    row = lambda i: (i, 0)
    fix = lambda i: (0, 0)
    return pl.pallas_call(
        _outproj_kernel, grid=(n // tm,),
        in_specs=[pl.BlockSpec((tm, D_SSM), row), pl.BlockSpec((tm, D_ATT), row),
                  pl.BlockSpec((tm, D_MODEL), row),
                  pl.BlockSpec((D_SSM, D_MODEL), fix), pl.BlockSpec((D_ATT, D_MODEL), fix),
                  pl.BlockSpec((1, D_MODEL), fix), pl.BlockSpec((1, D_MODEL), fix)],
        out_specs=pl.BlockSpec((tm, D_MODEL), row),
        out_shape=jax.ShapeDtypeStruct((n, D_MODEL), F32),
        compiler_params=_params(("parallel",)), name="outproj_ln")(ys, ya, x, w1, w2, g, b)


def _s5_kernel(u_ref, wb_ref, wc_ref, ar_ref, ai_ref, d_ref, s0_ref, gy_ref, sf_ref, bu_ref, st_ref,
               *, n_steps, batch):
    @pl.when(pl.program_id(0) == 0)
    def _():
        st_ref[...] = s0_ref[...]

    u = u_ref[...]
    bu_ref[...] = _dot(u.astype(BF16), wb_ref[...])
    n_chunk = 4
    cw = N_STATE // n_chunk

    def step(t, carry):
        r = pl.multiple_of(t * batch, batch)
        new = []
        for c in range(n_chunk):
            sr, si = carry[c], carry[n_chunk + c]
            ar = ar_ref[:, c * cw:(c + 1) * cw]
            ai = ai_ref[:, c * cw:(c + 1) * cw]
            nr = ar * sr - ai * si + bu_ref[pl.ds(r, batch), c * cw:(c + 1) * cw]
            ni = ar * si + ai * sr + bu_ref[pl.ds(r, batch), N_STATE + c * cw:N_STATE + (c + 1) * cw]
            bu_ref[pl.ds(r, batch), c * cw:(c + 1) * cw] = nr
            bu_ref[pl.ds(r, batch), N_STATE + c * cw:N_STATE + (c + 1) * cw] = ni
            new.append((nr, ni))
        return tuple(p[0] for p in new) + tuple(p[1] for p in new)

    init = tuple(st_ref[:, c * cw:(c + 1) * cw] for c in range(n_chunk)) + \
        tuple(st_ref[:, N_STATE + c * cw:N_STATE + (c + 1) * cw] for c in range(n_chunk))
    fin = lax.fori_loop(0, n_steps, step, init)
    for c in range(n_chunk):
        st_ref[:, c * cw:(c + 1) * cw] = fin[c]
        st_ref[:, N_STATE + c * cw:N_STATE + (c + 1) * cw] = fin[n_chunk + c]
    y = _dot(bu_ref[...].astype(BF16), wc_ref[...]) + d_ref[...] * u
    gy_ref[...] = jax.nn.gelu(y)
    sf_ref[...] = st_ref[...]


def _s5_scan(u_tb, wb, wc, ar, ai, d, s0, n_steps):
    batch = s0.shape[0]
    n = u_tb.shape[0]
    rows = n_steps * batch
    fix = lambda i: (0, 0)
    return pl.pallas_call(
        functools.partial(_s5_kernel, n_steps=n_steps, batch=batch),
        grid=(n // rows,),
        in_specs=[pl.BlockSpec((rows, D_SSM), lambda i: (i, 0)),
                  pl.BlockSpec((D_SSM, 2 * N_STATE), fix), pl.BlockSpec((2 * N_STATE, D_SSM), fix),
                  pl.BlockSpec((batch, N_STATE), fix), pl.BlockSpec((batch, N_STATE), fix),
                  pl.BlockSpec((1, D_SSM), fix), pl.BlockSpec((batch, 2 * N_STATE), fix)],
        out_specs=[pl.BlockSpec((rows, D_SSM), lambda i: (i, 0)),
                   pl.BlockSpec((batch, 2 * N_STATE), fix)],
        out_shape=[jax.ShapeDtypeStruct((n, D_SSM), F32),
                   jax.ShapeDtypeStruct((batch, 2 * N_STATE), F32)],
        scratch_shapes=[pltpu.VMEM((rows, 2 * N_STATE), F32), pltpu.VMEM((batch, 2 * N_STATE), F32)],
        compiler_params=_params(("arbitrary",)), name="s5_scan")(u_tb, wb, wc, ar, ai, d, s0)


def _s5_weights(a_re, a_im, log_dt, b_re, b_im, c_re, c_im, batch):
    dt = jnp.exp(log_dt)[:, None]
    mag = jnp.exp(dt * a_re)
    abar_re, abar_im = mag * jnp.cos(dt * a_im), mag * jnp.sin(dt * a_im)
    den = a_re * a_re + a_im * a_im
    f_re = ((abar_re - 1.0) * a_re + abar_im * a_im) / den
    f_im = (abar_im * a_re - (abar_re - 1.0) * a_im) / den
    bb_re = f_re[..., None] * b_re - f_im[..., None] * b_im
    bb_im = f_re[..., None] * b_im + f_im[..., None] * b_re
    eye = jnp.eye(N_GROUPS, dtype=F32)
    wb_re = jnp.einsum('gpc,gh->gchp', bb_re, eye).reshape(D_SSM, N_STATE)
    wb_im = jnp.einsum('gpc,gh->gchp', bb_im, eye).reshape(D_SSM, N_STATE)
    wb = jnp.concatenate([wb_re, wb_im], axis=1).astype(BF16)
    wc_re = jnp.einsum('gcp,gh->gphc', c_re, eye).reshape(N_STATE, D_SSM)
    wc_im = jnp.einsum('gcp,gh->gphc', c_im, eye).reshape(N_STATE, D_SSM)
    wc = jnp.concatenate([wc_re, -wc_im], axis=0).astype(BF16)
    ar = jnp.broadcast_to(abar_re.reshape(1, N_STATE), (batch, N_STATE))
    ai = jnp.broadcast_to(abar_im.reshape(1, N_STATE), (batch, N_STATE))
    return wb, wc, ar, ai


def _compress_kernel(x_ref, pe_ref, w1a_ref, w1b_ref, w2_ref, o_ref):
    x = x_ref[...]
    n = x.shape[0]
    first = _dot((x + pe_ref[0:1, :]).astype(BF16), w1a_ref[...])
    second = _dot((x + pe_ref[1:2, :]).astype(BF16), w1b_ref[...])
    nxt = pltpu.roll(second, n - 1, 0)
    o_ref[...] = _dot(jax.nn.gelu(first + nxt).astype(BF16), w2_ref[...])


def _compress(x, pe, w1a, w1b, w2):
    bsz, n, width = x.shape
    fix = lambda b: (0, 0)
    return pl.pallas_call(
        _compress_kernel, grid=(bsz,),
        in_specs=[pl.BlockSpec((None, n, width), lambda b: (b, 0, 0)),
                  pl.BlockSpec((2, width), fix), pl.BlockSpec((width, KV_WIDTH), fix),
                  pl.BlockSpec((width, KV_WIDTH), fix), pl.BlockSpec((KV_WIDTH, KV_WIDTH), fix)],
        out_specs=pl.BlockSpec((None, n, KV_WIDTH), lambda b: (b, 0, 0)),
        out_shape=jax.ShapeDtypeStruct((bsz, n, KV_WIDTH), F32),
        compiler_params=_params(("parallel",)), name="compress")(x, pe, w1a, w1b, w2)


def _compress_weights(pe, w1, w2):
    eye = jnp.eye(N_KV_HEADS, dtype=F32)
    w1 = w1.reshape(2, CMP_STRIDE, HEAD_DIM, HEAD_DIM)
    pe = pe.reshape(2, CMP_STRIDE, HEAD_DIM)
    w1a = jnp.einsum('jde,hk->jhdke', w1[0], eye).reshape(CMP_STRIDE * KV_WIDTH, KV_WIDTH).astype(BF16)
    w1b = jnp.einsum('jde,hk->jhdke', w1[1], eye).reshape(CMP_STRIDE * KV_WIDTH, KV_WIDTH).astype(BF16)
    w2d = jnp.einsum('de,hk->hdke', w2, eye).reshape(KV_WIDTH, KV_WIDTH).astype(BF16)
    pe2 = jnp.broadcast_to(pe[:, :, None, :], (2, CMP_STRIDE, N_KV_HEADS, HEAD_DIM)).reshape(2, CMP_STRIDE * KV_WIDTH)
    return pe2, w1a, w1b, w2d


def _sel_matrix(n_cmp_rows, n_blk_cols):
    n = jnp.arange(n_cmp_rows)[:, None]
    j = jnp.arange(n_blk_cols)[None, :]
    per = SLC_BLOCK // CMP_STRIDE
    m = (n // per == j).astype(F32) + ((n + 1) // per == j).astype(F32)
    return jnp.where(n < n_cmp_rows - 1, m, 0.0)


def _block_ranks(score, n_blk):
    blk = lax.broadcasted_iota(I32, score.shape, 1)
    rank = jnp.zeros(score.shape, I32)
    for i in range(n_blk):
        col = score[:, i:i + 1]
        before = (col > score) | ((col == score) & (blk > i))
        rank = rank + before.astype(I32)
    return rank


def _attn_kernel(q_ref, g_ref, kc_ref, vc_ref, ks_ref, vs_ref, kw_ref, vw_ref, msel_ref, e3_ref, eg_ref,
                 o_ref, *, seq, ck, wk):
    qb = pl.program_id(1)
    q0 = qb * Q_BLOCK
    n_cmp = seq // CMP_STRIDE
    n_blk = seq // SLC_BLOCK
    n_sel = min(N_SLC, n_blk)
    rows = GQA * Q_BLOCK
    pos = q0 + lax.broadcasted_iota(I32, (Q_BLOCK, 1), 0)
    pos4 = jnp.concatenate([pos] * GQA, axis=0)
    lane_r = lax.broadcasted_iota(I32, (rows, LANES), 1)
    kc = kc_ref[...].astype(BF16)
    vc = vc_ref[...].astype(BF16)
    n_chunks = (q0 + Q_BLOCK + ck - 1) // ck
    kstart = pl.multiple_of(jnp.maximum(q0 + Q_BLOCK - wk, 0), Q_BLOCK)
    kw = kw_ref[pl.ds(kstart, wk), :].astype(BF16)
    vw = vw_ref[pl.ds(kstart, wk), :].astype(BF16)
    q_heads, out_c, sels = [], [], []
    for hkv in range(N_KV_HEADS):
        parts = []
        for g in range(GQA):
            hd = hkv * GQA + g
            slab = q_ref[:, (hd // 2) * LANES:(hd // 2 + 1) * LANES]
            if hd % 2 != hkv:
                slab = pltpu.roll(slab, HEAD_DIM, 1)
            parts.append(slab)
        qh = jnp.concatenate(parts, axis=0)
        qh = jnp.where(lane_r // HEAD_DIM == hkv, qh * ATT_SCALE, 0.0).astype(BF16)

        s_c = _dot_nt(qh, kc)
        n_id = lax.broadcasted_iota(I32, s_c.shape, 1)
        p_c = _masked_softmax(s_c, n_id * CMP_STRIDE + (CMP_LEN - 1) <= pos4)
        o_c = _dot(p_c.astype(BF16), vc)
        p_grp = p_c[0:Q_BLOCK]
        for g in range(1, GQA):
            p_grp = p_grp + p_c[g * Q_BLOCK:(g + 1) * Q_BLOCK]
        p_slc = _dot_exact(p_grp, msel_ref[...])

        blk = lax.broadcasted_iota(I32, p_slc.shape, 1)
        cur = pos // SLC_BLOCK
        forced = (blk == 0) | (blk == cur) | (blk == cur - 1)
        future = blk * SLC_BLOCK > pos
        score = jnp.where(future, NEG_INF, jnp.where(forced, POS_INF, p_slc))
        q_heads.append(qh)
        out_c.append(o_c)
        sels.append((_block_ranks(score, n_blk) < n_sel).astype(BF16))

    def chunk(c, carry):
        base = pl.multiple_of(c * ck, ck)
        k = ks_ref[pl.ds(base, ck), :].astype(BF16)
        v = vs_ref[pl.ds(base, ck), :].astype(BF16)
        kpos = base + lax.broadcasted_iota(I32, (Q_BLOCK, ck), 1)
        causal = kpos <= pos
        new = []
        for hkv in range(N_KV_HEADS):
            m, l, acc = carry[hkv]
            s = _dot_nt(q_heads[hkv], k)
            hit = _dot(sels[hkv], e3_ref[c])
            ok = (hit > 0.5) & causal
            ok4 = jnp.concatenate([ok] * GQA, axis=0)
            s = jnp.where(ok4, s, NEG_INF)
            m_new = jnp.maximum(m, jnp.max(s, axis=-1, keepdims=True))
            m_use = jnp.where(m_new > NEG_INF, m_new, 0.0)
            alpha = jnp.exp(m - m_use)
            e = jnp.where(ok4, jnp.exp(s - m_use), 0.0)
            l = alpha * l + jnp.sum(e, axis=-1, keepdims=True)
            acc = alpha * acc + _dot(e.astype(BF16), v)
            new.append((m_new, l, acc))
        return tuple(new)

    init = (jnp.full((rows, 1), NEG_INF, F32), jnp.zeros((rows, 1), F32), jnp.zeros((rows, LANES), F32))
    sel_state = lax.fori_loop(0, n_chunks, chunk, (init, init))

    heads_c, heads_s, heads_w = [], [], []
    for hkv in range(N_KV_HEADS):
        _, l_s, acc_s = sel_state[hkv]
        o_s = acc_s / jnp.maximum(l_s, TINY)
        o_c = out_c[hkv]

        s_w = _dot_nt(q_heads[hkv], kw)
        dpos = pos4 - (kstart + lax.broadcasted_iota(I32, s_w.shape, 1))
        p_w = _masked_softmax(s_w, (dpos >= 0) & (dpos < WINDOW))
        o_w = _dot(p_w.astype(BF16), vw)

        for g in range(GQA):
            sl = (slice(g * Q_BLOCK, (g + 1) * Q_BLOCK), slice(hkv * HEAD_DIM, (hkv + 1) * HEAD_DIM))
            heads_c.append(o_c[sl])
            heads_s.append(o_s[sl])
            heads_w.append(o_w[sl])

    gates = g_ref[...]
    out = jnp.zeros((Q_BLOCK, D_ATT), F32)
    for t, heads in enumerate((heads_c, heads_s, heads_w)):
        out = out + _dot_exact(gates, eg_ref[t]) * jnp.concatenate(heads, axis=1)
    o_ref[...] = out


def _gate_expand():
    c = jnp.arange(LANES)[None, :, None]
    t = jnp.arange(3)[:, None, None]
    h = (jnp.arange(D_ATT) // HEAD_DIM)[None, None, :]
    return (c == t * N_HEADS + h).astype(F32)


def _attention(q, gates, kv, kc, vc, bsz, seq):
    ck = min(512, seq)
    wk = min(WINDOW + Q_BLOCK, seq)
    n_qb = seq // Q_BLOCK
    n_cmp = seq // CMP_STRIDE
    n_blk = seq // SLC_BLOCK
    msel = _sel_matrix(n_cmp, n_blk)
    key_blk = (jnp.arange(seq) // SLC_BLOCK).reshape(seq // ck, 1, ck)
    e3 = (key_blk == jnp.arange(n_blk)[None, :, None]).astype(BF16)
    eg = _gate_expand()
    fix2 = lambda b, i: (0, 0)
    fix3 = lambda b, i: (0, 0, 0)
    kv_spec = lambda col: pl.BlockSpec((seq, KV_WIDTH), lambda b, i: (b, col))
    return pl.pallas_call(
        functools.partial(_attn_kernel, seq=seq, ck=ck, wk=wk),
        grid=(bsz, n_qb),
        in_specs=[pl.BlockSpec((Q_BLOCK, D_ATT), lambda b, i: (b * n_qb + i, 0)),
                  pl.BlockSpec((Q_BLOCK, LANES), lambda b, i: (b * n_qb + i, 0)),
                  pl.BlockSpec((None, n_cmp, KV_WIDTH), lambda b, i: (b, 0, 0)),
                  pl.BlockSpec((None, n_cmp, KV_WIDTH), lambda b, i: (b, 0, 0)),
                  kv_spec(2), kv_spec(3), kv_spec(4), kv_spec(5),
                  pl.BlockSpec((n_cmp, n_blk), fix2),
                  pl.BlockSpec((seq // ck, n_blk, ck), fix3),
                  pl.BlockSpec((3, LANES, D_ATT), fix3)],
        out_specs=pl.BlockSpec((Q_BLOCK, D_ATT), lambda b, i: (b * n_qb + i, 0)),
        out_shape=jax.ShapeDtypeStruct((bsz * seq, D_ATT), F32),
        compiler_params=_params(("parallel", "arbitrary")), name="nsa_prompt",
    )(q, gates, kc, vc, kv, kv, kv, kv, msel, e3, eg)


def _page_gather_kernel(pt_ref, *refs):
    del pt_ref
    pages = refs[:PAGES_PER_STEP]
    xk_ref, xv_ref, ks_ref, vs_ref = refs[PAGES_PER_STEP:]
    per_page = PAGE_SIZE // CMP_STRIDE
    for i, page_ref in enumerate(pages):
        rows = slice(i * per_page, (i + 1) * per_page)
        for j in range(CMP_STRIDE):
            cols = slice(j * KV_WIDTH, (j + 1) * KV_WIDTH)
            xk_ref[rows, cols] = page_ref[pl.ds(4 * j, per_page, stride=4 * CMP_STRIDE), :]
            xv_ref[rows, cols] = page_ref[pl.ds(4 * j + 1, per_page, stride=4 * CMP_STRIDE), :]
        ks_ref[i * PAGE_SIZE:(i + 1) * PAGE_SIZE, :] = page_ref[pl.ds(2, PAGE_SIZE, stride=4), :]
        vs_ref[i * PAGE_SIZE:(i + 1) * PAGE_SIZE, :] = page_ref[pl.ds(3, PAGE_SIZE, stride=4), :]


def _page_gather(cache, page_table, layer):
    bsz, n_pages = page_table.shape
    past = n_pages * PAGE_SIZE
    per_step = PAGES_PER_STEP * (PAGE_SIZE // CMP_STRIDE)
    width = CMP_STRIDE * KV_WIDTH

    def page_spec(i):
        return pl.BlockSpec((None, None, 4 * PAGE_SIZE, KV_WIDTH),
                            lambda b, p, pt: (layer, pt[b, p * PAGES_PER_STEP + i], 0, 0))

    grid_spec = pltpu.PrefetchScalarGridSpec(
        num_scalar_prefetch=1, grid=(bsz, n_pages // PAGES_PER_STEP),
        in_specs=[page_spec(i) for i in range(PAGES_PER_STEP)],
        out_specs=[pl.BlockSpec((None, per_step, width), lambda b, p, pt: (b, p, 0)),
                   pl.BlockSpec((None, per_step, width), lambda b, p, pt: (b, p, 0)),
                   pl.BlockSpec((None, PAGES_PER_STEP * PAGE_SIZE, KV_WIDTH), lambda b, p, pt: (b, p, 0)),
                   pl.BlockSpec((None, PAGES_PER_STEP * PAGE_SIZE, KV_WIDTH), lambda b, p, pt: (b, p, 0))])
    return pl.pallas_call(
        _page_gather_kernel, grid_spec=grid_spec,
        out_shape=[jax.ShapeDtypeStruct((bsz, past // CMP_STRIDE, width), F32),
                   jax.ShapeDtypeStruct((bsz, past // CMP_STRIDE, width), F32),
                   jax.ShapeDtypeStruct((bsz, past, KV_WIDTH), F32),
                   jax.ShapeDtypeStruct((bsz, past, KV_WIDTH), F32)],
        compiler_params=_params(("parallel", "arbitrary")), name="page_gather",
    )(page_table, *([cache] * PAGES_PER_STEP))


def _dec_attn_kernel(q_ref, g_ref, nkv_ref, kc_ref, vc_ref, ks_ref, vs_ref, win_ref, msel_ref, e_ref, eg_ref,
                     o_ref, *, past, cb):
    pos = past
    n_blk = past // SLC_BLOCK + 1
    n_sel = min(N_SLC, n_blk)
    n_chunks = (past // SLC_BLOCK) // cb
    ckeys = cb * SLC_BLOCK
    win_len = win_ref.shape[0]
    row = lax.broadcasted_iota(I32, (N_HEADS, LANES), 0)
    lane = lax.broadcasted_iota(I32, (N_HEADS, LANES), 1)

    qrow = jnp.broadcast_to(q_ref[...], (N_HEADS, D_ATT))
    x = jnp.zeros((N_HEADS, LANES), F32)
    for j in range(4):
        x = jnp.where(row // 2 == j, qrow[:, j * LANES:(j + 1) * LANES], x)
    x = jnp.where((row % 2) != (row // GQA), pltpu.roll(x, HEAD_DIM, 1), x)
    q8 = jnp.where(lane // HEAD_DIM == row // GQA, x * ATT_SCALE, 0.0).astype(BF16)
    nkv = nkv_ref[...]

    def new_rows(col):
        return jnp.broadcast_to(nkv[:, col * KV_WIDTH:(col + 1) * KV_WIDTH], (N_HEADS, KV_WIDTH)).astype(BF16)

    s_c = _dot_nt(q8, kc_ref[...].astype(BF16))
    n_id = lax.broadcasted_iota(I32, s_c.shape, 1)
    p_c = _masked_softmax(s_c, n_id * CMP_STRIDE + (CMP_LEN - 1) <= pos)
    o_c = _dot(p_c.astype(BF16), vc_ref[...].astype(BF16))
    grp0 = jnp.sum(p_c[0:GQA], axis=0, keepdims=True)
    grp1 = jnp.sum(p_c[GQA:2 * GQA], axis=0, keepdims=True)
    p_grp = jnp.where(lax.broadcasted_iota(I32, p_c.shape, 0) < GQA, grp0, grp1)
    p_slc = _dot_exact(p_grp, msel_ref[...])
    blk = lax.broadcasted_iota(I32, p_slc.shape, 1)
    cur = pos // SLC_BLOCK
    forced = (blk == 0) | (blk == cur) | (blk == cur - 1)
    future = (blk * SLC_BLOCK > pos) | (blk >= n_blk)
    score = jnp.where(future, NEG_INF, jnp.where(forced, POS_INF, p_slc))
    sel = (_block_ranks(score, n_blk) < n_sel).astype(F32)

    k_new = new_rows(2)
    v_new = new_rows(3)
    ok_new = sel[:, cur:cur + 1] > 0.5
    s_new = jnp.where(ok_new, _dot_nt(q8, k_new)[:, 0:1], NEG_INF)
    scores, oks = [], []
    m = s_new
    for c in range(n_chunks):
        s = _dot_nt(q8, ks_ref[c * ckeys:(c + 1) * ckeys, :].astype(BF16))
        ok = _dot(sel[:, c * cb:(c + 1) * cb].astype(BF16), e_ref[...]) > 0.5
        s = jnp.where(ok, s, NEG_INF)
        m = jnp.maximum(m, jnp.max(s, axis=-1, keepdims=True))
        scores.append(s)
        oks.append(ok)
    m = jnp.where(m > NEG_INF, m, 0.0)
    e_new = jnp.where(ok_new, jnp.exp(s_new - m), 0.0)
    den = e_new
    acc = e_new.astype(BF16).astype(F32) * v_new.astype(F32)
    for c in range(n_chunks):
        e = jnp.where(oks[c], jnp.exp(scores[c] - m), 0.0)
        den = den + jnp.sum(e, axis=-1, keepdims=True)
        acc = acc + _dot(e.astype(BF16), vs_ref[c * ckeys:(c + 1) * ckeys, :].astype(BF16))
    o_s = acc / jnp.maximum(den, TINY)

    kw = win_ref[:, 0:KV_WIDTH].astype(BF16)
    vw = win_ref[:, KV_WIDTH:2 * KV_WIDTH].astype(BF16)
    s_w = _dot_nt(q8, kw)
    dist = win_len - lax.broadcasted_iota(I32, s_w.shape, 1)
    ok_w = (dist < WINDOW) & (pos - dist >= 0)
    s_w = jnp.where(ok_w, s_w, NEG_INF)
    s_wn = _dot_nt(q8, new_rows(4))[:, 0:1]
    m_w = jnp.maximum(jnp.max(s_w, axis=-1, keepdims=True), s_wn)
    e_w = jnp.where(ok_w, jnp.exp(s_w - m_w), 0.0)
    e_wn = jnp.exp(s_wn - m_w)
    den_w = jnp.sum(e_w, axis=-1, keepdims=True) + e_wn
    o_w = (_dot(e_w.astype(BF16), vw) + e_wn.astype(BF16).astype(F32) * new_rows(5).astype(F32)) / den_w

    gates = g_ref[...]
    out = jnp.zeros((1, D_ATT), F32)
    for t, o8 in enumerate((o_c, o_s, o_w)):
        pieces = [o8[hd:hd + 1, (hd // GQA) * HEAD_DIM:(hd // GQA + 1) * HEAD_DIM] for hd in range(N_HEADS)]
        g8 = jnp.broadcast_to(gates, (N_HEADS, LANES))
        out = out + _dot_exact(g8, eg_ref[t])[0:1] * jnp.concatenate(pieces, axis=1)
    o_ref[...] = out


def _decode_attention(q, gates, nkv, kc, vc, ks, vs, win, past):
    bsz = q.shape[0]
    n_cmp = kc.shape[1]
    n_past_blk = past // SLC_BLOCK
    cb = min(64, n_past_blk)
    n_blk_pad = -(-(n_past_blk + 1) // LANES) * LANES
    msel = _sel_matrix(n_cmp, n_blk_pad)
    e = (jnp.arange(cb * SLC_BLOCK)[None, :] // SLC_BLOCK == jnp.arange(cb)[:, None]).astype(BF16)
    eg = _gate_expand()
    win_len = win.shape[1]
    per_b = lambda b: (b, 0, 0)
    fix2 = lambda b: (0, 0)
    fix3 = lambda b: (0, 0, 0)
    once = pl.Buffered(1)
    return pl.pallas_call(
        functools.partial(_dec_attn_kernel, past=past, cb=cb),
        grid=(bsz,),
        in_specs=[pl.BlockSpec((None, 1, D_ATT), per_b), pl.BlockSpec((None, 1, LANES), per_b),
                  pl.BlockSpec((None, 1, 6 * KV_WIDTH), per_b),
                  pl.BlockSpec((None, n_cmp, KV_WIDTH), per_b), pl.BlockSpec((None, n_cmp, KV_WIDTH), per_b),
                  pl.BlockSpec((None, past, KV_WIDTH), per_b, pipeline_mode=once),
                  pl.BlockSpec((None, past, KV_WIDTH), per_b, pipeline_mode=once),
                  pl.BlockSpec((None, win_len, 2 * KV_WIDTH), per_b),
                  pl.BlockSpec((n_cmp, n_blk_pad), fix2), pl.BlockSpec((cb, cb * SLC_BLOCK), fix2),
                  pl.BlockSpec((3, LANES, D_ATT), fix3)],
        out_specs=pl.BlockSpec((None, 1, D_ATT), per_b),
        out_shape=jax.ShapeDtypeStruct((bsz, 1, D_ATT), F32),
        compiler_params=_params(("parallel",)), name="nsa_decode",
    )(q, gates, nkv, kc, vc, ks, vs, win, msel, e, eg)


def _top16(s, payload=None):
    n = s.shape[0]
    rid = lax.broadcasted_iota(I32, s.shape, 0)
    vals, picks = [], []
    for _ in range(PEER_TOPK):
        m = jnp.max(s, axis=0, keepdims=True)
        idx = jnp.min(jnp.where(s == m, rid, n), axis=0, keepdims=True)
        hit = rid == idx
        vals.append(m)
        picks.append(idx if payload is None else jnp.sum(jnp.where(hit, payload, 0), axis=0, keepdims=True))
        s = jnp.where(hit, NEG_INF, s)
    return jnp.concatenate(vals, axis=0), jnp.concatenate(picks, axis=0)


def _candidate_cells():
    return [(a, b) for a in range(PEER_TOPK) for b in range(PEER_TOPK) if (a + 1) * (b + 1) <= PEER_TOPK]


def _peer_topk_kernel(h_ref, wq_ref, keys_ref, eidx_ref, gw_ref):
    q = _dot(h_ref[...].astype(BF16), wq_ref[...])
    half = PEER_DK // 2
    cells = _candidate_cells()
    for h in range(PEER_HEADS):
        tops = []
        for i in range(2):
            col = (h * 2 + i) * half
            s_t = _dot_nt(keys_ref[h * 2 + i], q[:, col:col + half].astype(BF16))
            tops.append(_top16(s_t))
        (s1, i1), (s2, i2) = tops
        cand = jnp.concatenate([s1[a:a + 1] + s2[b:b + 1] for a, b in cells], axis=0)
        cidx = jnp.concatenate([i1[a:a + 1] * N_KEYS + i2[b:b + 1] for a, b in cells], axis=0)
        top_s, eidx = _top16(cand, cidx)
        e = jnp.exp(top_s - top_s[0:1])
        gw_ref[h * PEER_TOPK:(h + 1) * PEER_TOPK, :] = e / jnp.sum(e, axis=0, keepdims=True)
        eidx_ref[h * PEER_TOPK:(h + 1) * PEER_TOPK, :] = eidx


def _peer_topk(h, wq, keys, tm):
    n = h.shape[0]
    half = PEER_DK // 2
    return pl.pallas_call(
        _peer_topk_kernel, grid=(n // tm,),
        in_specs=[pl.BlockSpec((tm, D_MODEL), lambda i: (i, 0)),
                  pl.BlockSpec((D_MODEL, PEER_HEADS * PEER_DK), lambda i: (0, 0)),
                  pl.BlockSpec((PEER_HEADS * 2, N_KEYS, half), lambda i: (0, 0, 0))],
        out_specs=[pl.BlockSpec((PEER_ROWS, tm), lambda i: (0, i)),
                   pl.BlockSpec((PEER_ROWS, tm), lambda i: (0, i))],
        out_shape=[jax.ShapeDtypeStruct((PEER_ROWS, n), I32), jax.ShapeDtypeStruct((PEER_ROWS, n), F32)],
        compiler_params=_params(("parallel",)), name="peer_topk")(h, wq, keys)


def _pack_expert_tables(u_tab, v_tab):
    ub = lax.bitcast_convert_type(u_tab.astype(jnp.bfloat16), jnp.uint16).astype(jnp.uint32)
    vb = lax.bitcast_convert_type(v_tab.astype(jnp.bfloat16), jnp.uint16).astype(jnp.uint32)
    words = lax.bitcast_convert_type((vb << 16) | ub, I32)
    return words.reshape(words.shape[0] * PEER_FEAT_TILES, LANES)


def _peer_gather_kernel(idx_ref, gw_ref, x_ref, lng_ref, lnb_ref, uv_hbm, o_ref, idx_smem, *scratch):
    bufs = scratch[:PEER_SLOTS]
    ffn_ref, sem_idx, sem = scratch[PEER_SLOTS:]
    idx_copy = pltpu.make_async_copy(idx_ref, idx_smem, sem_idx)
    idx_copy.start()
    idx_copy.wait()

    tile = PEER_FEAT_TILES

    def issue(t, slot, lo=0, hi=PEER_ROWS):
        for e in range(lo, hi):
            row = pl.multiple_of(idx_smem[t, e], tile)
            copy = pltpu.make_async_copy(uv_hbm.at[pl.ds(row, tile)], bufs[slot].at[pl.ds(e * tile, tile)],
                                         sem.at[slot])
            copy.start(priority=e % 2)

    def wait(slot):
        pltpu.make_async_copy(uv_hbm.at[pl.ds(0, PEER_ROWS * tile)], bufs[slot], sem.at[slot]).wait()

    lane_t = lax.broadcasted_iota(I32, (PEER_ROWS, PEER_TB), 1)

    def feature_block(slot, c):
        return bufs[slot][pl.ds(c, PEER_ROWS, stride=tile), :]

    def token_step(t, slot, t_next, slot_next):
        per_part = PEER_ROWS // (2 * tile)

        def issue_part(k):
            if t_next is not None:
                issue(t_next, slot_next, k * per_part, (k + 1) * per_part)

        x_t = x_ref[pl.ds(t, 1), :]
        acc = None
        for c in range(tile):
            issue_part(c)
            u = lax.bitcast_convert_type(feature_block(slot, c) << 16, F32)
            part = u * x_t[:, c * LANES:(c + 1) * LANES]
            acc = part if acc is None else acc + part
        hidden = jnp.sum(acc, axis=1, keepdims=True)
        g_col = jnp.sum(jnp.where(lane_t == t, gw_ref[...], 0.0), axis=1, keepdims=True)
        coef = g_col * jax.nn.gelu(hidden)
        outs = []
        for c in range(tile):
            issue_part(tile + c)
            v = lax.bitcast_convert_type(feature_block(slot, c) & jnp.int32(-65536), F32)
            outs.append(jnp.sum(coef * v, axis=0, keepdims=True))
        ffn_ref[pl.ds(t, 1), :] = jnp.concatenate(outs, axis=1)

    ahead = PEER_SLOTS - 1
    for t0 in range(ahead):
        issue(t0, t0)

    def group(g, carry):
        for s in range(PEER_SLOTS):
            t = g * PEER_SLOTS + s
            wait(s)
            token_step(t, s, t + ahead, (s + ahead) % PEER_SLOTS)
        return carry

    n_groups = PEER_TB // PEER_SLOTS
    lax.fori_loop(0, n_groups - 1, group, 0)
    for s in range(PEER_SLOTS):
        t = (n_groups - 1) * PEER_SLOTS + s
        wait(s)
        token_step(t, s, t + ahead if t + ahead < PEER_TB else None, (s + ahead) % PEER_SLOTS)
    o_ref[...] = _layer_norm(DN_ALPHA * x_ref[...] + ffn_ref[...], lng_ref[...], lnb_ref[...])


def _peer_gather_ln(eidx, gw_t, x, uv_tab, g, b):
    n = x.shape[0]
    fix = lambda i: (0, 0)
    return pl.pallas_call(
        _peer_gather_kernel, grid=(n // PEER_TB,),
        in_specs=[pl.BlockSpec((PEER_TB, PEER_ROWS), lambda i: (i, 0)),
                  pl.BlockSpec((PEER_ROWS, PEER_TB), lambda i: (0, i)),
                  pl.BlockSpec((PEER_TB, D_MODEL), lambda i: (i, 0)),
                  pl.BlockSpec((1, D_MODEL), fix), pl.BlockSpec((1, D_MODEL), fix),
                  pl.BlockSpec(memory_space=pl.ANY)],
        out_specs=pl.BlockSpec((PEER_TB, D_MODEL), lambda i: (i, 0)),
        out_shape=jax.ShapeDtypeStruct((n, D_MODEL), F32),
        scratch_shapes=[pltpu.SMEM((PEER_TB, PEER_ROWS), I32)] +
                       [pltpu.VMEM((PEER_ROWS * PEER_FEAT_TILES, LANES), I32) for _ in range(PEER_SLOTS)] +
                       [pltpu.VMEM((PEER_TB, D_MODEL), F32),
                        pltpu.SemaphoreType.DMA(()),
                        pltpu.SemaphoreType.DMA((PEER_SLOTS,))],
        compiler_params=_params(("arbitrary",)), name="peer_gather",
    )(eidx, gw_t, x, g, b, uv_tab)


def _rope_tables(pos):
    half = HEAD_DIM // 2
    inv = ROPE_THETA ** (-jnp.arange(half, dtype=F32) / half)
    ang = pos.astype(F32)[:, None] * inv
    cos, sin = jnp.cos(ang), jnp.sin(ang)
    return jnp.tile(cos, (1, 4)), jnp.tile(jnp.concatenate([-sin, sin], axis=1), (1, 2))


def _layer_weights(p):
    w_in = p['w_in']
    w_u = w_in[:, :D_SSM].astype(BF16)
    pad = PROJ_W - (w_in.shape[1] - D_SSM)
    w_rest = jnp.pad(w_in[:, D_SSM:], ((0, 0), (0, pad))).astype(BF16)
    cmp_k = _compress_weights(p['cmp_pe'][0], p['cmp_w1'][0], p['cmp_w2'][0])
    cmp_v = _compress_weights(p['cmp_pe'][1], p['cmp_w1'][1], p['cmp_w2'][1])
    return dict(
        w_u=w_u, w_rest=w_rest, w_glu=p['w_glu'].astype(BF16), d=p['d'].reshape(1, D_SSM),
        wo_ssm=p['w_out'][:D_SSM].astype(BF16), wo_att=p['w_out'][D_SSM:].astype(BF16),
        ln1_g=p['ln1_g'].reshape(1, D_MODEL), ln1_b=p['ln1_b'].reshape(1, D_MODEL),
        ln2_g=p['ln2_g'].reshape(1, D_MODEL), ln2_b=p['ln2_b'].reshape(1, D_MODEL),
        peer_wq=p['peer_wq'].astype(BF16),
        peer_keys=p['peer_keys'].reshape(PEER_HEADS * 2, N_KEYS, PEER_DK // 2).astype(BF16),
        peer_uv=_pack_expert_tables(p['peer_u'], p['peer_v']), cmp_k=cmp_k, cmp_v=cmp_v)


def _token_tail(x, y_ssm, y_att, w, tm):
    h = _outproj_ln(y_ssm, y_att, x, w['wo_ssm'], w['wo_att'], w['ln1_g'], w['ln1_b'], tm)
    eidx_t, gw_t = _peer_topk(h, w['peer_wq'], w['peer_keys'], min(tm, 256))
    rows = eidx_t.T * PEER_FEAT_TILES
    return _peer_gather_ln(rows, gw_t, h, w['peer_uv'], w['ln2_g'], w['ln2_b'])


def _prompt_layer(x, bsz, seq, w, p):
    n = bsz * seq
    tm = min(512, seq)
    cos, sin = _rope_tables(jnp.arange(seq, dtype=I32))
    q, kv, gates = _project(x, w['w_rest'], cos, sin, tm)
    x_tb = x.reshape(bsz, seq, D_MODEL).transpose(1, 0, 2).reshape(n, D_MODEL)
    u_tb = _matmul(x_tb, w['w_u'], tm)
    wb, wc, ar, ai = _s5_weights(p['a_re'], p['a_im'], p['log_dt'], p['b_re'], p['b_im'], p['c_re'], p['c_im'], bsz)
    s0 = jnp.zeros((bsz, 2 * N_STATE), F32)
    gy_tb, s_fin = _s5_scan(u_tb, wb, wc, ar, ai, w['d'], s0, min(64, seq))
    y_ssm_tb = _glu(gy_tb, w['w_glu'], tm)
    y_ssm = y_ssm_tb.reshape(seq, bsz, D_SSM).transpose(1, 0, 2).reshape(n, D_SSM)
    n_cmp = seq // CMP_STRIDE
    xk = kv[:, 0:KV_WIDTH].reshape(bsz, n_cmp, CMP_STRIDE * KV_WIDTH)
    xv = kv[:, KV_WIDTH:2 * KV_WIDTH].reshape(bsz, n_cmp, CMP_STRIDE * KV_WIDTH)
    kc = _compress(xk, *w['cmp_k'])
    vc = _compress(xv, *w['cmp_v'])
    y_att = _attention(q, gates, kv, kc, vc, bsz, seq)
    y = _token_tail(x, y_ssm, y_att, w, tm)
    new_kv = kv[:, :4 * KV_WIDTH].reshape(bsz, seq, 4, N_KV_HEADS, HEAD_DIM)
    new_win = kv[:, 4 * KV_WIDTH:].reshape(bsz, seq, 2, N_KV_HEADS, HEAD_DIM)
    win = jnp.concatenate([jnp.zeros((bsz, WINDOW, 2, N_KV_HEADS, HEAD_DIM), F32), new_win], axis=1)[:, -WINDOW:]
    return y, new_kv, win, s_fin.reshape(bsz, 2, N_GROUPS, SSM_STATE)


def _sample_layer(x, bsz, past, cache, page_table, layer, win_past, ssm_state, w, p):
    rows = x.shape[0]
    cos, sin = _rope_tables(jnp.full((rows,), past, I32))
    q, kv, gates = _project(x, w['w_rest'], cos, sin, rows)
    u = _matmul(x, w['w_u'], rows)
    wb, wc, ar, ai = _s5_weights(p['a_re'], p['a_im'], p['log_dt'], p['b_re'], p['b_im'], p['c_re'], p['c_im'], bsz)
    gy, s_fin = _s5_scan(u[:bsz], wb, wc, ar, ai, w['d'], ssm_state.reshape(bsz, 2 * N_STATE), 1)
    y_ssm = jnp.pad(_glu(gy, w['w_glu'], bsz), ((0, rows - bsz), (0, 0)))
    xk, xv, ks, vs = _page_gather(cache, page_table, layer)
    kc = _compress(xk, *w['cmp_k'])
    vc = _compress(xv, *w['cmp_v'])
    win_len = win_past.shape[1]
    win_rows = win_past.reshape(bsz, win_len, 2 * KV_WIDTH)
    y_att = _decode_attention(q[:bsz].reshape(bsz, 1, D_ATT), gates[:bsz].reshape(bsz, 1, LANES),
                              kv[:bsz].reshape(bsz, 1, 6 * KV_WIDTH), kc, vc, ks, vs, win_rows, past)
    y_att = jnp.pad(y_att.reshape(bsz, D_ATT), ((0, rows - bsz), (0, 0)))
    y = _token_tail(x, y_ssm, y_att, w, rows)
    new_kv = kv[:bsz, :4 * KV_WIDTH].reshape(bsz, 1, 4, N_KV_HEADS, HEAD_DIM)
    new_win = kv[:bsz, 4 * KV_WIDTH:].reshape(bsz, 1, 2, N_KV_HEADS, HEAD_DIM)
    win = jnp.concatenate([win_past, new_win], axis=1)[:, -win_len:]
    return y, new_kv, win, s_fin.reshape(bsz, 2, N_GROUPS, SSM_STATE)


def kernel(x_prompt, x_sample, cache_kv, cache_win, state_ssm, page_table, w_in, ssm_a_re, ssm_a_im, ssm_log_dt, ssm_b_re, ssm_b_im, ssm_c_re, ssm_c_im, ssm_d, w_glu, cmp_pe, cmp_w1, cmp_w2, w_out, ln1_g, ln1_b, peer_wq, peer_keys, peer_u, peer_v, ln2_g, ln2_b):
    bsz, seq, _ = x_prompt.shape
    dec_bsz = x_sample.shape[0]
    depth = w_in.shape[0]
    past = page_table.shape[1] * PAGE_SIZE
    cache = cache_kv.reshape(depth, cache_kv.shape[1], 4 * PAGE_SIZE, KV_WIDTH)
    dec_rows = -(-dec_bsz // PEER_TB) * PEER_TB
    y_p = x_prompt.reshape(bsz * seq, D_MODEL)
    y_s = jnp.pad(x_sample.reshape(dec_bsz, D_MODEL), ((0, dec_rows - dec_bsz), (0, 0)))
    outs = [[] for _ in range(6)]
    for l in range(depth):
        p = {'w_in': w_in[l], 'a_re': ssm_a_re[l], 'a_im': ssm_a_im[l], 'log_dt': ssm_log_dt[l],
             'b_re': ssm_b_re[l], 'b_im': ssm_b_im[l], 'c_re': ssm_c_re[l], 'c_im': ssm_c_im[l],
             'd': ssm_d[l], 'w_glu': w_glu[l], 'cmp_pe': cmp_pe[l], 'cmp_w1': cmp_w1[l], 'cmp_w2': cmp_w2[l],
             'w_out': w_out[l], 'ln1_g': ln1_g[l], 'ln1_b': ln1_b[l], 'peer_wq': peer_wq[l],
             'peer_keys': peer_keys[l], 'peer_u': peer_u[l], 'peer_v': peer_v[l],
             'ln2_g': ln2_g[l], 'ln2_b': ln2_b[l]}
        w = _layer_weights(p)
        y_s, kvn, winn, sn = _sample_layer(y_s, dec_bsz, past, cache, page_table, l, cache_win[l], state_ssm[l], w, p)
        outs[3].append(kvn)
        outs[4].append(winn)
        outs[5].append(sn)
        y_p, kvn, winn, sn = _prompt_layer(y_p, bsz, seq, w, p)
        outs[0].append(kvn)
        outs[1].append(winn)
        outs[2].append(sn)
    return (y_p.reshape(bsz, seq, D_MODEL), y_s[:dec_bsz].reshape(dec_bsz, 1, D_MODEL),
            jnp.stack(outs[0]), jnp.stack(outs[1]), jnp.stack(outs[2]),
            jnp.stack(outs[3]), jnp.stack(outs[4]), jnp.stack(outs[5]))
```

```python
import functools
import math

import jax
import jax.numpy as jnp
from jax import lax
from jax.experimental import pallas as pl
from jax.experimental.pallas import tpu as pltpu

F32 = jnp.float32
BF16 = jnp.bfloat16
I32 = jnp.int32

D_MODEL = 1024
D_SSM = 512
D_ATT = 512
SSM_CH = 16
N_GROUPS = 32
SSM_STATE = 64
N_STATE = N_GROUPS * SSM_STATE
HEAD_DIM = 64
N_HEADS = 8
N_KV_HEADS = 2
GQA = 4
KV_WIDTH = 128
CMP_STRIDE = 16
CMP_LEN = 32
SLC_BLOCK = 64
N_SLC = 16
WINDOW = 512
Q_BLOCK = 64
ROPE_THETA = 10000.0
PEER_HEADS = 8
PEER_DK = 256
N_KEYS = 128
PEER_TOPK = 16
PAGE_SIZE = 128
DN_ALPHA = 4 ** 0.25
LN_EPS = 1e-5
ATT_SCALE = HEAD_DIM ** -0.5
TINY = float(jnp.finfo(jnp.float32).tiny)
NEG_INF = float("-inf")
POS_INF = float("inf")

LANES = 128
PROJ_W = 512 + 6 * KV_WIDTH + LANES
VMEM_LIMIT = 56 * 1024 * 1024
PEER_TB = 128
PEER_SLOTS = 8
PEER_ROWS = PEER_HEADS * PEER_TOPK
PEER_FEAT_TILES = D_MODEL // LANES
PAGES_PER_STEP = 4


def _params(sem, **kw):
    return pltpu.CompilerParams(dimension_semantics=sem, vmem_limit_bytes=VMEM_LIMIT, **kw)


def _dot(a, b):
    return jnp.dot(a, b, preferred_element_type=F32)


def _dot_nt(a, b):
    return lax.dot_general(a, b, (((1,), (1,)), ((), ())), preferred_element_type=F32)


def _dot_exact(a, b):
    return lax.dot_general(a, b, (((1,), (0,)), ((), ())), precision=lax.Precision.HIGHEST,
                           preferred_element_type=F32)


def _layer_norm(x, g, b):
    mu = jnp.mean(x, axis=-1, keepdims=True)
    xc = x - mu
    var = jnp.mean(xc * xc, axis=-1, keepdims=True)
    return xc * lax.rsqrt(var + LN_EPS) * g + b


def _masked_softmax(s, valid):
    s = jnp.where(valid, s, NEG_INF)
    m = jnp.max(s, axis=-1, keepdims=True)
    m = jnp.where(m > NEG_INF, m, 0.0)
    e = jnp.where(valid, jnp.exp(s - m), 0.0)
    den = jnp.maximum(jnp.sum(e, axis=-1, keepdims=True), TINY)
    return e / den


def _mm_kernel(x_ref, w_ref, o_ref):
    o_ref[...] = _dot(x_ref[...].astype(BF16), w_ref[...])


def _matmul(x, w, tm):
    n, k = x.shape
    m = w.shape[1]
    return pl.pallas_call(
        _mm_kernel, grid=(n // tm,),
        in_specs=[pl.BlockSpec((tm, k), lambda i: (i, 0)), pl.BlockSpec((k, m), lambda i: (0, 0))],
        out_specs=pl.BlockSpec((tm, m), lambda i: (i, 0)),
        out_shape=jax.ShapeDtypeStruct((n, m), F32),
        compiler_params=_params(("parallel",)), name="mm")(x, w)


def _proj_kernel(x_ref, w_ref, cos_ref, sin_ref, q_ref, kv_ref, g_ref):
    acc = _dot(x_ref[...].astype(BF16), w_ref[...])
    cos = cos_ref[...]
    sin = sin_ref[...]
    lane = lax.broadcasted_iota(I32, cos.shape, 1)
    first_half = (lane % HEAD_DIM) < (HEAD_DIM // 2)

    def rope(v):
        rot = jnp.where(first_half, pltpu.roll(v, 96, 1), pltpu.roll(v, 32, 1))
        return v * cos + rot * sin

    for j in range(4):
        q_ref[:, j * LANES:(j + 1) * LANES] = rope(acc[:, j * LANES:(j + 1) * LANES])
    for j in range(6):
        blk = acc[:, 512 + j * LANES:512 + (j + 1) * LANES]
        kv_ref[:, j * LANES:(j + 1) * LANES] = rope(blk) if j % 2 == 0 else blk
    g_ref[...] = jax.nn.sigmoid(acc[:, 512 + 6 * LANES:])


def _project(x, w, cos, sin, tm):
    n = x.shape[0]
    tab_blocks = cos.shape[0] // tm
    return pl.pallas_call(
        _proj_kernel, grid=(n // tm,),
        in_specs=[pl.BlockSpec((tm, D_MODEL), lambda i: (i, 0)),
                  pl.BlockSpec((D_MODEL, PROJ_W), lambda i: (0, 0)),
                  pl.BlockSpec((tm, LANES), lambda i: (i % tab_blocks, 0)),
                  pl.BlockSpec((tm, LANES), lambda i: (i % tab_blocks, 0))],
        out_specs=[pl.BlockSpec((tm, 512), lambda i: (i, 0)),
                   pl.BlockSpec((tm, 6 * KV_WIDTH), lambda i: (i, 0)),
                   pl.BlockSpec((tm, LANES), lambda i: (i, 0))],
        out_shape=[jax.ShapeDtypeStruct((n, 512), F32),
                   jax.ShapeDtypeStruct((n, 6 * KV_WIDTH), F32),
                   jax.ShapeDtypeStruct((n, LANES), F32)],
        compiler_params=_params(("parallel",)), name="proj")(x, w, cos, sin)


def _glu_kernel(a_ref, w_ref, o_ref):
    gl = _dot(a_ref[...].astype(BF16), w_ref[...])
    o_ref[...] = gl[:, :D_SSM] * jax.nn.sigmoid(gl[:, D_SSM:])


def _glu(a, w, tm):
    n = a.shape[0]
    return pl.pallas_call(
        _glu_kernel, grid=(n // tm,),
        in_specs=[pl.BlockSpec((tm, D_SSM), lambda i: (i, 0)),
                  pl.BlockSpec((D_SSM, 2 * D_SSM), lambda i: (0, 0))],
        out_specs=pl.BlockSpec((tm, D_SSM), lambda i: (i, 0)),
        out_shape=jax.ShapeDtypeStruct((n, D_SSM), F32),
        compiler_params=_params(("parallel",)), name="glu")(a, w)


def _outproj_kernel(ys_ref, ya_ref, x_ref, w1_ref, w2_ref, g_ref, b_ref, o_ref):
    mix = _dot(ys_ref[...].astype(BF16), w1_ref[...]) + _dot(ya_ref[...].astype(BF16), w2_ref[...])
    o_ref[...] = _layer_norm(DN_ALPHA * x_ref[...] + mix, g_ref[...], b_ref[...])


def _outproj_ln(ys, ya, x, w1, w2, g, b, tm):
    n = x.shape[0]
    row = lambda i: (i, 0)
    fix = lambda i: (0, 0)
    return pl.pallas_call(
        _outproj_kernel, grid=(n // tm,),
        in_specs=[pl.BlockSpec((tm, D_SSM), row), pl.BlockSpec((tm, D_ATT), row),
                  pl.BlockSpec((tm, D_MODEL), row),
                  pl.BlockSpec((D_SSM, D_MODEL), fix), pl.BlockSpec((D_ATT, D_MODEL), fix),
                  pl.BlockSpec((1, D_MODEL), fix), pl.BlockSpec((1, D_MODEL), fix)],
        out_specs=pl.BlockSpec((tm, D_MODEL), row),
        out_shape=jax.ShapeDtypeStruct((n, D_MODEL), F32),
        compiler_params=_params(("parallel",)), name="outproj_ln")(ys, ya, x, w1, w2, g, b)


def _s5_kernel(u_ref, wb_ref, wc_ref, ar_ref, ai_ref, d_ref, s0_ref, gy_ref, sf_ref, bu_ref, st_ref,
               *, n_steps, batch):
    @pl.when(pl.program_id(0) == 0)
    def _():
        st_ref[...] = s0_ref[...]

    u = u_ref[...]
    bu_ref[...] = _dot(u.astype(BF16), wb_ref[...])
    n_chunk = 4
    cw = N_STATE // n_chunk

    def step(t, carry):
        r = pl.multiple_of(t * batch, batch)
        new = []
        for c in range(n_chunk):
            sr, si = carry[c], carry[n_chunk + c]
            ar = ar_ref[:, c * cw:(c + 1) * cw]
            ai = ai_ref[:, c * cw:(c + 1) * cw]
            nr = ar * sr - ai * si + bu_ref[pl.ds(r, batch), c * cw:(c + 1) * cw]
            ni = ar * si + ai * sr + bu_ref[pl.ds(r, batch), N_STATE + c * cw:N_STATE + (c + 1) * cw]
            bu_ref[pl.ds(r, batch), c * cw:(c + 1) * cw] = nr
            bu_ref[pl.ds(r, batch), N_STATE + c * cw:N_STATE + (c + 1) * cw] = ni
            new.append((nr, ni))
        return tuple(p[0] for p in new) + tuple(p[1] for p in new)

    init = tuple(st_ref[:, c * cw:(c + 1) * cw] for c in range(n_chunk)) + \
        tuple(st_ref[:, N_STATE + c * cw:N_STATE + (c + 1) * cw] for c in range(n_chunk))
    fin = lax.fori_loop(0, n_steps, step, init)
    for c in range(n_chunk):
        st_ref[:, c * cw:(c + 1) * cw] = fin[c]
        st_ref[:, N_STATE + c * cw:N_STATE + (c + 1) * cw] = fin[n_chunk + c]
    y = _dot(bu_ref[...].astype(BF16), wc_ref[...]) + d_ref[...] * u
    gy_ref[...] = jax.nn.gelu(y)
    sf_ref[...] = st_ref[...]


def _s5_scan(u_tb, wb, wc, ar, ai, d, s0, n_steps):
    batch = s0.shape[0]
    n = u_tb.shape[0]
    rows = n_steps * batch
    fix = lambda i: (0, 0)
    return pl.pallas_call(
        functools.partial(_s5_kernel, n_steps=n_steps, batch=batch),
        grid=(n // rows,),
        in_specs=[pl.BlockSpec((rows, D_SSM), lambda i: (i, 0)),
                  pl.BlockSpec((D_SSM, 2 * N_STATE), fix), pl.BlockSpec((2 * N_STATE, D_SSM), fix),
                  pl.BlockSpec((batch, N_STATE), fix), pl.BlockSpec((batch, N_STATE), fix),
                  pl.BlockSpec((1, D_SSM), fix), pl.BlockSpec((batch, 2 * N_STATE), fix)],
        out_specs=[pl.BlockSpec((rows, D_SSM), lambda i: (i, 0)),
                   pl.BlockSpec((batch, 2 * N_STATE), fix)],
        out_shape=[jax.ShapeDtypeStruct((n, D_SSM), F32),
                   jax.ShapeDtypeStruct((batch, 2 * N_STATE), F32)],
        scratch_shapes=[pltpu.VMEM((rows, 2 * N_STATE), F32), pltpu.VMEM((batch, 2 * N_STATE), F32)],
        compiler_params=_params(("arbitrary",)), name="s5_scan")(u_tb, wb, wc, ar, ai, d, s0)


def _s5_weights(a_re, a_im, log_dt, b_re, b_im, c_re, c_im, batch):
    dt = jnp.exp(log_dt)[:, None]
    mag = jnp.exp(dt * a_re)
    abar_re, abar_im = mag * jnp.cos(dt * a_im), mag * jnp.sin(dt * a_im)
    den = a_re * a_re + a_im * a_im
    f_re = ((abar_re - 1.0) * a_re + abar_im * a_im) / den
    f_im = (abar_im * a_re - (abar_re - 1.0) * a_im) / den
    bb_re = f_re[..., None] * b_re - f_im[..., None] * b_im
    bb_im = f_re[..., None] * b_im + f_im[..., None] * b_re
    eye = jnp.eye(N_GROUPS, dtype=F32)
    wb_re = jnp.einsum('gpc,gh->gchp', bb_re, eye).reshape(D_SSM, N_STATE)
    wb_im = jnp.einsum('gpc,gh->gchp', bb_im, eye).reshape(D_SSM, N_STATE)
    wb = jnp.concatenate([wb_re, wb_im], axis=1).astype(BF16)
    wc_re = jnp.einsum('gcp,gh->gphc', c_re, eye).reshape(N_STATE, D_SSM)
    wc_im = jnp.einsum('gcp,gh->gphc', c_im, eye).reshape(N_STATE, D_SSM)
    wc = jnp.concatenate([wc_re, -wc_im], axis=0).astype(BF16)
    ar = jnp.broadcast_to(abar_re.reshape(1, N_STATE), (batch, N_STATE))
    ai = jnp.broadcast_to(abar_im.reshape(1, N_STATE), (batch, N_STATE))
    return wb, wc, ar, ai


def _compress_kernel(x_ref, pe_ref, w1a_ref, w1b_ref, w2_ref, o_ref):
    x = x_ref[...]
    n = x.shape[0]
    first = _dot((x + pe_ref[0:1, :]).astype(BF16), w1a_ref[...])
    second = _dot((x + pe_ref[1:2, :]).astype(BF16), w1b_ref[...])
    nxt = pltpu.roll(second, n - 1, 0)
    o_ref[...] = _dot(jax.nn.gelu(first + nxt).astype(BF16), w2_ref[...])


def _compress(x, pe, w1a, w1b, w2):
    bsz, n, width = x.shape
    fix = lambda b: (0, 0)
    return pl.pallas_call(
        _compress_kernel, grid=(bsz,),
        in_specs=[pl.BlockSpec((None, n, width), lambda b: (b, 0, 0)),
                  pl.BlockSpec((2, width), fix), pl.BlockSpec((width, KV_WIDTH), fix),
                  pl.BlockSpec((width, KV_WIDTH), fix), pl.BlockSpec((KV_WIDTH, KV_WIDTH), fix)],
        out_specs=pl.BlockSpec((None, n, KV_WIDTH), lambda b: (b, 0, 0)),
        out_shape=jax.ShapeDtypeStruct((bsz, n, KV_WIDTH), F32),
        compiler_params=_params(("parallel",)), name="compress")(x, pe, w1a, w1b, w2)


def _compress_weights(pe, w1, w2):
    eye = jnp.eye(N_KV_HEADS, dtype=F32)
    w1 = w1.reshape(2, CMP_STRIDE, HEAD_DIM, HEAD_DIM)
    pe = pe.reshape(2, CMP_STRIDE, HEAD_DIM)
    w1a = jnp.einsum('jde,hk->jhdke', w1[0], eye).reshape(CMP_STRIDE * KV_WIDTH, KV_WIDTH).astype(BF16)
    w1b = jnp.einsum('jde,hk->jhdke', w1[1], eye).reshape(CMP_STRIDE * KV_WIDTH, KV_WIDTH).astype(BF16)
    w2d = jnp.einsum('de,hk->hdke', w2, eye).reshape(KV_WIDTH, KV_WIDTH).astype(BF16)
    pe2 = jnp.broadcast_to(pe[:, :, None, :], (2, CMP_STRIDE, N_KV_HEADS, HEAD_DIM)).reshape(2, CMP_STRIDE * KV_WIDTH)
    return pe2, w1a, w1b, w2d


def _sel_matrix(n_cmp_rows, n_blk_cols):
    n = jnp.arange(n_cmp_rows)[:, None]
    j = jnp.arange(n_blk_cols)[None, :]
    per = SLC_BLOCK // CMP_STRIDE
    m = (n // per == j).astype(F32) + ((n + 1) // per == j).astype(F32)
    return jnp.where(n < n_cmp_rows - 1, m, 0.0)


def _block_ranks(score, n_blk):
    blk = lax.broadcasted_iota(I32, score.shape, 1)
    rank = jnp.zeros(score.shape, I32)
    for i in range(n_blk):
        col = score[:, i:i + 1]
        before = (col > score) | ((col == score) & (blk > i))
        rank = rank + before.astype(I32)
    return rank


def _attn_kernel(q_ref, g_ref, kc_ref, vc_ref, ks_ref, vs_ref, kw_ref, vw_ref, msel_ref, e3_ref, eg_ref,
                 o_ref, *, seq, ck, wk):
    qb = pl.program_id(1)
    q0 = qb * Q_BLOCK
    n_cmp = seq // CMP_STRIDE
    n_blk = seq // SLC_BLOCK
    n_sel = min(N_SLC, n_blk)
    rows = GQA * Q_BLOCK
    pos = q0 + lax.broadcasted_iota(I32, (Q_BLOCK, 1), 0)
    pos4 = jnp.concatenate([pos] * GQA, axis=0)
    lane_r = lax.broadcasted_iota(I32, (rows, LANES), 1)
    kc = kc_ref[...].astype(BF16)
    vc = vc_ref[...].astype(BF16)
    n_chunks = (q0 + Q_BLOCK + ck - 1) // ck
    kstart = pl.multiple_of(jnp.maximum(q0 + Q_BLOCK - wk, 0), Q_BLOCK)
    kw = kw_ref[pl.ds(kstart, wk), :].astype(BF16)
    vw = vw_ref[pl.ds(kstart, wk), :].astype(BF16)
    q_heads, out_c, sels = [], [], []
    for hkv in range(N_KV_HEADS):
        parts = []
        for g in range(GQA):
            hd = hkv * GQA + g
            slab = q_ref[:, (hd // 2) * LANES:(hd // 2 + 1) * LANES]
            if hd % 2 != hkv:
                slab = pltpu.roll(slab, HEAD_DIM, 1)
            parts.append(slab)
        qh = jnp.concatenate(parts, axis=0)
        qh = jnp.where(lane_r // HEAD_DIM == hkv, qh * ATT_SCALE, 0.0).astype(BF16)

        s_c = _dot_nt(qh, kc)
        n_id = lax.broadcasted_iota(I32, s_c.shape, 1)
        p_c = _masked_softmax(s_c, n_id * CMP_STRIDE + (CMP_LEN - 1) <= pos4)
        o_c = _dot(p_c.astype(BF16), vc)
        p_grp = p_c[0:Q_BLOCK]
        for g in range(1, GQA):
            p_grp = p_grp + p_c[g * Q_BLOCK:(g + 1) * Q_BLOCK]
        p_slc = _dot_exact(p_grp, msel_ref[...])

        blk = lax.broadcasted_iota(I32, p_slc.shape, 1)
        cur = pos // SLC_BLOCK
        forced = (blk == 0) | (blk == cur) | (blk == cur - 1)
        future = blk * SLC_BLOCK > pos
        score = jnp.where(future, NEG_INF, jnp.where(forced, POS_INF, p_slc))
        q_heads.append(qh)
        out_c.append(o_c)
        sels.append((_block_ranks(score, n_blk) < n_sel).astype(BF16))

    def chunk(c, carry):
        base = pl.multiple_of(c * ck, ck)
        k = ks_ref[pl.ds(base, ck), :].astype(BF16)
        v = vs_ref[pl.ds(base, ck), :].astype(BF16)
        kpos = base + lax.broadcasted_iota(I32, (Q_BLOCK, ck), 1)
        causal = kpos <= pos
        new = []
        for hkv in range(N_KV_HEADS):
            m, l, acc = carry[hkv]
            s = _dot_nt(q_heads[hkv], k)
            hit = _dot(sels[hkv], e3_ref[c])
            ok = (hit > 0.5) & causal
            ok4 = jnp.concatenate([ok] * GQA, axis=0)
            s = jnp.where(ok4, s, NEG_INF)
            m_new = jnp.maximum(m, jnp.max(s, axis=-1, keepdims=True))
            m_use = jnp.where(m_new > NEG_INF, m_new, 0.0)
            alpha = jnp.exp(m - m_use)
            e = jnp.where(ok4, jnp.exp(s - m_use), 0.0)
            l = alpha * l + jnp.sum(e, axis=-1, keepdims=True)
            acc = alpha * acc + _dot(e.astype(BF16), v)
            new.append((m_new, l, acc))
        return tuple(new)

    init = (jnp.full((rows, 1), NEG_INF, F32), jnp.zeros((rows, 1), F32), jnp.zeros((rows, LANES), F32))
    sel_state = lax.fori_loop(0, n_chunks, chunk, (init, init))

    heads_c, heads_s, heads_w = [], [], []
    for hkv in range(N_KV_HEADS):
        _, l_s, acc_s = sel_state[hkv]
        o_s = acc_s / jnp.maximum(l_s, TINY)
        o_c = out_c[hkv]

        s_w = _dot_nt(q_heads[hkv], kw)
        dpos = pos4 - (kstart + lax.broadcasted_iota(I32, s_w.shape, 1))
        p_w = _masked_softmax(s_w, (dpos >= 0) & (dpos < WINDOW))
        o_w = _dot(p_w.astype(BF16), vw)

        for g in range(GQA):
            sl = (slice(g * Q_BLOCK, (g + 1) * Q_BLOCK), slice(hkv * HEAD_DIM, (hkv + 1) * HEAD_DIM))
            heads_c.append(o_c[sl])
            heads_s.append(o_s[sl])
            heads_w.append(o_w[sl])

    gates = g_ref[...]
    out = jnp.zeros((Q_BLOCK, D_ATT), F32)
    for t, heads in enumerate((heads_c, heads_s, heads_w)):
        out = out + _dot_exact(gates, eg_ref[t]) * jnp.concatenate(heads, axis=1)
    o_ref[...] = out


def _gate_expand():
    c = jnp.arange(LANES)[None, :, None]
    t = jnp.arange(3)[:, None, None]
    h = (jnp.arange(D_ATT) // HEAD_DIM)[None, None, :]
    return (c == t * N_HEADS + h).astype(F32)


def _attention(q, gates, kv, kc, vc, bsz, seq):
    ck = min(512, seq)
    wk = min(WINDOW + Q_BLOCK, seq)
    n_qb = seq // Q_BLOCK
    n_cmp = seq // CMP_STRIDE
    n_blk = seq // SLC_BLOCK
    msel = _sel_matrix(n_cmp, n_blk)
    key_blk = (jnp.arange(seq) // SLC_BLOCK).reshape(seq // ck, 1, ck)
    e3 = (key_blk == jnp.arange(n_blk)[None, :, None]).astype(BF16)
    eg = _gate_expand()
    fix2 = lambda b, i: (0, 0)
    fix3 = lambda b, i: (0, 0, 0)
    kv_spec = lambda col: pl.BlockSpec((seq, KV_WIDTH), lambda b, i: (b, col))
    return pl.pallas_call(
        functools.partial(_attn_kernel, seq=seq, ck=ck, wk=wk),
        grid=(bsz, n_qb),
        in_specs=[pl.BlockSpec((Q_BLOCK, D_ATT), lambda b, i: (b * n_qb + i, 0)),
                  pl.BlockSpec((Q_BLOCK, LANES), lambda b, i: (b * n_qb + i, 0)),
                  pl.BlockSpec((None, n_cmp, KV_WIDTH), lambda b, i: (b, 0, 0)),
                  pl.BlockSpec((None, n_cmp, KV_WIDTH), lambda b, i: (b, 0, 0)),
                  kv_spec(2), kv_spec(3), kv_spec(4), kv_spec(5),
                  pl.BlockSpec((n_cmp, n_blk), fix2),
                  pl.BlockSpec((seq // ck, n_blk, ck), fix3),
                  pl.BlockSpec((3, LANES, D_ATT), fix3)],
        out_specs=pl.BlockSpec((Q_BLOCK, D_ATT), lambda b, i: (b * n_qb + i, 0)),
        out_shape=jax.ShapeDtypeStruct((bsz * seq, D_ATT), F32),
        compiler_params=_params(("parallel", "arbitrary")), name="nsa_prompt",
    )(q, gates, kc, vc, kv, kv, kv, kv, msel, e3, eg)


def _page_gather_kernel(pt_ref, *refs):
    del pt_ref
    pages = refs[:PAGES_PER_STEP]
    xk_ref, xv_ref, ks_ref, vs_ref, kc_tmp, vc_tmp = refs[PAGES_PER_STEP:]
    per_page = PAGE_SIZE // CMP_STRIDE
    for i, page_ref in enumerate(pages):

        def both_heads(kind):
            return jnp.concatenate([page_ref[kind, h] for h in range(N_KV_HEADS)], axis=0).T

        kc_tmp[...] = both_heads(0)
        vc_tmp[...] = both_heads(1)
        rows = slice(i * per_page, (i + 1) * per_page)
        for j in range(CMP_STRIDE):
            cols = slice(j * KV_WIDTH, (j + 1) * KV_WIDTH)
            xk_ref[rows, cols] = kc_tmp[pl.ds(j, per_page, stride=CMP_STRIDE), :]
            xv_ref[rows, cols] = vc_tmp[pl.ds(j, per_page, stride=CMP_STRIDE), :]
        ks_ref[i * PAGE_SIZE:(i + 1) * PAGE_SIZE, :] = both_heads(2)
        vs_ref[i * PAGE_SIZE:(i + 1) * PAGE_SIZE, :] = both_heads(3)


def _page_gather(cache, page_table, layer):
    bsz, n_pages = page_table.shape
    past = n_pages * PAGE_SIZE
    per_step = PAGES_PER_STEP * (PAGE_SIZE // CMP_STRIDE)
    width = CMP_STRIDE * KV_WIDTH

    def page_spec(i):
        return pl.BlockSpec((None, None, 4, N_KV_HEADS, HEAD_DIM, PAGE_SIZE),
                            lambda b, p, pt: (layer, pt[b, p * PAGES_PER_STEP + i], 0, 0, 0, 0))

    grid_spec = pltpu.PrefetchScalarGridSpec(
        num_scalar_prefetch=1, grid=(bsz, n_pages // PAGES_PER_STEP),
        in_specs=[page_spec(i) for i in range(PAGES_PER_STEP)],
        scratch_shapes=[pltpu.VMEM((PAGE_SIZE, KV_WIDTH), F32), pltpu.VMEM((PAGE_SIZE, KV_WIDTH), F32)],
        out_specs=[pl.BlockSpec((None, per_step, width), lambda b, p, pt: (b, p, 0)),
                   pl.BlockSpec((None, per_step, width), lambda b, p, pt: (b, p, 0)),
                   pl.BlockSpec((None, PAGES_PER_STEP * PAGE_SIZE, KV_WIDTH), lambda b, p, pt: (b, p, 0)),
                   pl.BlockSpec((None, PAGES_PER_STEP * PAGE_SIZE, KV_WIDTH), lambda b, p, pt: (b, p, 0))])
    return pl.pallas_call(
        _page_gather_kernel, grid_spec=grid_spec,
        out_shape=[jax.ShapeDtypeStruct((bsz, past // CMP_STRIDE, width), F32),
                   jax.ShapeDtypeStruct((bsz, past // CMP_STRIDE, width), F32),
                   jax.ShapeDtypeStruct((bsz, past, KV_WIDTH), F32),
                   jax.ShapeDtypeStruct((bsz, past, KV_WIDTH), F32)],
        compiler_params=_params(("parallel", "arbitrary")), name="page_gather",
    )(page_table, *([cache] * PAGES_PER_STEP))


def _dec_attn_kernel(q_ref, g_ref, nkv_ref, kc_ref, vc_ref, ks_ref, vs_ref, win_ref, msel_ref, e_ref, eg_ref,
                     o_ref, *, past, cb):
    pos = past
    n_blk = past // SLC_BLOCK + 1
    n_sel = min(N_SLC, n_blk)
    n_chunks = (past // SLC_BLOCK) // cb
    ckeys = cb * SLC_BLOCK
    win_len = win_ref.shape[0]
    row = lax.broadcasted_iota(I32, (N_HEADS, LANES), 0)
    lane = lax.broadcasted_iota(I32, (N_HEADS, LANES), 1)

    qrow = jnp.broadcast_to(q_ref[...], (N_HEADS, D_ATT))
    x = jnp.zeros((N_HEADS, LANES), F32)
    for j in range(4):
        x = jnp.where(row // 2 == j, qrow[:, j * LANES:(j + 1) * LANES], x)
    x = jnp.where((row % 2) != (row // GQA), pltpu.roll(x, HEAD_DIM, 1), x)
    q8 = jnp.where(lane // HEAD_DIM == row // GQA, x * ATT_SCALE, 0.0).astype(BF16)
    nkv = nkv_ref[...]

    def new_rows(col):
        return jnp.broadcast_to(nkv[:, col * KV_WIDTH:(col + 1) * KV_WIDTH], (N_HEADS, KV_WIDTH)).astype(BF16)

    s_c = _dot_nt(q8, kc_ref[...].astype(BF16))
    n_id = lax.broadcasted_iota(I32, s_c.shape, 1)
    p_c = _masked_softmax(s_c, n_id * CMP_STRIDE + (CMP_LEN - 1) <= pos)
    o_c = _dot(p_c.astype(BF16), vc_ref[...].astype(BF16))
    grp0 = jnp.sum(p_c[0:GQA], axis=0, keepdims=True)
    grp1 = jnp.sum(p_c[GQA:2 * GQA], axis=0, keepdims=True)
    p_grp = jnp.where(lax.broadcasted_iota(I32, p_c.shape, 0) < GQA, grp0, grp1)
    p_slc = _dot_exact(p_grp, msel_ref[...])
    blk = lax.broadcasted_iota(I32, p_slc.shape, 1)
    cur = pos // SLC_BLOCK
    forced = (blk == 0) | (blk == cur) | (blk == cur - 1)
    future = (blk * SLC_BLOCK > pos) | (blk >= n_blk)
    score = jnp.where(future, NEG_INF, jnp.where(forced, POS_INF, p_slc))
    sel = (_block_ranks(score, n_blk) < n_sel).astype(F32)

    k_new = new_rows(2)
    v_new = new_rows(3)
    ok_new = sel[:, cur:cur + 1] > 0.5
    s_new = jnp.where(ok_new, _dot_nt(q8, k_new)[:, 0:1], NEG_INF)
    scores, oks = [], []
    m = s_new
    for c in range(n_chunks):
        s = _dot_nt(q8, ks_ref[c * ckeys:(c + 1) * ckeys, :].astype(BF16))
        ok = _dot(sel[:, c * cb:(c + 1) * cb].astype(BF16), e_ref[...]) > 0.5
        s = jnp.where(ok, s, NEG_INF)
        m = jnp.maximum(m, jnp.max(s, axis=-1, keepdims=True))
        scores.append(s)
        oks.append(ok)
    m = jnp.where(m > NEG_INF, m, 0.0)
    e_new = jnp.where(ok_new, jnp.exp(s_new - m), 0.0)
    den = e_new
    acc = e_new.astype(BF16).astype(F32) * v_new.astype(F32)
    for c in range(n_chunks):
        e = jnp.where(oks[c], jnp.exp(scores[c] - m), 0.0)
        den = den + jnp.sum(e, axis=-1, keepdims=True)
        acc = acc + _dot(e.astype(BF16), vs_ref[c * ckeys:(c + 1) * ckeys, :].astype(BF16))
    o_s = acc / jnp.maximum(den, TINY)

    kw = win_ref[:, 0:KV_WIDTH].astype(BF16)
    vw = win_ref[:, KV_WIDTH:2 * KV_WIDTH].astype(BF16)
    s_w = _dot_nt(q8, kw)
    dist = win_len - lax.broadcasted_iota(I32, s_w.shape, 1)
    ok_w = (dist < WINDOW) & (pos - dist >= 0)
    s_w = jnp.where(ok_w, s_w, NEG_INF)
    s_wn = _dot_nt(q8, new_rows(4))[:, 0:1]
    m_w = jnp.maximum(jnp.max(s_w, axis=-1, keepdims=True), s_wn)
    e_w = jnp.where(ok_w, jnp.exp(s_w - m_w), 0.0)
    e_wn = jnp.exp(s_wn - m_w)
    den_w = jnp.sum(e_w, axis=-1, keepdims=True) + e_wn
    o_w = (_dot(e_w.astype(BF16), vw) + e_wn.astype(BF16).astype(F32) * new_rows(5).astype(F32)) / den_w

    gates = g_ref[...]
    out = jnp.zeros((1, D_ATT), F32)
    for t, o8 in enumerate((o_c, o_s, o_w)):
        pieces = [o8[hd:hd + 1, (hd // GQA) * HEAD_DIM:(hd // GQA + 1) * HEAD_DIM] for hd in range(N_HEADS)]
        g8 = jnp.broadcast_to(gates, (N_HEADS, LANES))
        out = out + _dot_exact(g8, eg_ref[t])[0:1] * jnp.concatenate(pieces, axis=1)
    o_ref[...] = out


def _decode_attention(q, gates, nkv, kc, vc, ks, vs, win, past):
    bsz = q.shape[0]
    n_cmp = kc.shape[1]
    n_past_blk = past // SLC_BLOCK
    cb = min(64, n_past_blk)
    n_blk_pad = -(-(n_past_blk + 1) // LANES) * LANES
    msel = _sel_matrix(n_cmp, n_blk_pad)
    e = (jnp.arange(cb * SLC_BLOCK)[None, :] // SLC_BLOCK == jnp.arange(cb)[:, None]).astype(BF16)
    eg = _gate_expand()
    win_len = win.shape[1]
    per_b = lambda b: (b, 0, 0)
    fix2 = lambda b: (0, 0)
    fix3 = lambda b: (0, 0, 0)
    once = pl.Buffered(1)
    return pl.pallas_call(
        functools.partial(_dec_attn_kernel, past=past, cb=cb),
        grid=(bsz,),
        in_specs=[pl.BlockSpec((None, 1, D_ATT), per_b), pl.BlockSpec((None, 1, LANES), per_b),
                  pl.BlockSpec((None, 1, 6 * KV_WIDTH), per_b),
                  pl.BlockSpec((None, n_cmp, KV_WIDTH), per_b), pl.BlockSpec((None, n_cmp, KV_WIDTH), per_b),
                  pl.BlockSpec((None, past, KV_WIDTH), per_b, pipeline_mode=once),
                  pl.BlockSpec((None, past, KV_WIDTH), per_b, pipeline_mode=once),
                  pl.BlockSpec((None, win_len, 2 * KV_WIDTH), per_b),
                  pl.BlockSpec((n_cmp, n_blk_pad), fix2), pl.BlockSpec((cb, cb * SLC_BLOCK), fix2),
                  pl.BlockSpec((3, LANES, D_ATT), fix3)],
        out_specs=pl.BlockSpec((None, 1, D_ATT), per_b),
        out_shape=jax.ShapeDtypeStruct((bsz, 1, D_ATT), F32),
        compiler_params=_params(("parallel",)), name="nsa_decode",
    )(q, gates, nkv, kc, vc, ks, vs, win, msel, e, eg)


def _top16(s, payload=None):
    n = s.shape[0]
    rid = lax.broadcasted_iota(I32, s.shape, 0).astype(F32)
    vals, picks = [], []
    for _ in range(PEER_TOPK):
        m = jnp.max(s, axis=0, keepdims=True)
        idx = jnp.min(jnp.where(s == m, rid, float(n)), axis=0, keepdims=True)
        hit = rid == idx
        vals.append(m)
        picks.append(idx if payload is None else jnp.sum(jnp.where(hit, payload, 0.0), axis=0, keepdims=True))
        s = jnp.where(hit, NEG_INF, s)
    return jnp.concatenate(vals, axis=0), jnp.concatenate(picks, axis=0)


def _candidate_cells():
    return [(a, b) for a in range(PEER_TOPK) for b in range(PEER_TOPK) if (a + 1) * (b + 1) <= PEER_TOPK]


def _peer_topk_kernel(h_ref, wq_ref, keys_ref, eidx_ref, gw_ref):
    q = _dot(h_ref[...].astype(BF16), wq_ref[...])
    half = PEER_DK // 2
    cells = _candidate_cells()
    for h in range(PEER_HEADS):
        tops = []
        for i in range(2):
            col = (h * 2 + i) * half
            s_t = _dot_nt(keys_ref[h * 2 + i], q[:, col:col + half].astype(BF16))
            tops.append(_top16(s_t))
        (s1, i1), (s2, i2) = tops
        cand = jnp.concatenate([s1[a:a + 1] + s2[b:b + 1] for a, b in cells], axis=0)
        cidx = jnp.concatenate([i1[a:a + 1] * N_KEYS + i2[b:b + 1] for a, b in cells], axis=0)
        top_s, eidx = _top16(cand, cidx)
        e = jnp.exp(top_s - top_s[0:1])
        gw_ref[h * PEER_TOPK:(h + 1) * PEER_TOPK, :] = e / jnp.sum(e, axis=0, keepdims=True)
        eidx_ref[h * PEER_TOPK:(h + 1) * PEER_TOPK, :] = eidx.astype(I32)


def _peer_topk(h, wq, keys, tm):
    n = h.shape[0]
    half = PEER_DK // 2
    return pl.pallas_call(
        _peer_topk_kernel, grid=(n // tm,),
        in_specs=[pl.BlockSpec((tm, D_MODEL), lambda i: (i, 0)),
                  pl.BlockSpec((D_MODEL, PEER_HEADS * PEER_DK), lambda i: (0, 0)),
                  pl.BlockSpec((PEER_HEADS * 2, N_KEYS, half), lambda i: (0, 0, 0))],
        out_specs=[pl.BlockSpec((PEER_ROWS, tm), lambda i: (0, i)),
                   pl.BlockSpec((PEER_ROWS, tm), lambda i: (0, i))],
        out_shape=[jax.ShapeDtypeStruct((PEER_ROWS, n), I32), jax.ShapeDtypeStruct((PEER_ROWS, n), F32)],
        compiler_params=_params(("parallel",)), name="peer_topk")(h, wq, keys)


def _pack_expert_tables(u_tab, v_tab):
    ub = lax.bitcast_convert_type(u_tab.astype(jnp.bfloat16), jnp.uint16).astype(jnp.uint32)
    vb = lax.bitcast_convert_type(v_tab.astype(jnp.bfloat16), jnp.uint16).astype(jnp.uint32)
    words = lax.bitcast_convert_type((vb << 16) | ub, I32)
    return words.reshape(words.shape[0] * PEER_FEAT_TILES, LANES)


def _peer_gather_kernel(idx_ref, gw_ref, x_ref, lng_ref, lnb_ref, uv_hbm, o_ref, idx_smem, *scratch):
    bufs = scratch[:PEER_SLOTS]
    ffn_ref, sem_idx, sem = scratch[PEER_SLOTS:]
    idx_copy = pltpu.make_async_copy(idx_ref, idx_smem, sem_idx)
    idx_copy.start()
    idx_copy.wait()

    tile = PEER_FEAT_TILES

    def issue(t, slot, lo=0, hi=PEER_ROWS):
        for e in range(lo, hi):
            row = pl.multiple_of(idx_smem[t, e], tile)
            copy = pltpu.make_async_copy(uv_hbm.at[pl.ds(row, tile)], bufs[slot].at[pl.ds(e * tile, tile)],
                                         sem.at[slot])
            copy.start(priority=e % 2)

    def wait(slot):
        pltpu.make_async_copy(uv_hbm.at[pl.ds(0, PEER_ROWS * tile)], bufs[slot], sem.at[slot]).wait()

    lane_t = lax.broadcasted_iota(I32, (PEER_ROWS, PEER_TB), 1)

    def feature_block(slot, c):
        return bufs[slot][pl.ds(c, PEER_ROWS, stride=tile), :]

    def token_step(t, slot, t_next, slot_next):
        per_part = PEER_ROWS // (2 * tile)

        def issue_part(k):
            if t_next is not None:
                issue(t_next, slot_next, k * per_part, (k + 1) * per_part)

        x_t = x_ref[pl.ds(t, 1), :]
        acc = None
        for c in range(tile):
            issue_part(c)
            u = lax.bitcast_convert_type(feature_block(slot, c) << 16, F32)
            part = u * x_t[:, c * LANES:(c + 1) * LANES]
            acc = part if acc is None else acc + part
        hidden = jnp.sum(acc, axis=1, keepdims=True)
        g_col = jnp.sum(jnp.where(lane_t == t, gw_ref[...], 0.0), axis=1, keepdims=True)
        coef = g_col * jax.nn.gelu(hidden)
        outs = []
        for c in range(tile):
            issue_part(tile + c)
            v = lax.bitcast_convert_type(feature_block(slot, c) & jnp.int32(-65536), F32)
            outs.append(jnp.sum(coef * v, axis=0, keepdims=True))
        ffn_ref[pl.ds(t, 1), :] = jnp.concatenate(outs, axis=1)

    ahead = PEER_SLOTS - 1
    for t0 in range(ahead):
        issue(t0, t0)

    def group(g, carry):
        for s in range(PEER_SLOTS):
            t = g * PEER_SLOTS + s
            wait(s)
            token_step(t, s, t + ahead, (s + ahead) % PEER_SLOTS)
        return carry

    n_groups = PEER_TB // PEER_SLOTS
    lax.fori_loop(0, n_groups - 1, group, 0)
    for s in range(PEER_SLOTS):
        t = (n_groups - 1) * PEER_SLOTS + s
        wait(s)
        token_step(t, s, t + ahead if t + ahead < PEER_TB else None, (s + ahead) % PEER_SLOTS)
    o_ref[...] = _layer_norm(DN_ALPHA * x_ref[...] + ffn_ref[...], lng_ref[...], lnb_ref[...])


def _peer_gather_ln(eidx, gw_t, x, uv_tab, g, b):
    n = x.shape[0]
    fix = lambda i: (0, 0)
    return pl.pallas_call(
        _peer_gather_kernel, grid=(n // PEER_TB,),
        in_specs=[pl.BlockSpec((PEER_TB, PEER_ROWS), lambda i: (i, 0)),
                  pl.BlockSpec((PEER_ROWS, PEER_TB), lambda i: (0, i)),
                  pl.BlockSpec((PEER_TB, D_MODEL), lambda i: (i, 0)),
                  pl.BlockSpec((1, D_MODEL), fix), pl.BlockSpec((1, D_MODEL), fix),
                  pl.BlockSpec(memory_space=pl.ANY)],
        out_specs=pl.BlockSpec((PEER_TB, D_MODEL), lambda i: (i, 0)),
        out_shape=jax.ShapeDtypeStruct((n, D_MODEL), F32),
        scratch_shapes=[pltpu.SMEM((PEER_TB, PEER_ROWS), I32)] +
                       [pltpu.VMEM((PEER_ROWS * PEER_FEAT_TILES, LANES), I32) for _ in range(PEER_SLOTS)] +
                       [pltpu.VMEM((PEER_TB, D_MODEL), F32),
                        pltpu.SemaphoreType.DMA(()),
                        pltpu.SemaphoreType.DMA((PEER_SLOTS,))],
        compiler_params=_params(("arbitrary",)), name="peer_gather",
    )(eidx, gw_t, x, g, b, uv_tab)


def _rope_tables(pos):
    half = HEAD_DIM // 2
    inv = ROPE_THETA ** (-jnp.arange(half, dtype=F32) / half)
    ang = pos.astype(F32)[:, None] * inv
    cos, sin = jnp.cos(ang), jnp.sin(ang)
    return jnp.tile(cos, (1, 4)), jnp.tile(jnp.concatenate([-sin, sin], axis=1), (1, 2))


def _layer_weights(p):
    w_in = p['w_in']
    w_u = w_in[:, :D_SSM].astype(BF16)
    pad = PROJ_W - (w_in.shape[1] - D_SSM)
    w_rest = jnp.pad(w_in[:, D_SSM:], ((0, 0), (0, pad))).astype(BF16)
    cmp_k = _compress_weights(p['cmp_pe'][0], p['cmp_w1'][0], p['cmp_w2'][0])
    cmp_v = _compress_weights(p['cmp_pe'][1], p['cmp_w1'][1], p['cmp_w2'][1])
    return dict(
        w_u=w_u, w_rest=w_rest, w_glu=p['w_glu'].astype(BF16), d=p['d'].reshape(1, D_SSM),
        wo_ssm=p['w_out'][:D_SSM].astype(BF16), wo_att=p['w_out'][D_SSM:].astype(BF16),
        ln1_g=p['ln1_g'].reshape(1, D_MODEL), ln1_b=p['ln1_b'].reshape(1, D_MODEL),
        ln2_g=p['ln2_g'].reshape(1, D_MODEL), ln2_b=p['ln2_b'].reshape(1, D_MODEL),
        peer_wq=p['peer_wq'].astype(BF16),
        peer_keys=p['peer_keys'].reshape(PEER_HEADS * 2, N_KEYS, PEER_DK // 2).astype(BF16),
        peer_uv=_pack_expert_tables(p['peer_u'], p['peer_v']), cmp_k=cmp_k, cmp_v=cmp_v)


def _token_tail(x, y_ssm, y_att, w, tm):
    h = _outproj_ln(y_ssm, y_att, x, w['wo_ssm'], w['wo_att'], w['ln1_g'], w['ln1_b'], tm)
    eidx_t, gw_t = _peer_topk(h, w['peer_wq'], w['peer_keys'], min(tm, 256))
    rows = eidx_t.T * PEER_FEAT_TILES
    return _peer_gather_ln(rows, gw_t, h, w['peer_uv'], w['ln2_g'], w['ln2_b'])


def _prompt_layer(x, bsz, seq, w, p):
    n = bsz * seq
    tm = min(512, seq)
    cos, sin = _rope_tables(jnp.arange(seq, dtype=I32))
    q, kv, gates = _project(x, w['w_rest'], cos, sin, tm)
    x_tb = x.reshape(bsz, seq, D_MODEL).transpose(1, 0, 2).reshape(n, D_MODEL)
    u_tb = _matmul(x_tb, w['w_u'], tm)
    wb, wc, ar, ai = _s5_weights(p['a_re'], p['a_im'], p['log_dt'], p['b_re'], p['b_im'], p['c_re'], p['c_im'], bsz)
    s0 = jnp.zeros((bsz, 2 * N_STATE), F32)
    gy_tb, s_fin = _s5_scan(u_tb, wb, wc, ar, ai, w['d'], s0, min(64, seq))
    y_ssm_tb = _glu(gy_tb, w['w_glu'], tm)
    y_ssm = y_ssm_tb.reshape(seq, bsz, D_SSM).transpose(1, 0, 2).reshape(n, D_SSM)
    n_cmp = seq // CMP_STRIDE
    xk = kv[:, 0:KV_WIDTH].reshape(bsz, n_cmp, CMP_STRIDE * KV_WIDTH)
    xv = kv[:, KV_WIDTH:2 * KV_WIDTH].reshape(bsz, n_cmp, CMP_STRIDE * KV_WIDTH)
    kc = _compress(xk, *w['cmp_k'])
    vc = _compress(xv, *w['cmp_v'])
    y_att = _attention(q, gates, kv, kc, vc, bsz, seq)
    y = _token_tail(x, y_ssm, y_att, w, tm)
    new_kv = kv[:, :4 * KV_WIDTH].reshape(bsz, seq, 4, N_KV_HEADS, HEAD_DIM)
    new_win = kv[:, 4 * KV_WIDTH:].reshape(bsz, seq, 2, N_KV_HEADS, HEAD_DIM)
    win = jnp.concatenate([jnp.zeros((bsz, WINDOW, 2, N_KV_HEADS, HEAD_DIM), F32), new_win], axis=1)[:, -WINDOW:]
    return y, new_kv, win, s_fin.reshape(bsz, 2, N_GROUPS, SSM_STATE)


def _sample_layer(x, bsz, past, cache, page_table, layer, win_past, ssm_state, w, p):
    rows = x.shape[0]
    cos, sin = _rope_tables(jnp.full((rows,), past, I32))
    q, kv, gates = _project(x, w['w_rest'], cos, sin, rows)
    u = _matmul(x, w['w_u'], rows)
    wb, wc, ar, ai = _s5_weights(p['a_re'], p['a_im'], p['log_dt'], p['b_re'], p['b_im'], p['c_re'], p['c_im'], bsz)
    gy, s_fin = _s5_scan(u[:bsz], wb, wc, ar, ai, w['d'], ssm_state.reshape(bsz, 2 * N_STATE), 1)
    y_ssm = jnp.pad(_glu(gy, w['w_glu'], bsz), ((0, rows - bsz), (0, 0)))
    xk, xv, ks, vs = _page_gather(cache, page_table, layer)
    kc = _compress(xk, *w['cmp_k'])
    vc = _compress(xv, *w['cmp_v'])
    win_len = win_past.shape[1]
    win_rows = win_past.reshape(bsz, win_len, 2 * KV_WIDTH)
    y_att = _decode_attention(q[:bsz].reshape(bsz, 1, D_ATT), gates[:bsz].reshape(bsz, 1, LANES),
                              kv[:bsz].reshape(bsz, 1, 6 * KV_WIDTH), kc, vc, ks, vs, win_rows, past)
    y_att = jnp.pad(y_att.reshape(bsz, D_ATT), ((0, rows - bsz), (0, 0)))
    y = _token_tail(x, y_ssm, y_att, w, rows)
    new_kv = kv[:bsz, :4 * KV_WIDTH].reshape(bsz, 1, 4, N_KV_HEADS, HEAD_DIM)
    new_win = kv[:bsz, 4 * KV_WIDTH:].reshape(bsz, 1, 2, N_KV_HEADS, HEAD_DIM)
    win = jnp.concatenate([win_past, new_win], axis=1)[:, -win_len:]
    return y, new_kv, win, s_fin.reshape(bsz, 2, N_GROUPS, SSM_STATE)


def kernel(x_prompt, x_sample, cache_kv, cache_win, state_ssm, page_table, w_in, ssm_a_re, ssm_a_im, ssm_log_dt, ssm_b_re, ssm_b_im, ssm_c_re, ssm_c_im, ssm_d, w_glu, cmp_pe, cmp_w1, cmp_w2, w_out, ln1_g, ln1_b, peer_wq, peer_keys, peer_u, peer_v, ln2_g, ln2_b):
    bsz, seq, _ = x_prompt.shape
    dec_bsz = x_sample.shape[0]
    depth = w_in.shape[0]
    past = page_table.shape[1] * PAGE_SIZE
    cache = jnp.transpose(cache_kv, (0, 1, 3, 4, 5, 2))
    dec_rows = -(-dec_bsz // PEER_TB) * PEER_TB
    y_p = x_prompt.reshape(bsz * seq, D_MODEL)
    y_s = jnp.pad(x_sample.reshape(dec_bsz, D_MODEL), ((0, dec_rows - dec_bsz), (0, 0)))
    outs = [[] for _ in range(6)]
    for l in range(depth):
        p = {'w_in': w_in[l], 'a_re': ssm_a_re[l], 'a_im': ssm_a_im[l], 'log_dt': ssm_log_dt[l],
             'b_re': ssm_b_re[l], 'b_im': ssm_b_im[l], 'c_re': ssm_c_re[l], 'c_im': ssm_c_im[l],
             'd': ssm_d[l], 'w_glu': w_glu[l], 'cmp_pe': cmp_pe[l], 'cmp_w1': cmp_w1[l], 'cmp_w2': cmp_w2[l],
             'w_out': w_out[l], 'ln1_g': ln1_g[l], 'ln1_b': ln1_b[l], 'peer_wq': peer_wq[l],
             'peer_keys': peer_keys[l], 'peer_u': peer_u[l], 'peer_v': peer_v[l],
             'ln2_g': ln2_g[l], 'ln2_b': ln2_b[l]}
        w = _layer_weights(p)
        y_s, kvn, winn, sn = _sample_layer(y_s, dec_bsz, past, cache, page_table, l, cache_win[l], state_ssm[l], w, p)
        outs[3].append(kvn)
        outs[4].append(winn)
        outs[5].append(sn)
        y_p, kvn, winn, sn = _prompt_layer(y_p, bsz, seq, w, p)
        outs[0].append(kvn)
        outs[1].append(winn)
        outs[2].append(sn)
    return (y_p.reshape(bsz, seq, D_MODEL), y_s[:dec_bsz].reshape(dec_bsz, 1, D_MODEL),
            jnp.stack(outs[0]), jnp.stack(outs[1]), jnp.stack(outs[2]),
            jnp.stack(outs[3]), jnp.stack(outs[4]), jnp.stack(outs[5]))
```

```python
import functools
import math

import jax
import jax.numpy as jnp
from jax import lax
from jax.experimental import pallas as pl
from jax.experimental.pallas import tpu as pltpu

F32 = jnp.float32
BF16 = jnp.bfloat16
I32 = jnp.int32

D_MODEL = 1024
D_SSM = 512
D_ATT = 512
SSM_CH = 16
N_GROUPS = 32
SSM_STATE = 64
N_STATE = N_GROUPS * SSM_STATE
HEAD_DIM = 64
N_HEADS = 8
N_KV_HEADS = 2
GQA = 4
KV_WIDTH = 128
CMP_STRIDE = 16
CMP_LEN = 32
SLC_BLOCK = 64
N_SLC = 16
WINDOW = 512
Q_BLOCK = 64
ROPE_THETA = 10000.0
PEER_HEADS = 8
PEER_DK = 256
N_KEYS = 128
PEER_TOPK = 16
PAGE_SIZE = 128
DN_ALPHA = 4 ** 0.25
LN_EPS = 1e-5
ATT_SCALE = HEAD_DIM ** -0.5
TINY = float(jnp.finfo(jnp.float32).tiny)
NEG_INF = float("-inf")
POS_INF = float("inf")

LANES = 128
PROJ_W = 512 + 6 * KV_WIDTH + LANES
VMEM_LIMIT = 56 * 1024 * 1024
PEER_TB = 128
PEER_SLOTS = 8
PEER_ROWS = PEER_HEADS * PEER_TOPK
PEER_FEAT_TILES = D_MODEL // LANES
PAGES_PER_STEP = 4


def _params(sem, **kw):
    return pltpu.CompilerParams(dimension_semantics=sem, vmem_limit_bytes=VMEM_LIMIT, **kw)


def _dot(a, b):
    return jnp.dot(a, b, preferred_element_type=F32)


def _dot_nt(a, b):
    return lax.dot_general(a, b, (((1,), (1,)), ((), ())), preferred_element_type=F32)


def _dot_exact(a, b):
    bb = b.astype(BF16)
    hi = a.astype(BF16)
    rest = a - hi.astype(F32)
    mid = rest.astype(BF16)
    lo = (rest - mid.astype(F32)).astype(BF16)
    return _dot(hi, bb) + _dot(mid, bb) + _dot(lo, bb)


def _layer_norm(x, g, b):
    mu = jnp.mean(x, axis=-1, keepdims=True)
    xc = x - mu
    var = jnp.mean(xc * xc, axis=-1, keepdims=True)
    return xc * lax.rsqrt(var + LN_EPS) * g + b


def _masked_softmax(s, valid):
    s = jnp.where(valid, s, NEG_INF)
    m = jnp.max(s, axis=-1, keepdims=True)
    m = jnp.where(m > NEG_INF, m, 0.0)
    e = jnp.exp(s - m)
    den = jnp.maximum(jnp.sum(e, axis=-1, keepdims=True), TINY)
    return e / den


def _mm_kernel(x_ref, w_ref, o_ref):
    o_ref[...] = _dot(x_ref[...].astype(BF16), w_ref[...])


def _matmul(x, w, tm):
    n, k = x.shape
    m = w.shape[1]
    return pl.pallas_call(
        _mm_kernel, grid=(n // tm,),
        in_specs=[pl.BlockSpec((tm, k), lambda i: (i, 0)), pl.BlockSpec((k, m), lambda i: (0, 0))],
        out_specs=pl.BlockSpec((tm, m), lambda i: (i, 0)),
        out_shape=jax.ShapeDtypeStruct((n, m), F32),
        compiler_params=_params(("parallel",)), name="mm")(x, w)


def _proj_kernel(x_ref, w_ref, cos_ref, sin_ref, q_ref, kv_ref, g_ref):
    acc = _dot(x_ref[...].astype(BF16), w_ref[...])
    cos = cos_ref[...]
    sin = sin_ref[...]
    lane = lax.broadcasted_iota(I32, cos.shape, 1)
    first_half = (lane % HEAD_DIM) < (HEAD_DIM // 2)

    def rope(v):
        rot = jnp.where(first_half, pltpu.roll(v, 96, 1), pltpu.roll(v, 32, 1))
        return v * cos + rot * sin

    for j in range(4):
        q_ref[:, j * LANES:(j + 1) * LANES] = rope(acc[:, j * LANES:(j + 1) * LANES])
    for j in range(6):
        blk = acc[:, 512 + j * LANES:512 + (j + 1) * LANES]
        kv_ref[:, j * LANES:(j + 1) * LANES] = rope(blk) if j % 2 == 0 else blk
    g_ref[...] = jax.nn.sigmoid(acc[:, 512 + 6 * LANES:])


def _project(x, w, cos, sin, tm):
    n = x.shape[0]
    tab_blocks = cos.shape[0] // tm
    return pl.pallas_call(
        _proj_kernel, grid=(n // tm,),
        in_specs=[pl.BlockSpec((tm, D_MODEL), lambda i: (i, 0)),
                  pl.BlockSpec((D_MODEL, PROJ_W), lambda i: (0, 0)),
                  pl.BlockSpec((tm, LANES), lambda i: (i % tab_blocks, 0)),
                  pl.BlockSpec((tm, LANES), lambda i: (i % tab_blocks, 0))],
        out_specs=[pl.BlockSpec((tm, 512), lambda i: (i, 0)),
                   pl.BlockSpec((tm, 6 * KV_WIDTH), lambda i: (i, 0)),
                   pl.BlockSpec((tm, LANES), lambda i: (i, 0))],
        out_shape=[jax.ShapeDtypeStruct((n, 512), F32),
                   jax.ShapeDtypeStruct((n, 6 * KV_WIDTH), F32),
                   jax.ShapeDtypeStruct((n, LANES), F32)],
        compiler_params=_params(("parallel",)), name="proj")(x, w, cos, sin)


def _glu_kernel(a_ref, w_ref, o_ref):
    gl = _dot(a_ref[...].astype(BF16), w_ref[...])
    o_ref[...] = gl[:, :D_SSM] * jax.nn.sigmoid(gl[:, D_SSM:])


def _glu(a, w, tm):
    n = a.shape[0]
    return pl.pallas_call(
        _glu_kernel, grid=(n // tm,),
        in_specs=[pl.BlockSpec((tm, D_SSM), lambda i: (i, 0)),
                  pl.BlockSpec((D_SSM, 2 * D_SSM), lambda i: (0, 0))],
        out_specs=pl.BlockSpec((tm, D_SSM), lambda i: (i, 0)),
        out_shape=jax.ShapeDtypeStruct((n, D_SSM), F32),
        compiler_params=_params(("parallel",)), name="glu")(a, w)


def _outproj_kernel(ys_ref, ya_ref, x_ref, w1_ref, w2_ref, g_ref, b_ref, o_ref):
    mix = _dot(ys_ref[...].astype(BF16), w1_ref[...]) + _dot(ya_ref[...].astype(BF16), w2_ref[...])
    o_ref[...] = _layer_norm(DN_ALPHA * x_ref[...] + mix, g_ref[...], b_ref[...])


def _outproj_ln(ys, ya, x, w1, w2, g, b, tm):
    n = x.shape[0]
    row = lambda i: (i, 0)
    fix = lambda i: (0, 0)
    return pl.pallas_call(
        _outproj_kernel, grid=(n // tm,),
        in_specs=[pl.BlockSpec((tm, D_SSM), row), pl.BlockSpec((tm, D_ATT), row),
                  pl.BlockSpec((tm, D_MODEL), row),
                  pl.BlockSpec((D_SSM, D_MODEL), fix), pl.BlockSpec((D_ATT, D_MODEL), fix),
                  pl.BlockSpec((1, D_MODEL), fix), pl.BlockSpec((1, D_MODEL), fix)],
        out_specs=pl.BlockSpec((tm, D_MODEL), row),
        out_shape=jax.ShapeDtypeStruct((n, D_MODEL), F32),
        compiler_params=_params(("parallel",)), name="outproj_ln")(ys, ya, x, w1, w2, g, b)


def _s5_kernel(u_ref, wb_ref, wc_ref, ar_ref, ai_ref, d_ref, s0_ref, gy_ref, sf_ref, bu_ref, st_ref,
               *, n_steps, batch):
    @pl.when(pl.program_id(0) == 0)
    def _():
        st_ref[...] = s0_ref[...]

    u = u_ref[...]
    bu_ref[...] = _dot(u.astype(BF16), wb_ref[...])
    n_chunk = 4
    cw = N_STATE // n_chunk

    def step(t, carry):
        r = pl.multiple_of(t * batch, batch)
        new = []
        for c in range(n_chunk):
            sr, si = carry[c], carry[n_chunk + c]
            ar = ar_ref[:, c * cw:(c + 1) * cw]
            ai = ai_ref[:, c * cw:(c + 1) * cw]
            nr = ar * sr - ai * si + bu_ref[pl.ds(r, batch), c * cw:(c + 1) * cw]
            ni = ar * si + ai * sr + bu_ref[pl.ds(r, batch), N_STATE + c * cw:N_STATE + (c + 1) * cw]
            bu_ref[pl.ds(r, batch), c * cw:(c + 1) * cw] = nr
            bu_ref[pl.ds(r, batch), N_STATE + c * cw:N_STATE + (c + 1) * cw] = ni
            new.append((nr, ni))
        return tuple(p[0] for p in new) + tuple(p[1] for p in new)

    init = tuple(st_ref[:, c * cw:(c + 1) * cw] for c in range(n_chunk)) + \
        tuple(st_ref[:, N_STATE + c * cw:N_STATE + (c + 1) * cw] for c in range(n_chunk))
    fin = lax.fori_loop(0, n_steps, step, init)
    for c in range(n_chunk):
        st_ref[:, c * cw:(c + 1) * cw] = fin[c]
        st_ref[:, N_STATE + c * cw:N_STATE + (c + 1) * cw] = fin[n_chunk + c]
    y = _dot(bu_ref[...].astype(BF16), wc_ref[...]) + d_ref[...] * u
    gy_ref[...] = jax.nn.gelu(y)
    sf_ref[...] = st_ref[...]


def _s5_scan(u_tb, wb, wc, ar, ai, d, s0, n_steps):
    batch = s0.shape[0]
    n = u_tb.shape[0]
    rows = n_steps * batch
    fix = lambda i: (0, 0)
    return pl.pallas_call(
        functools.partial(_s5_kernel, n_steps=n_steps, batch=batch),
        grid=(n // rows,),
        in_specs=[pl.BlockSpec((rows, D_SSM), lambda i: (i, 0)),
                  pl.BlockSpec((D_SSM, 2 * N_STATE), fix), pl.BlockSpec((2 * N_STATE, D_SSM), fix),
                  pl.BlockSpec((batch, N_STATE), fix), pl.BlockSpec((batch, N_STATE), fix),
                  pl.BlockSpec((1, D_SSM), fix), pl.BlockSpec((batch, 2 * N_STATE), fix)],
        out_specs=[pl.BlockSpec((rows, D_SSM), lambda i: (i, 0)),
                   pl.BlockSpec((batch, 2 * N_STATE), fix)],
        out_shape=[jax.ShapeDtypeStruct((n, D_SSM), F32),
                   jax.ShapeDtypeStruct((batch, 2 * N_STATE), F32)],
        scratch_shapes=[pltpu.VMEM((rows, 2 * N_STATE), F32), pltpu.VMEM((batch, 2 * N_STATE), F32)],
        compiler_params=_params(("arbitrary",)), name="s5_scan")(u_tb, wb, wc, ar, ai, d, s0)


def _s5_weights(a_re, a_im, log_dt, b_re, b_im, c_re, c_im, batch):
    dt = jnp.exp(log_dt)[:, None]
    mag = jnp.exp(dt * a_re)
    abar_re, abar_im = mag * jnp.cos(dt * a_im), mag * jnp.sin(dt * a_im)
    den = a_re * a_re + a_im * a_im
    f_re = ((abar_re - 1.0) * a_re + abar_im * a_im) / den
    f_im = (abar_im * a_re - (abar_re - 1.0) * a_im) / den
    bb_re = f_re[..., None] * b_re - f_im[..., None] * b_im
    bb_im = f_re[..., None] * b_im + f_im[..., None] * b_re
    eye = jnp.eye(N_GROUPS, dtype=F32)
    wb_re = jnp.einsum('gpc,gh->gchp', bb_re, eye).reshape(D_SSM, N_STATE)
    wb_im = jnp.einsum('gpc,gh->gchp', bb_im, eye).reshape(D_SSM, N_STATE)
    wb = jnp.concatenate([wb_re, wb_im], axis=1).astype(BF16)
    wc_re = jnp.einsum('gcp,gh->gphc', c_re, eye).reshape(N_STATE, D_SSM)
    wc_im = jnp.einsum('gcp,gh->gphc', c_im, eye).reshape(N_STATE, D_SSM)
    wc = jnp.concatenate([wc_re, -wc_im], axis=0).astype(BF16)
    ar = jnp.broadcast_to(abar_re.reshape(1, N_STATE), (batch, N_STATE))
    ai = jnp.broadcast_to(abar_im.reshape(1, N_STATE), (batch, N_STATE))
    return wb, wc, ar, ai


def _compress_kernel(x_ref, pe_ref, w1a_ref, w1b_ref, w2_ref, o_ref):
    x = x_ref[...]
    n = x.shape[0]
    first = _dot((x + pe_ref[0:1, :]).astype(BF16), w1a_ref[...])
    second = _dot((x + pe_ref[1:2, :]).astype(BF16), w1b_ref[...])
    nxt = pltpu.roll(second, n - 1, 0)
    o_ref[...] = _dot(jax.nn.gelu(first + nxt).astype(BF16), w2_ref[...])


def _compress(x, pe, w1a, w1b, w2):
    bsz, n, width = x.shape
    fix = lambda b: (0, 0)
    return pl.pallas_call(
        _compress_kernel, grid=(bsz,),
        in_specs=[pl.BlockSpec((None, n, width), lambda b: (b, 0, 0)),
                  pl.BlockSpec((2, width), fix), pl.BlockSpec((width, KV_WIDTH), fix),
                  pl.BlockSpec((width, KV_WIDTH), fix), pl.BlockSpec((KV_WIDTH, KV_WIDTH), fix)],
        out_specs=pl.BlockSpec((None, n, KV_WIDTH), lambda b: (b, 0, 0)),
        out_shape=jax.ShapeDtypeStruct((bsz, n, KV_WIDTH), F32),
        compiler_params=_params(("parallel",)), name="compress")(x, pe, w1a, w1b, w2)


def _compress_weights(pe, w1, w2):
    eye = jnp.eye(N_KV_HEADS, dtype=F32)
    w1 = w1.reshape(2, CMP_STRIDE, HEAD_DIM, HEAD_DIM)
    pe = pe.reshape(2, CMP_STRIDE, HEAD_DIM)
    w1a = jnp.einsum('jde,hk->jhdke', w1[0], eye).reshape(CMP_STRIDE * KV_WIDTH, KV_WIDTH).astype(BF16)
    w1b = jnp.einsum('jde,hk->jhdke', w1[1], eye).reshape(CMP_STRIDE * KV_WIDTH, KV_WIDTH).astype(BF16)
    w2d = jnp.einsum('de,hk->hdke', w2, eye).reshape(KV_WIDTH, KV_WIDTH).astype(BF16)
    pe2 = jnp.broadcast_to(pe[:, :, None, :], (2, CMP_STRIDE, N_KV_HEADS, HEAD_DIM)).reshape(2, CMP_STRIDE * KV_WIDTH)
    return pe2, w1a, w1b, w2d


def _sel_matrix(n_cmp_rows, n_blk_cols):
    n = jnp.arange(n_cmp_rows)[:, None]
    j = jnp.arange(n_blk_cols)[None, :]
    per = SLC_BLOCK // CMP_STRIDE
    m = (n // per == j).astype(F32) + ((n + 1) // per == j).astype(F32)
    return jnp.where(n < n_cmp_rows - 1, m, 0.0)


def _block_ranks(score, n_blk, last_live=None):
    blk = lax.broadcasted_iota(I32, score.shape, 1)
    segment = 16

    def add_segment(rank, start):
        for i in range(start, min(start + segment, n_blk)):
            col = score[:, i:i + 1]
            before = (col > score) | ((col == score) & (blk > i))
            rank = rank + before.astype(I32)
        return rank

    rank = jnp.zeros(score.shape, I32)
    for start in range(0, n_blk, segment):
        if last_live is None or start == 0:
            rank = add_segment(rank, start)
        else:
            rank = lax.cond(start <= last_live, functools.partial(add_segment, start=start), lambda r: r, rank)
    return rank


def _attn_kernel(q_ref, g_ref, kc_ref, vc_ref, ks_ref, vs_ref, kw_ref, vw_ref, msel_ref, e3_ref, eg_ref,
                 o_ref, s_ref, *, seq, ck, wk):
    qb = pl.program_id(1)
    q0 = qb * Q_BLOCK
    n_cmp = seq // CMP_STRIDE
    n_blk = seq // SLC_BLOCK
    n_sel = min(N_SLC, n_blk)
    rows = GQA * Q_BLOCK
    pos = q0 + lax.broadcasted_iota(I32, (Q_BLOCK, 1), 0)
    pos4 = jnp.concatenate([pos] * GQA, axis=0)
    lane_r = lax.broadcasted_iota(I32, (rows, LANES), 1)
    kc = kc_ref[...].astype(BF16)
    vc = vc_ref[...].astype(BF16)
    n_chunks = (q0 + Q_BLOCK + ck - 1) // ck
    kstart = pl.multiple_of(jnp.maximum(q0 + Q_BLOCK - wk, 0), Q_BLOCK)
    kw = kw_ref[pl.ds(kstart, wk), :].astype(BF16)
    vw = vw_ref[pl.ds(kstart, wk), :].astype(BF16)
    q_heads, out_c, sels = [], [], []
    for hkv in range(N_KV_HEADS):
        parts = []
        for g in range(GQA):
            hd = hkv * GQA + g
            slab = q_ref[:, (hd // 2) * LANES:(hd // 2 + 1) * LANES]
            if hd % 2 != hkv:
                slab = pltpu.roll(slab, HEAD_DIM, 1)
            parts.append(slab)
        qh = jnp.concatenate(parts, axis=0)
        qh = jnp.where(lane_r // HEAD_DIM == hkv, qh * ATT_SCALE, 0.0).astype(BF16)

        s_c = _dot_nt(qh, kc)
        n_id = lax.broadcasted_iota(I32, s_c.shape, 1)
        p_c = _masked_softmax(s_c, n_id * CMP_STRIDE + (CMP_LEN - 1) <= pos4)
        o_c = _dot(p_c.astype(BF16), vc)
        p_grp = p_c[0:Q_BLOCK]
        for g in range(1, GQA):
            p_grp = p_grp + p_c[g * Q_BLOCK:(g + 1) * Q_BLOCK]
        p_slc = _dot_exact(p_grp, msel_ref[...])

        blk = lax.broadcasted_iota(I32, p_slc.shape, 1)
        cur = pos // SLC_BLOCK
        forced = (blk == 0) | (blk == cur) | (blk == cur - 1)
        future = blk * SLC_BLOCK > pos
        score = jnp.where(future, NEG_INF, jnp.where(forced, POS_INF, p_slc))
        q_heads.append(qh)
        out_c.append(o_c)
        sels.append((_block_ranks(score, n_blk, qb) < n_sel).astype(BF16))

    n_chunk_total = seq // ck
    folds = ck // LANES

    def fold(x, op):
        out = x[:, 0:LANES]
        for f in range(1, folds):
            out = op(out, x[:, f * LANES:(f + 1) * LANES])
        return out

    def score_chunk(c, carry):
        base = pl.multiple_of(c * ck, ck)
        k = ks_ref[pl.ds(base, ck), :].astype(BF16)
        kpos = base + lax.broadcasted_iota(I32, (Q_BLOCK, ck), 1)
        causal = kpos <= pos
        new = []
        for hkv in range(N_KV_HEADS):
            s = _dot_nt(q_heads[hkv], k)
            ok = (_dot(sels[hkv], e3_ref[c]) > 0.5) & causal
            s = jnp.where(jnp.concatenate([ok] * GQA, axis=0), s, NEG_INF)
            s_ref[hkv * n_chunk_total + c] = s
            new.append(jnp.maximum(carry[hkv], fold(s, jnp.maximum)))
        return tuple(new)

    neg = jnp.full((rows, LANES), NEG_INF, F32)
    folded_max = lax.fori_loop(0, n_chunks, score_chunk, (neg, neg))
    row_max = []
    for hkv in range(N_KV_HEADS):
        m = jnp.max(folded_max[hkv], axis=-1, keepdims=True)
        row_max.append(jnp.where(m > NEG_INF, m, 0.0))

    def prob_chunk(c, carry):
        base = pl.multiple_of(c * ck, ck)
        v = vs_ref[pl.ds(base, ck), :].astype(BF16)
        new = []
        for hkv in range(N_KV_HEADS):
            l, acc = carry[hkv]
            e = jnp.exp(s_ref[hkv * n_chunk_total + c] - row_max[hkv])
            new.append((l + fold(e, jnp.add), acc + _dot(e.astype(BF16), v)))
        return tuple(new)

    zero = jnp.zeros((rows, LANES), F32)
    sel_state = lax.fori_loop(0, n_chunks, prob_chunk, ((zero, zero), (zero, zero)))

    heads_c, heads_s, heads_w = [], [], []
    for hkv in range(N_KV_HEADS):
        l_s, acc_s = sel_state[hkv]
        o_s = acc_s / jnp.maximum(jnp.sum(l_s, axis=-1, keepdims=True), TINY)
        o_c = out_c[hkv]

        s_w = _dot_nt(q_heads[hkv], kw)
        dpos = pos4 - (kstart + lax.broadcasted_iota(I32, s_w.shape, 1))
        p_w = _masked_softmax(s_w, (dpos >= 0) & (dpos < WINDOW))
        o_w = _dot(p_w.astype(BF16), vw)

        for g in range(GQA):
            sl = (slice(g * Q_BLOCK, (g + 1) * Q_BLOCK), slice(hkv * HEAD_DIM, (hkv + 1) * HEAD_DIM))
            heads_c.append(o_c[sl])
            heads_s.append(o_s[sl])
            heads_w.append(o_w[sl])

    gates = g_ref[...]
    out = jnp.zeros((Q_BLOCK, D_ATT), F32)
    for t, heads in enumerate((heads_c, heads_s, heads_w)):
        out = out + _dot_exact(gates, eg_ref[t]) * jnp.concatenate(heads, axis=1)
    o_ref[...] = out


def _gate_expand():
    c = jnp.arange(LANES)[None, :, None]
    t = jnp.arange(3)[:, None, None]
    h = (jnp.arange(D_ATT) // HEAD_DIM)[None, None, :]
    return (c == t * N_HEADS + h).astype(F32)


def _attention(q, gates, kv, kc, vc, bsz, seq):
    ck = min(512, seq)
    wk = min(WINDOW + Q_BLOCK, seq)
    n_qb = seq // Q_BLOCK
    n_cmp = seq // CMP_STRIDE
    n_blk = seq // SLC_BLOCK
    msel = _sel_matrix(n_cmp, n_blk)
    key_blk = (jnp.arange(seq) // SLC_BLOCK).reshape(seq // ck, 1, ck)
    e3 = (key_blk == jnp.arange(n_blk)[None, :, None]).astype(BF16)
    eg = _gate_expand()
    fix2 = lambda b, i: (0, 0)
    fix3 = lambda b, i: (0, 0, 0)
    kv_spec = lambda col: pl.BlockSpec((seq, KV_WIDTH), lambda b, i: (b, col))
    return pl.pallas_call(
        functools.partial(_attn_kernel, seq=seq, ck=ck, wk=wk),
        grid=(bsz, n_qb),
        in_specs=[pl.BlockSpec((Q_BLOCK, D_ATT), lambda b, i: (b * n_qb + i, 0)),
                  pl.BlockSpec((Q_BLOCK, LANES), lambda b, i: (b * n_qb + i, 0)),
                  pl.BlockSpec((None, n_cmp, KV_WIDTH), lambda b, i: (b, 0, 0)),
                  pl.BlockSpec((None, n_cmp, KV_WIDTH), lambda b, i: (b, 0, 0)),
                  kv_spec(2), kv_spec(3), kv_spec(4), kv_spec(5),
                  pl.BlockSpec((n_cmp, n_blk), fix2),
                  pl.BlockSpec((seq // ck, n_blk, ck), fix3),
                  pl.BlockSpec((3, LANES, D_ATT), fix3)],
        out_specs=pl.BlockSpec((Q_BLOCK, D_ATT), lambda b, i: (b * n_qb + i, 0)),
        out_shape=jax.ShapeDtypeStruct((bsz * seq, D_ATT), F32),
        scratch_shapes=[pltpu.VMEM((N_KV_HEADS * (seq // ck), GQA * Q_BLOCK, ck), F32)],
        compiler_params=_params(("parallel", "arbitrary")), name="nsa_prompt",
    )(q, gates, kc, vc, kv, kv, kv, kv, msel, e3, eg)


def _page_gather_kernel(pt_ref, *refs):
    del pt_ref
    pages = refs[:PAGES_PER_STEP]
    xk_ref, xv_ref, ks_ref, vs_ref, kc_tmp, vc_tmp = refs[PAGES_PER_STEP:]
    per_page = PAGE_SIZE // CMP_STRIDE
    for i, page_ref in enumerate(pages):

        def both_heads(kind):
            return jnp.concatenate([page_ref[kind, h] for h in range(N_KV_HEADS)], axis=0).T

        kc_tmp[...] = both_heads(0)
        vc_tmp[...] = both_heads(1)
        rows = slice(i * per_page, (i + 1) * per_page)
        for j in range(CMP_STRIDE):
            cols = slice(j * KV_WIDTH, (j + 1) * KV_WIDTH)
            xk_ref[rows, cols] = kc_tmp[pl.ds(j, per_page, stride=CMP_STRIDE), :]
            xv_ref[rows, cols] = vc_tmp[pl.ds(j, per_page, stride=CMP_STRIDE), :]
        ks_ref[i * PAGE_SIZE:(i + 1) * PAGE_SIZE, :] = both_heads(2)
        vs_ref[i * PAGE_SIZE:(i + 1) * PAGE_SIZE, :] = both_heads(3)


def _page_gather(cache, page_table, layer):
    bsz, n_pages = page_table.shape
    past = n_pages * PAGE_SIZE
    per_step = PAGES_PER_STEP * (PAGE_SIZE // CMP_STRIDE)
    width = CMP_STRIDE * KV_WIDTH

    def page_spec(i):
        return pl.BlockSpec((None, None, 4, N_KV_HEADS, HEAD_DIM, PAGE_SIZE),
                            lambda b, p, pt: (layer, pt[b, p * PAGES_PER_STEP + i], 0, 0, 0, 0))

    grid_spec = pltpu.PrefetchScalarGridSpec(
        num_scalar_prefetch=1, grid=(bsz, n_pages // PAGES_PER_STEP),
        in_specs=[page_spec(i) for i in range(PAGES_PER_STEP)],
        scratch_shapes=[pltpu.VMEM((PAGE_SIZE, KV_WIDTH), F32), pltpu.VMEM((PAGE_SIZE, KV_WIDTH), F32)],
        out_specs=[pl.BlockSpec((None, per_step, width), lambda b, p, pt: (b, p, 0)),
                   pl.BlockSpec((None, per_step, width), lambda b, p, pt: (b, p, 0)),
                   pl.BlockSpec((None, PAGES_PER_STEP * PAGE_SIZE, KV_WIDTH), lambda b, p, pt: (b, p, 0)),
                   pl.BlockSpec((None, PAGES_PER_STEP * PAGE_SIZE, KV_WIDTH), lambda b, p, pt: (b, p, 0))])
    return pl.pallas_call(
        _page_gather_kernel, grid_spec=grid_spec,
        out_shape=[jax.ShapeDtypeStruct((bsz, past // CMP_STRIDE, width), F32),
                   jax.ShapeDtypeStruct((bsz, past // CMP_STRIDE, width), F32),
                   jax.ShapeDtypeStruct((bsz, past, KV_WIDTH), F32),
                   jax.ShapeDtypeStruct((bsz, past, KV_WIDTH), F32)],
        compiler_params=_params(("parallel", "arbitrary")), name="page_gather",
    )(page_table, *([cache] * PAGES_PER_STEP))


def _dec_attn_kernel(q_ref, g_ref, nkv_ref, kc_ref, vc_ref, ks_ref, vs_ref, win_ref, msel_ref, e_ref, eg_ref,
                     o_ref, *, past, cb):
    pos = past
    n_blk = past // SLC_BLOCK + 1
    n_sel = min(N_SLC, n_blk)
    n_chunks = (past // SLC_BLOCK) // cb
    ckeys = cb * SLC_BLOCK
    win_len = win_ref.shape[0]
    row = lax.broadcasted_iota(I32, (N_HEADS, LANES), 0)
    lane = lax.broadcasted_iota(I32, (N_HEADS, LANES), 1)

    qrow = jnp.broadcast_to(q_ref[...], (N_HEADS, D_ATT))
    x = jnp.zeros((N_HEADS, LANES), F32)
    for j in range(4):
        x = jnp.where(row // 2 == j, qrow[:, j * LANES:(j + 1) * LANES], x)
    x = jnp.where((row % 2) != (row // GQA), pltpu.roll(x, HEAD_DIM, 1), x)
    q8 = jnp.where(lane // HEAD_DIM == row // GQA, x * ATT_SCALE, 0.0).astype(BF16)
    nkv = nkv_ref[...]

    def new_rows(col):
        return jnp.broadcast_to(nkv[:, col * KV_WIDTH:(col + 1) * KV_WIDTH], (N_HEADS, KV_WIDTH)).astype(BF16)

    s_c = _dot_nt(q8, kc_ref[...].astype(BF16))
    n_id = lax.broadcasted_iota(I32, s_c.shape, 1)
    p_c = _masked_softmax(s_c, n_id * CMP_STRIDE + (CMP_LEN - 1) <= pos)
    o_c = _dot(p_c.astype(BF16), vc_ref[...].astype(BF16))
    grp0 = jnp.sum(p_c[0:GQA], axis=0, keepdims=True)
    grp1 = jnp.sum(p_c[GQA:2 * GQA], axis=0, keepdims=True)
    p_grp = jnp.where(lax.broadcasted_iota(I32, p_c.shape, 0) < GQA, grp0, grp1)
    p_slc = _dot_exact(p_grp, msel_ref[...])
    blk = lax.broadcasted_iota(I32, p_slc.shape, 1)
    cur = pos // SLC_BLOCK
    forced = (blk == 0) | (blk == cur) | (blk == cur - 1)
    future = (blk * SLC_BLOCK > pos) | (blk >= n_blk)
    score = jnp.where(future, NEG_INF, jnp.where(forced, POS_INF, p_slc))
    sel = (_block_ranks(score, n_blk) < n_sel).astype(F32)

    k_new = new_rows(2)
    v_new = new_rows(3)
    ok_new = sel[:, cur:cur + 1] > 0.5
    s_new = jnp.where(ok_new, _dot_nt(q8, k_new)[:, 0:1], NEG_INF)
    scores, oks = [], []
    m = s_new
    for c in range(n_chunks):
        s = _dot_nt(q8, ks_ref[c * ckeys:(c + 1) * ckeys, :].astype(BF16))
        ok = _dot(sel[:, c * cb:(c + 1) * cb].astype(BF16), e_ref[...]) > 0.5
        s = jnp.where(ok, s, NEG_INF)
        m = jnp.maximum(m, jnp.max(s, axis=-1, keepdims=True))
        scores.append(s)
        oks.append(ok)
    m = jnp.where(m > NEG_INF, m, 0.0)
    e_new = jnp.where(ok_new, jnp.exp(s_new - m), 0.0)
    den = e_new
    acc = e_new.astype(BF16).astype(F32) * v_new.astype(F32)
    for c in range(n_chunks):
        e = jnp.where(oks[c], jnp.exp(scores[c] - m), 0.0)
        den = den + jnp.sum(e, axis=-1, keepdims=True)
        acc = acc + _dot(e.astype(BF16), vs_ref[c * ckeys:(c + 1) * ckeys, :].astype(BF16))
    o_s = acc / jnp.maximum(den, TINY)

    kw = win_ref[:, 0:KV_WIDTH].astype(BF16)
    vw = win_ref[:, KV_WIDTH:2 * KV_WIDTH].astype(BF16)
    s_w = _dot_nt(q8, kw)
    dist = win_len - lax.broadcasted_iota(I32, s_w.shape, 1)
    ok_w = (dist < WINDOW) & (pos - dist >= 0)
    s_w = jnp.where(ok_w, s_w, NEG_INF)
    s_wn = _dot_nt(q8, new_rows(4))[:, 0:1]
    m_w = jnp.maximum(jnp.max(s_w, axis=-1, keepdims=True), s_wn)
    e_w = jnp.where(ok_w, jnp.exp(s_w - m_w), 0.0)
    e_wn = jnp.exp(s_wn - m_w)
    den_w = jnp.sum(e_w, axis=-1, keepdims=True) + e_wn
    o_w = (_dot(e_w.astype(BF16), vw) + e_wn.astype(BF16).astype(F32) * new_rows(5).astype(F32)) / den_w

    gates = g_ref[...]
    out = jnp.zeros((1, D_ATT), F32)
    for t, o8 in enumerate((o_c, o_s, o_w)):
        pieces = [o8[hd:hd + 1, (hd // GQA) * HEAD_DIM:(hd // GQA + 1) * HEAD_DIM] for hd in range(N_HEADS)]
        g8 = jnp.broadcast_to(gates, (N_HEADS, LANES))
        out = out + _dot_exact(g8, eg_ref[t])[0:1] * jnp.concatenate(pieces, axis=1)
    o_ref[...] = out


def _decode_attention(q, gates, nkv, kc, vc, ks, vs, win, past):
    bsz = q.shape[0]
    n_cmp = kc.shape[1]
    n_past_blk = past // SLC_BLOCK
    cb = min(64, n_past_blk)
    n_blk_pad = -(-(n_past_blk + 1) // LANES) * LANES
    msel = _sel_matrix(n_cmp, n_blk_pad)
    e = (jnp.arange(cb * SLC_BLOCK)[None, :] // SLC_BLOCK == jnp.arange(cb)[:, None]).astype(BF16)
    eg = _gate_expand()
    win_len = win.shape[1]
    per_b = lambda b: (b, 0, 0)
    fix2 = lambda b: (0, 0)
    fix3 = lambda b: (0, 0, 0)
    once = pl.Buffered(1)
    return pl.pallas_call(
        functools.partial(_dec_attn_kernel, past=past, cb=cb),
        grid=(bsz,),
        in_specs=[pl.BlockSpec((None, 1, D_ATT), per_b), pl.BlockSpec((None, 1, LANES), per_b),
                  pl.BlockSpec((None, 1, 6 * KV_WIDTH), per_b),
                  pl.BlockSpec((None, n_cmp, KV_WIDTH), per_b), pl.BlockSpec((None, n_cmp, KV_WIDTH), per_b),
                  pl.BlockSpec((None, past, KV_WIDTH), per_b, pipeline_mode=once),
                  pl.BlockSpec((None, past, KV_WIDTH), per_b, pipeline_mode=once),
                  pl.BlockSpec((None, win_len, 2 * KV_WIDTH), per_b),
                  pl.BlockSpec((n_cmp, n_blk_pad), fix2), pl.BlockSpec((cb, cb * SLC_BLOCK), fix2),
                  pl.BlockSpec((3, LANES, D_ATT), fix3)],
        out_specs=pl.BlockSpec((None, 1, D_ATT), per_b),
        out_shape=jax.ShapeDtypeStruct((bsz, 1, D_ATT), F32),
        compiler_params=_params(("parallel",)), name="nsa_decode",
    )(q, gates, nkv, kc, vc, ks, vs, win, msel, e, eg)


def _top16(s, payload=None):
    n = s.shape[0]
    rid = lax.broadcasted_iota(I32, s.shape, 0).astype(F32)
    vals, picks = [], []
    for _ in range(PEER_TOPK):
        m = jnp.max(s, axis=0, keepdims=True)
        idx = jnp.min(jnp.where(s == m, rid, float(n)), axis=0, keepdims=True)
        hit = rid == idx
        vals.append(m)
        picks.append(idx if payload is None else jnp.sum(jnp.where(hit, payload, 0.0), axis=0, keepdims=True))
        s = jnp.where(hit, NEG_INF, s)
    return jnp.concatenate(vals, axis=0), jnp.concatenate(picks, axis=0)


def _candidate_cells():
    return [(a, b) for a in range(PEER_TOPK) for b in range(PEER_TOPK) if (a + 1) * (b + 1) <= PEER_TOPK]


def _peer_topk_kernel(h_ref, wq_ref, keys_ref, eidx_ref, gw_ref):
    q = _dot(h_ref[...].astype(BF16), wq_ref[...])
    half = PEER_DK // 2
    cells = _candidate_cells()
    for h in range(PEER_HEADS):
        tops = []
        for i in range(2):
            col = (h * 2 + i) * half
            s_t = _dot_nt(keys_ref[h * 2 + i], q[:, col:col + half].astype(BF16))
            tops.append(_top16(s_t))
        (s1, i1), (s2, i2) = tops
        cand = jnp.concatenate([s1[a:a + 1] + s2[b:b + 1] for a, b in cells], axis=0)
        cidx = jnp.concatenate([i1[a:a + 1] * N_KEYS + i2[b:b + 1] for a, b in cells], axis=0)
        top_s, eidx = _top16(cand, cidx)
        e = jnp.exp(top_s - top_s[0:1])
        gw_ref[h * PEER_TOPK:(h + 1) * PEER_TOPK, :] = e / jnp.sum(e, axis=0, keepdims=True)
        eidx_ref[h * PEER_TOPK:(h + 1) * PEER_TOPK, :] = eidx.astype(I32)


def _peer_topk(h, wq, keys, tm):
    n = h.shape[0]
    half = PEER_DK // 2
    return pl.pallas_call(
        _peer_topk_kernel, grid=(n // tm,),
        in_specs=[pl.BlockSpec((tm, D_MODEL), lambda i: (i, 0)),
                  pl.BlockSpec((D_MODEL, PEER_HEADS * PEER_DK), lambda i: (0, 0)),
                  pl.BlockSpec((PEER_HEADS * 2, N_KEYS, half), lambda i: (0, 0, 0))],
        out_specs=[pl.BlockSpec((PEER_ROWS, tm), lambda i: (0, i)),
                   pl.BlockSpec((PEER_ROWS, tm), lambda i: (0, i))],
        out_shape=[jax.ShapeDtypeStruct((PEER_ROWS, n), I32), jax.ShapeDtypeStruct((PEER_ROWS, n), F32)],
        compiler_params=_params(("parallel",)), name="peer_topk")(h, wq, keys)


def _pack_expert_tables(u_tab, v_tab):
    ub = lax.bitcast_convert_type(u_tab.astype(jnp.bfloat16), jnp.uint16).astype(jnp.uint32)
    vb = lax.bitcast_convert_type(v_tab.astype(jnp.bfloat16), jnp.uint16).astype(jnp.uint32)
    words = lax.bitcast_convert_type((vb << 16) | ub, I32)
    return words.reshape(words.shape[0] * PEER_FEAT_TILES, LANES)


def _peer_gather_kernel(idx_ref, gw_ref, x_ref, lng_ref, lnb_ref, uv_hbm, o_ref, idx_smem, *scratch):
    bufs = scratch[:PEER_SLOTS]
    ffn_ref, sem_idx, sem = scratch[PEER_SLOTS:]
    idx_copy = pltpu.make_async_copy(idx_ref, idx_smem, sem_idx)
    idx_copy.start()
    idx_copy.wait()

    tile = PEER_FEAT_TILES

    def issue(t, slot, lo=0, hi=PEER_ROWS):
        for e in range(lo, hi):
            row = pl.multiple_of(idx_smem[t, e], tile)
            copy = pltpu.make_async_copy(uv_hbm.at[pl.ds(row, tile)], bufs[slot].at[pl.ds(e * tile, tile)],
                                         sem.at[slot])
            copy.start(priority=e % 2)

    def wait(slot):
        pltpu.make_async_copy(uv_hbm.at[pl.ds(0, PEER_ROWS * tile)], bufs[slot], sem.at[slot]).wait()

    lane_t = lax.broadcasted_iota(I32, (PEER_ROWS, PEER_TB), 1)

    def feature_block(slot, c):
        return bufs[slot][pl.ds(c, PEER_ROWS, stride=tile), :]

    def token_step(t, slot, t_next, slot_next):
        per_part = PEER_ROWS // (2 * tile)

        def issue_part(k):
            if t_next is not None:
                issue(t_next, slot_next, k * per_part, (k + 1) * per_part)

        x_t = x_ref[pl.ds(t, 1), :]
        acc = None
        for c in range(tile):
            issue_part(c)
            u = lax.bitcast_convert_type(feature_block(slot, c) << 16, F32)
            part = u * x_t[:, c * LANES:(c + 1) * LANES]
            acc = part if acc is None else acc + part
        hidden = jnp.sum(acc, axis=1, keepdims=True)
        g_col = jnp.sum(jnp.where(lane_t == t, gw_ref[...], 0.0), axis=1, keepdims=True)
        coef = g_col * jax.nn.gelu(hidden)
        outs = []
        for c in range(tile):
            issue_part(tile + c)
            v = lax.bitcast_convert_type(feature_block(slot, c) & jnp.int32(-65536), F32)
            outs.append(jnp.sum(coef * v, axis=0, keepdims=True))
        ffn_ref[pl.ds(t, 1), :] = jnp.concatenate(outs, axis=1)

    ahead = PEER_SLOTS - 1
    for t0 in range(ahead):
        issue(t0, t0)

    def group(g, carry):
        for s in range(PEER_SLOTS):
            t = g * PEER_SLOTS + s
            wait(s)
            token_step(t, s, t + ahead, (s + ahead) % PEER_SLOTS)
        return carry

    n_groups = PEER_TB // PEER_SLOTS
    lax.fori_loop(0, n_groups - 1, group, 0)
    for s in range(PEER_SLOTS):
        t = (n_groups - 1) * PEER_SLOTS + s
        wait(s)
        token_step(t, s, t + ahead if t + ahead < PEER_TB else None, (s + ahead) % PEER_SLOTS)
    o_ref[...] = _layer_norm(DN_ALPHA * x_ref[...] + ffn_ref[...], lng_ref[...], lnb_ref[...])


def _peer_gather_ln(eidx, gw_t, x, uv_tab, g, b):
    n = x.shape[0]
    fix = lambda i: (0, 0)
    return pl.pallas_call(
        _peer_gather_kernel, grid=(n // PEER_TB,),
        in_specs=[pl.BlockSpec((PEER_TB, PEER_ROWS), lambda i: (i, 0)),
                  pl.BlockSpec((PEER_ROWS, PEER_TB), lambda i: (0, i)),
                  pl.BlockSpec((PEER_TB, D_MODEL), lambda i: (i, 0)),
                  pl.BlockSpec((1, D_MODEL), fix), pl.BlockSpec((1, D_MODEL), fix),
                  pl.BlockSpec(memory_space=pl.ANY)],
        out_specs=pl.BlockSpec((PEER_TB, D_MODEL), lambda i: (i, 0)),
        out_shape=jax.ShapeDtypeStruct((n, D_MODEL), F32),
        scratch_shapes=[pltpu.SMEM((PEER_TB, PEER_ROWS), I32)] +
                       [pltpu.VMEM((PEER_ROWS * PEER_FEAT_TILES, LANES), I32) for _ in range(PEER_SLOTS)] +
                       [pltpu.VMEM((PEER_TB, D_MODEL), F32),
                        pltpu.SemaphoreType.DMA(()),
                        pltpu.SemaphoreType.DMA((PEER_SLOTS,))],
        compiler_params=_params(("arbitrary",)), name="peer_gather",
    )(eidx, gw_t, x, g, b, uv_tab)


def _rope_tables(pos):
    half = HEAD_DIM // 2
    inv = ROPE_THETA ** (-jnp.arange(half, dtype=F32) / half)
    ang = pos.astype(F32)[:, None] * inv
    cos, sin = jnp.cos(ang), jnp.sin(ang)
    return jnp.tile(cos, (1, 4)), jnp.tile(jnp.concatenate([-sin, sin], axis=1), (1, 2))


def _layer_weights(p):
    w_in = p['w_in']
    w_u = w_in[:, :D_SSM].astype(BF16)
    pad = PROJ_W - (w_in.shape[1] - D_SSM)
    w_rest = jnp.pad(w_in[:, D_SSM:], ((0, 0), (0, pad))).astype(BF16)
    cmp_k = _compress_weights(p['cmp_pe'][0], p['cmp_w1'][0], p['cmp_w2'][0])
    cmp_v = _compress_weights(p['cmp_pe'][1], p['cmp_w1'][1], p['cmp_w2'][1])
    return dict(
        w_u=w_u, w_rest=w_rest, w_glu=p['w_glu'].astype(BF16), d=p['d'].reshape(1, D_SSM),
        wo_ssm=p['w_out'][:D_SSM].astype(BF16), wo_att=p['w_out'][D_SSM:].astype(BF16),
        ln1_g=p['ln1_g'].reshape(1, D_MODEL), ln1_b=p['ln1_b'].reshape(1, D_MODEL),
        ln2_g=p['ln2_g'].reshape(1, D_MODEL), ln2_b=p['ln2_b'].reshape(1, D_MODEL),
        peer_wq=p['peer_wq'].astype(BF16),
        peer_keys=p['peer_keys'].reshape(PEER_HEADS * 2, N_KEYS, PEER_DK // 2).astype(BF16),
        peer_uv=_pack_expert_tables(p['peer_u'], p['peer_v']), cmp_k=cmp_k, cmp_v=cmp_v)


def _token_tail(x, y_ssm, y_att, w, tm):
    h = _outproj_ln(y_ssm, y_att, x, w['wo_ssm'], w['wo_att'], w['ln1_g'], w['ln1_b'], tm)
    eidx_t, gw_t = _peer_topk(h, w['peer_wq'], w['peer_keys'], min(tm, 256))
    rows = eidx_t.T * PEER_FEAT_TILES
    return _peer_gather_ln(rows, gw_t, h, w['peer_uv'], w['ln2_g'], w['ln2_b'])


def _prompt_layer(x, bsz, seq, w, p):
    n = bsz * seq
    tm = min(512, seq)
    cos, sin = _rope_tables(jnp.arange(seq, dtype=I32))
    q, kv, gates = _project(x, w['w_rest'], cos, sin, tm)
    x_tb = x.reshape(bsz, seq, D_MODEL).transpose(1, 0, 2).reshape(n, D_MODEL)
    u_tb = _matmul(x_tb, w['w_u'], tm)
    wb, wc, ar, ai = _s5_weights(p['a_re'], p['a_im'], p['log_dt'], p['b_re'], p['b_im'], p['c_re'], p['c_im'], bsz)
    s0 = jnp.zeros((bsz, 2 * N_STATE), F32)
    gy_tb, s_fin = _s5_scan(u_tb, wb, wc, ar, ai, w['d'], s0, min(64, seq))
    y_ssm_tb = _glu(gy_tb, w['w_glu'], tm)
    y_ssm = y_ssm_tb.reshape(seq, bsz, D_SSM).transpose(1, 0, 2).reshape(n, D_SSM)
    n_cmp = seq // CMP_STRIDE
    xk = kv[:, 0:KV_WIDTH].reshape(bsz, n_cmp, CMP_STRIDE * KV_WIDTH)
    xv = kv[:, KV_WIDTH:2 * KV_WIDTH].reshape(bsz, n_cmp, CMP_STRIDE * KV_WIDTH)
    kc = _compress(xk, *w['cmp_k'])
    vc = _compress(xv, *w['cmp_v'])
    y_att = _attention(q, gates, kv, kc, vc, bsz, seq)
    y = _token_tail(x, y_ssm, y_att, w, tm)
    new_kv = kv[:, :4 * KV_WIDTH].reshape(bsz, seq, 4, N_KV_HEADS, HEAD_DIM)
    new_win = kv[:, 4 * KV_WIDTH:].reshape(bsz, seq, 2, N_KV_HEADS, HEAD_DIM)
    win = jnp.concatenate([jnp.zeros((bsz, WINDOW, 2, N_KV_HEADS, HEAD_DIM), F32), new_win], axis=1)[:, -WINDOW:]
    return y, new_kv, win, s_fin.reshape(bsz, 2, N_GROUPS, SSM_STATE)


def _sample_layer(x, bsz, past, cache, page_table, layer, win_past, ssm_state, w, p):
    rows = x.shape[0]
    cos, sin = _rope_tables(jnp.full((rows,), past, I32))
    q, kv, gates = _project(x, w['w_rest'], cos, sin, rows)
    u = _matmul(x, w['w_u'], rows)
    wb, wc, ar, ai = _s5_weights(p['a_re'], p['a_im'], p['log_dt'], p['b_re'], p['b_im'], p['c_re'], p['c_im'], bsz)
    gy, s_fin = _s5_scan(u[:bsz], wb, wc, ar, ai, w['d'], ssm_state.reshape(bsz, 2 * N_STATE), 1)
    y_ssm = jnp.pad(_glu(gy, w['w_glu'], bsz), ((0, rows - bsz), (0, 0)))
    xk, xv, ks, vs = _page_gather(cache, page_table, layer)
    kc = _compress(xk, *w['cmp_k'])
    vc = _compress(xv, *w['cmp_v'])
    win_len = win_past.shape[1]
    win_rows = win_past.reshape(bsz, win_len, 2 * KV_WIDTH)
    y_att = _decode_attention(q[:bsz].reshape(bsz, 1, D_ATT), gates[:bsz].reshape(bsz, 1, LANES),
                              kv[:bsz].reshape(bsz, 1, 6 * KV_WIDTH), kc, vc, ks, vs, win_rows, past)
    y_att = jnp.pad(y_att.reshape(bsz, D_ATT), ((0, rows - bsz), (0, 0)))
    y = _token_tail(x, y_ssm, y_att, w, rows)
    new_kv = kv[:bsz, :4 * KV_WIDTH].reshape(bsz, 1, 4, N_KV_HEADS, HEAD_DIM)
    new_win = kv[:bsz, 4 * KV_WIDTH:].reshape(bsz, 1, 2, N_KV_HEADS, HEAD_DIM)
    win = jnp.concatenate([win_past, new_win], axis=1)[:, -win_len:]
    return y, new_kv, win, s_fin.reshape(bsz, 2, N_GROUPS, SSM_STATE)


def kernel(x_prompt, x_sample, cache_kv, cache_win, state_ssm, page_table, w_in, ssm_a_re, ssm_a_im, ssm_log_dt, ssm_b_re, ssm_b_im, ssm_c_re, ssm_c_im, ssm_d, w_glu, cmp_pe, cmp_w1, cmp_w2, w_out, ln1_g, ln1_b, peer_wq, peer_keys, peer_u, peer_v, ln2_g, ln2_b):
    bsz, seq, _ = x_prompt.shape
    dec_bsz = x_sample.shape[0]
    depth = w_in.shape[0]
    past = page_table.shape[1] * PAGE_SIZE
    cache = jnp.transpose(cache_kv, (0, 1, 3, 4, 5, 2))
    dec_rows = -(-dec_bsz // PEER_TB) * PEER_TB
    y_p = x_prompt.reshape(bsz * seq, D_MODEL)
    y_s = jnp.pad(x_sample.reshape(dec_bsz, D_MODEL), ((0, dec_rows - dec_bsz), (0, 0)))
    outs = [[] for _ in range(6)]
    for l in range(depth):
        p = {'w_in': w_in[l], 'a_re': ssm_a_re[l], 'a_im': ssm_a_im[l], 'log_dt': ssm_log_dt[l],
             'b_re': ssm_b_re[l], 'b_im': ssm_b_im[l], 'c_re': ssm_c_re[l], 'c_im': ssm_c_im[l],
             'd': ssm_d[l], 'w_glu': w_glu[l], 'cmp_pe': cmp_pe[l], 'cmp_w1': cmp_w1[l], 'cmp_w2': cmp_w2[l],
             'w_out': w_out[l], 'ln1_g': ln1_g[l], 'ln1_b': ln1_b[l], 'peer_wq': peer_wq[l],
             'peer_keys': peer_keys[l], 'peer_u': peer_u[l], 'peer_v': peer_v[l],
             'ln2_g': ln2_g[l], 'ln2_b': ln2_b[l]}
        w = _layer_weights(p)
        y_s, kvn, winn, sn = _sample_layer(y_s, dec_bsz, past, cache, page_table, l, cache_win[l], state_ssm[l], w, p)
        outs[3].append(kvn)
        outs[4].append(winn)
        outs[5].append(sn)
        y_p, kvn, winn, sn = _prompt_layer(y_p, bsz, seq, w, p)
        outs[0].append(kvn)
        outs[1].append(winn)
        outs[2].append(sn)
    return (y_p.reshape(bsz, seq, D_MODEL), y_s[:dec_bsz].reshape(dec_bsz, 1, D_MODEL),
            jnp.stack(outs[0]), jnp.stack(outs[1]), jnp.stack(outs[2]),
            jnp.stack(outs[3]), jnp.stack(outs[4]), jnp.stack(outs[5]))
```

```python
import functools
import math

import jax
import jax.numpy as jnp
from jax import lax
from jax.experimental import pallas as pl
from jax.experimental.pallas import tpu as pltpu

F32 = jnp.float32
BF16 = jnp.bfloat16
I32 = jnp.int32

D_MODEL = 1024
D_SSM = 512
D_ATT = 512
SSM_CH = 16
N_GROUPS = 32
SSM_STATE = 64
N_STATE = N_GROUPS * SSM_STATE
HEAD_DIM = 64
N_HEADS = 8
N_KV_HEADS = 2
GQA = 4
KV_WIDTH = 128
CMP_STRIDE = 16
CMP_LEN = 32
SLC_BLOCK = 64
N_SLC = 16
WINDOW = 512
Q_BLOCK = 64
ROPE_THETA = 10000.0
PEER_HEADS = 8
PEER_DK = 256
N_KEYS = 128
PEER_TOPK = 16
PAGE_SIZE = 128
DN_ALPHA = 4 ** 0.25
LN_EPS = 1e-5
ATT_SCALE = HEAD_DIM ** -0.5
TINY = float(jnp.finfo(jnp.float32).tiny)
NEG_INF = float("-inf")
POS_INF = float("inf")

LANES = 128
PROJ_W = 512 + 6 * KV_WIDTH + LANES
VMEM_LIMIT = 56 * 1024 * 1024
PEER_TB = 128
PEER_SLOTS = 16
PEER_ROWS = PEER_HEADS * PEER_TOPK
PEER_FEAT_TILES = D_MODEL // LANES
PAGES_PER_STEP = 4


def _params(sem, **kw):
    return pltpu.CompilerParams(dimension_semantics=sem, vmem_limit_bytes=VMEM_LIMIT, **kw)


def _dot(a, b):
    return jnp.dot(a, b, preferred_element_type=F32)


def _dot_nt(a, b):
    return lax.dot_general(a, b, (((1,), (1,)), ((), ())), preferred_element_type=F32)


def _dot_exact(a, b):
    bb = b.astype(BF16)
    hi = a.astype(BF16)
    rest = a - hi.astype(F32)
    mid = rest.astype(BF16)
    lo = (rest - mid.astype(F32)).astype(BF16)
    return _dot(hi, bb) + _dot(mid, bb) + _dot(lo, bb)


def _layer_norm(x, g, b):
    mu = jnp.mean(x, axis=-1, keepdims=True)
    xc = x - mu
    var = jnp.mean(xc * xc, axis=-1, keepdims=True)
    return xc * lax.rsqrt(var + LN_EPS) * g + b


def _masked_softmax(s, valid):
    s = jnp.where(valid, s, NEG_INF)
    m = jnp.max(s, axis=-1, keepdims=True)
    m = jnp.where(m > NEG_INF, m, 0.0)
    e = jnp.exp(s - m)
    den = jnp.maximum(jnp.sum(e, axis=-1, keepdims=True), TINY)
    return e / den


def _mm_kernel(x_ref, w_ref, o_ref):
    o_ref[...] = _dot(x_ref[...].astype(BF16), w_ref[...])


def _matmul(x, w, tm):
    n, k = x.shape
    m = w.shape[1]
    return pl.pallas_call(
        _mm_kernel, grid=(n // tm,),
        in_specs=[pl.BlockSpec((tm, k), lambda i: (i, 0)), pl.BlockSpec((k, m), lambda i: (0, 0))],
        out_specs=pl.BlockSpec((tm, m), lambda i: (i, 0)),
        out_shape=jax.ShapeDtypeStruct((n, m), F32),
        compiler_params=_params(("parallel",)), name="mm")(x, w)


def _proj_kernel(x_ref, w_ref, cos_ref, sin_ref, q_ref, kv_ref, g_ref):
    acc = _dot(x_ref[...].astype(BF16), w_ref[...])
    cos = cos_ref[...]
    sin = sin_ref[...]
    lane = lax.broadcasted_iota(I32, cos.shape, 1)
    first_half = (lane % HEAD_DIM) < (HEAD_DIM // 2)

    def rope(v):
        rot = jnp.where(first_half, pltpu.roll(v, 96, 1), pltpu.roll(v, 32, 1))
        return v * cos + rot * sin

    for j in range(4):
        q_ref[:, j * LANES:(j + 1) * LANES] = rope(acc[:, j * LANES:(j + 1) * LANES])
    for j in range(6):
        blk = acc[:, 512 + j * LANES:512 + (j + 1) * LANES]
        kv_ref[:, j * LANES:(j + 1) * LANES] = rope(blk) if j % 2 == 0 else blk
    g_ref[...] = jax.nn.sigmoid(acc[:, 512 + 6 * LANES:])


def _project(x, w, cos, sin, tm):
    n = x.shape[0]
    tab_blocks = cos.shape[0] // tm
    return pl.pallas_call(
        _proj_kernel, grid=(n // tm,),
        in_specs=[pl.BlockSpec((tm, D_MODEL), lambda i: (i, 0)),
                  pl.BlockSpec((D_MODEL, PROJ_W), lambda i: (0, 0)),
                  pl.BlockSpec((tm, LANES), lambda i: (i % tab_blocks, 0)),
                  pl.BlockSpec((tm, LANES), lambda i: (i % tab_blocks, 0))],
        out_specs=[pl.BlockSpec((tm, 512), lambda i: (i, 0)),
                   pl.BlockSpec((tm, 6 * KV_WIDTH), lambda i: (i, 0)),
                   pl.BlockSpec((tm, LANES), lambda i: (i, 0))],
        out_shape=[jax.ShapeDtypeStruct((n, 512), F32),
                   jax.ShapeDtypeStruct((n, 6 * KV_WIDTH), F32),
                   jax.ShapeDtypeStruct((n, LANES), F32)],
        compiler_params=_params(("parallel",)), name="proj")(x, w, cos, sin)


def _glu_kernel(a_ref, w_ref, o_ref):
    gl = _dot(a_ref[...].astype(BF16), w_ref[...])
    o_ref[...] = gl[:, :D_SSM] * jax.nn.sigmoid(gl[:, D_SSM:])


def _glu(a, w, tm):
    n = a.shape[0]
    return pl.pallas_call(
        _glu_kernel, grid=(n // tm,),
        in_specs=[pl.BlockSpec((tm, D_SSM), lambda i: (i, 0)),
                  pl.BlockSpec((D_SSM, 2 * D_SSM), lambda i: (0, 0))],
        out_specs=pl.BlockSpec((tm, D_SSM), lambda i: (i, 0)),
        out_shape=jax.ShapeDtypeStruct((n, D_SSM), F32),
        compiler_params=_params(("parallel",)), name="glu")(a, w)


def _outproj_kernel(ys_ref, ya_ref, x_ref, w1_ref, w2_ref, g_ref, b_ref, o_ref):
    mix = _dot(ys_ref[...].astype(BF16), w1_ref[...]) + _dot(ya_ref[...].astype(BF16), w2_ref[...])
    o_ref[...] = _layer_norm(DN_ALPHA * x_ref[...] + mix, g_ref[...], b_ref[...])


def _outproj_ln(ys, ya, x, w1, w2, g, b, tm):
    n = x.shape[0]
    row = lambda i: (i, 0)
    fix = lambda i: (0, 0)
    return pl.pallas_call(
        _outproj_kernel, grid=(n // tm,),
        in_specs=[pl.BlockSpec((tm, D_SSM), row), pl.BlockSpec((tm, D_ATT), row),
                  pl.BlockSpec((tm, D_MODEL), row),
                  pl.BlockSpec((D_SSM, D_MODEL), fix), pl.BlockSpec((D_ATT, D_MODEL), fix),
                  pl.BlockSpec((1, D_MODEL), fix), pl.BlockSpec((1, D_MODEL), fix)],
        out_specs=pl.BlockSpec((tm, D_MODEL), row),
        out_shape=jax.ShapeDtypeStruct((n, D_MODEL), F32),
        compiler_params=_params(("parallel",)), name="outproj_ln")(ys, ya, x, w1, w2, g, b)


def _s5_kernel(u_ref, wb_ref, wc_ref, ar_ref, ai_ref, d_ref, s0_ref, gy_ref, sf_ref, bu_ref, st_ref,
               *, n_steps, batch):
    @pl.when(pl.program_id(0) == 0)
    def _():
        st_ref[...] = s0_ref[...]

    u = u_ref[...]
    bu_ref[...] = _dot(u.astype(BF16), wb_ref[...])
    n_chunk = 4
    cw = N_STATE // n_chunk

    def step(t, carry):
        r = pl.multiple_of(t * batch, batch)
        new = []
        for c in range(n_chunk):
            sr, si = carry[c], carry[n_chunk + c]
            ar = ar_ref[:, c * cw:(c + 1) * cw]
            ai = ai_ref[:, c * cw:(c + 1) * cw]
            nr = ar * sr - ai * si + bu_ref[pl.ds(r, batch), c * cw:(c + 1) * cw]
            ni = ar * si + ai * sr + bu_ref[pl.ds(r, batch), N_STATE + c * cw:N_STATE + (c + 1) * cw]
            bu_ref[pl.ds(r, batch), c * cw:(c + 1) * cw] = nr
            bu_ref[pl.ds(r, batch), N_STATE + c * cw:N_STATE + (c + 1) * cw] = ni
            new.append((nr, ni))
        return tuple(p[0] for p in new) + tuple(p[1] for p in new)

    init = tuple(st_ref[:, c * cw:(c + 1) * cw] for c in range(n_chunk)) + \
        tuple(st_ref[:, N_STATE + c * cw:N_STATE + (c + 1) * cw] for c in range(n_chunk))
    fin = lax.fori_loop(0, n_steps, step, init)
    for c in range(n_chunk):
        st_ref[:, c * cw:(c + 1) * cw] = fin[c]
        st_ref[:, N_STATE + c * cw:N_STATE + (c + 1) * cw] = fin[n_chunk + c]
    y = _dot(bu_ref[...].astype(BF16), wc_ref[...]) + d_ref[...] * u
    gy_ref[...] = jax.nn.gelu(y)
    sf_ref[...] = st_ref[...]


def _s5_scan(u_tb, wb, wc, ar, ai, d, s0, n_steps):
    batch = s0.shape[0]
    n = u_tb.shape[0]
    rows = n_steps * batch
    fix = lambda i: (0, 0)
    return pl.pallas_call(
        functools.partial(_s5_kernel, n_steps=n_steps, batch=batch),
        grid=(n // rows,),
        in_specs=[pl.BlockSpec((rows, D_SSM), lambda i: (i, 0)),
                  pl.BlockSpec((D_SSM, 2 * N_STATE), fix), pl.BlockSpec((2 * N_STATE, D_SSM), fix),
                  pl.BlockSpec((batch, N_STATE), fix), pl.BlockSpec((batch, N_STATE), fix),
                  pl.BlockSpec((1, D_SSM), fix), pl.BlockSpec((batch, 2 * N_STATE), fix)],
        out_specs=[pl.BlockSpec((rows, D_SSM), lambda i: (i, 0)),
                   pl.BlockSpec((batch, 2 * N_STATE), fix)],
        out_shape=[jax.ShapeDtypeStruct((n, D_SSM), F32),
                   jax.ShapeDtypeStruct((batch, 2 * N_STATE), F32)],
        scratch_shapes=[pltpu.VMEM((rows, 2 * N_STATE), F32), pltpu.VMEM((batch, 2 * N_STATE), F32)],
        compiler_params=_params(("arbitrary",)), name="s5_scan")(u_tb, wb, wc, ar, ai, d, s0)


def _s5_weights(a_re, a_im, log_dt, b_re, b_im, c_re, c_im, batch):
    dt = jnp.exp(log_dt)[:, None]
    mag = jnp.exp(dt * a_re)
    abar_re, abar_im = mag * jnp.cos(dt * a_im), mag * jnp.sin(dt * a_im)
    den = a_re * a_re + a_im * a_im
    f_re = ((abar_re - 1.0) * a_re + abar_im * a_im) / den
    f_im = (abar_im * a_re - (abar_re - 1.0) * a_im) / den
    bb_re = f_re[..., None] * b_re - f_im[..., None] * b_im
    bb_im = f_re[..., None] * b_im + f_im[..., None] * b_re
    eye = jnp.eye(N_GROUPS, dtype=F32)
    wb_re = jnp.einsum('gpc,gh->gchp', bb_re, eye).reshape(D_SSM, N_STATE)
    wb_im = jnp.einsum('gpc,gh->gchp', bb_im, eye).reshape(D_SSM, N_STATE)
    wb = jnp.concatenate([wb_re, wb_im], axis=1).astype(BF16)
    wc_re = jnp.einsum('gcp,gh->gphc', c_re, eye).reshape(N_STATE, D_SSM)
    wc_im = jnp.einsum('gcp,gh->gphc', c_im, eye).reshape(N_STATE, D_SSM)
    wc = jnp.concatenate([wc_re, -wc_im], axis=0).astype(BF16)
    ar = jnp.broadcast_to(abar_re.reshape(1, N_STATE), (batch, N_STATE))
    ai = jnp.broadcast_to(abar_im.reshape(1, N_STATE), (batch, N_STATE))
    return wb, wc, ar, ai


def _compress_kernel(x_ref, pe_ref, w1a_ref, w1b_ref, w2_ref, o_ref):
    x = x_ref[...]
    n = x.shape[0]
    first = _dot((x + pe_ref[0:1, :]).astype(BF16), w1a_ref[...])
    second = _dot((x + pe_ref[1:2, :]).astype(BF16), w1b_ref[...])
    nxt = pltpu.roll(second, n - 1, 0)
    o_ref[...] = _dot(jax.nn.gelu(first + nxt).astype(BF16), w2_ref[...])


def _compress(x, pe, w1a, w1b, w2):
    bsz, n, width = x.shape
    fix = lambda b: (0, 0)
    return pl.pallas_call(
        _compress_kernel, grid=(bsz,),
        in_specs=[pl.BlockSpec((None, n, width), lambda b: (b, 0, 0)),
                  pl.BlockSpec((2, width), fix), pl.BlockSpec((width, KV_WIDTH), fix),
                  pl.BlockSpec((width, KV_WIDTH), fix), pl.BlockSpec((KV_WIDTH, KV_WIDTH), fix)],
        out_specs=pl.BlockSpec((None, n, KV_WIDTH), lambda b: (b, 0, 0)),
        out_shape=jax.ShapeDtypeStruct((bsz, n, KV_WIDTH), F32),
        compiler_params=_params(("parallel",)), name="compress")(x, pe, w1a, w1b, w2)


def _compress_weights(pe, w1, w2):
    eye = jnp.eye(N_KV_HEADS, dtype=F32)
    w1 = w1.reshape(2, CMP_STRIDE, HEAD_DIM, HEAD_DIM)
    pe = pe.reshape(2, CMP_STRIDE, HEAD_DIM)
    w1a = jnp.einsum('jde,hk->jhdke', w1[0], eye).reshape(CMP_STRIDE * KV_WIDTH, KV_WIDTH).astype(BF16)
    w1b = jnp.einsum('jde,hk->jhdke', w1[1], eye).reshape(CMP_STRIDE * KV_WIDTH, KV_WIDTH).astype(BF16)
    w2d = jnp.einsum('de,hk->hdke', w2, eye).reshape(KV_WIDTH, KV_WIDTH).astype(BF16)
    pe2 = jnp.broadcast_to(pe[:, :, None, :], (2, CMP_STRIDE, N_KV_HEADS, HEAD_DIM)).reshape(2, CMP_STRIDE * KV_WIDTH)
    return pe2, w1a, w1b, w2d


def _sel_matrix(n_cmp_rows, n_blk_cols):
    n = jnp.arange(n_cmp_rows)[:, None]
    j = jnp.arange(n_blk_cols)[None, :]
    per = SLC_BLOCK // CMP_STRIDE
    m = (n // per == j).astype(F32) + ((n + 1) // per == j).astype(F32)
    return jnp.where(n < n_cmp_rows - 1, m, 0.0)


def _block_ranks(score, n_blk, last_live=None):
    blk = lax.broadcasted_iota(I32, score.shape, 1)
    segment = 16

    def add_segment(rank, start):
        for i in range(start, min(start + segment, n_blk)):
            col = score[:, i:i + 1]
            before = (col > score) | ((col == score) & (blk > i))
            rank = rank + before.astype(I32)
        return rank

    rank = jnp.zeros(score.shape, I32)
    for start in range(0, n_blk, segment):
        if last_live is None or start == 0:
            rank = add_segment(rank, start)
        else:
            rank = lax.cond(start <= last_live, functools.partial(add_segment, start=start), lambda r: r, rank)
    return rank


def _attn_kernel(q_ref, g_ref, kc_ref, vc_ref, ks_ref, vs_ref, kw_ref, vw_ref, msel_ref, e3_ref, eg_ref,
                 o_ref, s_ref, *, seq, ck, wk):
    qb = pl.program_id(1)
    q0 = qb * Q_BLOCK
    n_cmp = seq // CMP_STRIDE
    n_blk = seq // SLC_BLOCK
    n_sel = min(N_SLC, n_blk)
    rows = GQA * Q_BLOCK
    pos = q0 + lax.broadcasted_iota(I32, (Q_BLOCK, 1), 0)
    pos4 = jnp.concatenate([pos] * GQA, axis=0)
    lane_r = lax.broadcasted_iota(I32, (rows, LANES), 1)
    kc = kc_ref[...].astype(BF16)
    vc = vc_ref[...].astype(BF16)
    n_chunks = (q0 + Q_BLOCK + ck - 1) // ck
    kstart = pl.multiple_of(jnp.maximum(q0 + Q_BLOCK - wk, 0), Q_BLOCK)
    kw = kw_ref[pl.ds(kstart, wk), :].astype(BF16)
    vw = vw_ref[pl.ds(kstart, wk), :].astype(BF16)
    q_heads, out_c, sels = [], [], []
    for hkv in range(N_KV_HEADS):
        parts = []
        for g in range(GQA):
            hd = hkv * GQA + g
            slab = q_ref[:, (hd // 2) * LANES:(hd // 2 + 1) * LANES]
            if hd % 2 != hkv:
                slab = pltpu.roll(slab, HEAD_DIM, 1)
            parts.append(slab)
        qh = jnp.concatenate(parts, axis=0)
        qh = jnp.where(lane_r // HEAD_DIM == hkv, qh * ATT_SCALE, 0.0).astype(BF16)

        s_c = _dot_nt(qh, kc)
        n_id = lax.broadcasted_iota(I32, s_c.shape, 1)
        p_c = _masked_softmax(s_c, n_id * CMP_STRIDE + (CMP_LEN - 1) <= pos4)
        o_c = _dot(p_c.astype(BF16), vc)
        p_grp = p_c[0:Q_BLOCK]
        for g in range(1, GQA):
            p_grp = p_grp + p_c[g * Q_BLOCK:(g + 1) * Q_BLOCK]
        p_slc = _dot_exact(p_grp, msel_ref[...])

        blk = lax.broadcasted_iota(I32, p_slc.shape, 1)
        cur = pos // SLC_BLOCK
        forced = (blk == 0) | (blk == cur) | (blk == cur - 1)
        future = blk * SLC_BLOCK > pos
        score = jnp.where(future, NEG_INF, jnp.where(forced, POS_INF, p_slc))
        q_heads.append(qh)
        out_c.append(o_c)
        sels.append((_block_ranks(score, n_blk, qb) < n_sel).astype(BF16))

    n_chunk_total = seq // ck
    folds = ck // LANES

    def fold(x, op):
        out = x[:, 0:LANES]
        for f in range(1, folds):
            out = op(out, x[:, f * LANES:(f + 1) * LANES])
        return out

    def score_chunk(c, carry):
        base = pl.multiple_of(c * ck, ck)
        k = ks_ref[pl.ds(base, ck), :].astype(BF16)
        kpos = base + lax.broadcasted_iota(I32, (Q_BLOCK, ck), 1)
        causal = kpos <= pos
        new = []
        for hkv in range(N_KV_HEADS):
            s = _dot_nt(q_heads[hkv], k)
            ok = (_dot(sels[hkv], e3_ref[c]) > 0.5) & causal
            s = jnp.where(jnp.concatenate([ok] * GQA, axis=0), s, NEG_INF)
            s_ref[hkv * n_chunk_total + c] = s
            new.append(jnp.maximum(carry[hkv], fold(s, jnp.maximum)))
        return tuple(new)

    neg = jnp.full((rows, LANES), NEG_INF, F32)
    folded_max = lax.fori_loop(0, n_chunks, score_chunk, (neg, neg))
    row_max = []
    for hkv in range(N_KV_HEADS):
        m = jnp.max(folded_max[hkv], axis=-1, keepdims=True)
        row_max.append(jnp.where(m > NEG_INF, m, 0.0))

    def prob_chunk(c, carry):
        base = pl.multiple_of(c * ck, ck)
        v = vs_ref[pl.ds(base, ck), :].astype(BF16)
        new = []
        for hkv in range(N_KV_HEADS):
            l, acc = carry[hkv]
            e = jnp.exp(s_ref[hkv * n_chunk_total + c] - row_max[hkv])
            new.append((l + fold(e, jnp.add), acc + _dot(e.astype(BF16), v)))
        return tuple(new)

    zero = jnp.zeros((rows, LANES), F32)
    sel_state = lax.fori_loop(0, n_chunks, prob_chunk, ((zero, zero), (zero, zero)))

    heads_c, heads_s, heads_w = [], [], []
    for hkv in range(N_KV_HEADS):
        l_s, acc_s = sel_state[hkv]
        o_s = acc_s / jnp.maximum(jnp.sum(l_s, axis=-1, keepdims=True), TINY)
        o_c = out_c[hkv]

        s_w = _dot_nt(q_heads[hkv], kw)
        dpos = pos4 - (kstart + lax.broadcasted_iota(I32, s_w.shape, 1))
        p_w = _masked_softmax(s_w, (dpos >= 0) & (dpos < WINDOW))
        o_w = _dot(p_w.astype(BF16), vw)

        for g in range(GQA):
            sl = (slice(g * Q_BLOCK, (g + 1) * Q_BLOCK), slice(hkv * HEAD_DIM, (hkv + 1) * HEAD_DIM))
            heads_c.append(o_c[sl])
            heads_s.append(o_s[sl])
            heads_w.append(o_w[sl])

    gates = g_ref[...]
    out = jnp.zeros((Q_BLOCK, D_ATT), F32)
    for t, heads in enumerate((heads_c, heads_s, heads_w)):
        out = out + _dot_exact(gates, eg_ref[t]) * jnp.concatenate(heads, axis=1)
    o_ref[...] = out


def _gate_expand():
    c = jnp.arange(LANES)[None, :, None]
    t = jnp.arange(3)[:, None, None]
    h = (jnp.arange(D_ATT) // HEAD_DIM)[None, None, :]
    return (c == t * N_HEADS + h).astype(F32)


def _attention(q, gates, kv, kc, vc, bsz, seq):
    ck = min(512, seq)
    wk = min(WINDOW + Q_BLOCK, seq)
    n_qb = seq // Q_BLOCK
    n_cmp = seq // CMP_STRIDE
    n_blk = seq // SLC_BLOCK
    msel = _sel_matrix(n_cmp, n_blk)
    key_blk = (jnp.arange(seq) // SLC_BLOCK).reshape(seq // ck, 1, ck)
    e3 = (key_blk == jnp.arange(n_blk)[None, :, None]).astype(BF16)
    eg = _gate_expand()
    fix2 = lambda b, i: (0, 0)
    fix3 = lambda b, i: (0, 0, 0)
    kv_spec = lambda col: pl.BlockSpec((seq, KV_WIDTH), lambda b, i: (b, col))
    return pl.pallas_call(
        functools.partial(_attn_kernel, seq=seq, ck=ck, wk=wk),
        grid=(bsz, n_qb),
        in_specs=[pl.BlockSpec((Q_BLOCK, D_ATT), lambda b, i: (b * n_qb + i, 0)),
                  pl.BlockSpec((Q_BLOCK, LANES), lambda b, i: (b * n_qb + i, 0)),
                  pl.BlockSpec((None, n_cmp, KV_WIDTH), lambda b, i: (b, 0, 0)),
                  pl.BlockSpec((None, n_cmp, KV_WIDTH), lambda b, i: (b, 0, 0)),
                  kv_spec(2), kv_spec(3), kv_spec(4), kv_spec(5),
                  pl.BlockSpec((n_cmp, n_blk), fix2),
                  pl.BlockSpec((seq // ck, n_blk, ck), fix3),
                  pl.BlockSpec((3, LANES, D_ATT), fix3)],
        out_specs=pl.BlockSpec((Q_BLOCK, D_ATT), lambda b, i: (b * n_qb + i, 0)),
        out_shape=jax.ShapeDtypeStruct((bsz * seq, D_ATT), F32),
        scratch_shapes=[pltpu.VMEM((N_KV_HEADS * (seq // ck), GQA * Q_BLOCK, ck), F32)],
        compiler_params=_params(("parallel", "arbitrary")), name="nsa_prompt",
    )(q, gates, kc, vc, kv, kv, kv, kv, msel, e3, eg)


def _page_gather_kernel(pt_ref, *refs):
    del pt_ref
    pages = refs[:PAGES_PER_STEP]
    xk_ref, xv_ref, ks_ref, vs_ref, kc_tmp, vc_tmp = refs[PAGES_PER_STEP:]
    per_page = PAGE_SIZE // CMP_STRIDE
    for i, page_ref in enumerate(pages):

        def both_heads(kind):
            return jnp.concatenate([page_ref[kind, h] for h in range(N_KV_HEADS)], axis=0).T

        kc_tmp[...] = both_heads(0)
        vc_tmp[...] = both_heads(1)
        rows = slice(i * per_page, (i + 1) * per_page)
        for j in range(CMP_STRIDE):
            cols = slice(j * KV_WIDTH, (j + 1) * KV_WIDTH)
            xk_ref[rows, cols] = kc_tmp[pl.ds(j, per_page, stride=CMP_STRIDE), :]
            xv_ref[rows, cols] = vc_tmp[pl.ds(j, per_page, stride=CMP_STRIDE), :]
        ks_ref[i * PAGE_SIZE:(i + 1) * PAGE_SIZE, :] = both_heads(2)
        vs_ref[i * PAGE_SIZE:(i + 1) * PAGE_SIZE, :] = both_heads(3)


def _page_gather(cache, page_table, layer):
    bsz, n_pages = page_table.shape
    past = n_pages * PAGE_SIZE
    per_step = PAGES_PER_STEP * (PAGE_SIZE // CMP_STRIDE)
    width = CMP_STRIDE * KV_WIDTH

    def page_spec(i):
        return pl.BlockSpec((None, None, 4, N_KV_HEADS, HEAD_DIM, PAGE_SIZE),
                            lambda b, p, pt: (layer, pt[b, p * PAGES_PER_STEP + i], 0, 0, 0, 0))

    grid_spec = pltpu.PrefetchScalarGridSpec(
        num_scalar_prefetch=1, grid=(bsz, n_pages // PAGES_PER_STEP),
        in_specs=[page_spec(i) for i in range(PAGES_PER_STEP)],
        scratch_shapes=[pltpu.VMEM((PAGE_SIZE, KV_WIDTH), F32), pltpu.VMEM((PAGE_SIZE, KV_WIDTH), F32)],
        out_specs=[pl.BlockSpec((None, per_step, width), lambda b, p, pt: (b, p, 0)),
                   pl.BlockSpec((None, per_step, width), lambda b, p, pt: (b, p, 0)),
                   pl.BlockSpec((None, PAGES_PER_STEP * PAGE_SIZE, KV_WIDTH), lambda b, p, pt: (b, p, 0)),
                   pl.BlockSpec((None, PAGES_PER_STEP * PAGE_SIZE, KV_WIDTH), lambda b, p, pt: (b, p, 0))])
    return pl.pallas_call(
        _page_gather_kernel, grid_spec=grid_spec,
        out_shape=[jax.ShapeDtypeStruct((bsz, past // CMP_STRIDE, width), F32),
                   jax.ShapeDtypeStruct((bsz, past // CMP_STRIDE, width), F32),
                   jax.ShapeDtypeStruct((bsz, past, KV_WIDTH), F32),
                   jax.ShapeDtypeStruct((bsz, past, KV_WIDTH), F32)],
        compiler_params=_params(("parallel", "arbitrary")), name="page_gather",
    )(page_table, *([cache] * PAGES_PER_STEP))


def _dec_attn_kernel(q_ref, g_ref, nkv_ref, kc_ref, vc_ref, ks_ref, vs_ref, win_ref, msel_ref, e_ref, eg_ref,
                     o_ref, *, past, cb):
    pos = past
    n_blk = past // SLC_BLOCK + 1
    n_sel = min(N_SLC, n_blk)
    n_chunks = (past // SLC_BLOCK) // cb
    ckeys = cb * SLC_BLOCK
    win_len = win_ref.shape[0]
    row = lax.broadcasted_iota(I32, (N_HEADS, LANES), 0)
    lane = lax.broadcasted_iota(I32, (N_HEADS, LANES), 1)

    qrow = jnp.broadcast_to(q_ref[...], (N_HEADS, D_ATT))
    x = jnp.zeros((N_HEADS, LANES), F32)
    for j in range(4):
        x = jnp.where(row // 2 == j, qrow[:, j * LANES:(j + 1) * LANES], x)
    x = jnp.where((row % 2) != (row // GQA), pltpu.roll(x, HEAD_DIM, 1), x)
    q8 = jnp.where(lane // HEAD_DIM == row // GQA, x * ATT_SCALE, 0.0).astype(BF16)
    nkv = nkv_ref[...]

    def new_rows(col):
        return jnp.broadcast_to(nkv[:, col * KV_WIDTH:(col + 1) * KV_WIDTH], (N_HEADS, KV_WIDTH)).astype(BF16)

    s_c = _dot_nt(q8, kc_ref[...].astype(BF16))
    n_id = lax.broadcasted_iota(I32, s_c.shape, 1)
    p_c = _masked_softmax(s_c, n_id * CMP_STRIDE + (CMP_LEN - 1) <= pos)
    o_c = _dot(p_c.astype(BF16), vc_ref[...].astype(BF16))
    grp0 = jnp.sum(p_c[0:GQA], axis=0, keepdims=True)
    grp1 = jnp.sum(p_c[GQA:2 * GQA], axis=0, keepdims=True)
    p_grp = jnp.where(lax.broadcasted_iota(I32, p_c.shape, 0) < GQA, grp0, grp1)
    p_slc = _dot_exact(p_grp, msel_ref[...])
    blk = lax.broadcasted_iota(I32, p_slc.shape, 1)
    cur = pos // SLC_BLOCK
    forced = (blk == 0) | (blk == cur) | (blk == cur - 1)
    future = (blk * SLC_BLOCK > pos) | (blk >= n_blk)
    score = jnp.where(future, NEG_INF, jnp.where(forced, POS_INF, p_slc))
    sel = (_block_ranks(score, n_blk) < n_sel).astype(F32)

    k_new = new_rows(2)
    v_new = new_rows(3)
    ok_new = sel[:, cur:cur + 1] > 0.5
    s_new = jnp.where(ok_new, _dot_nt(q8, k_new)[:, 0:1], NEG_INF)
    scores, oks = [], []
    m = s_new
    for c in range(n_chunks):
        s = _dot_nt(q8, ks_ref[c * ckeys:(c + 1) * ckeys, :].astype(BF16))
        ok = _dot(sel[:, c * cb:(c + 1) * cb].astype(BF16), e_ref[...]) > 0.5
        s = jnp.where(ok, s, NEG_INF)
        m = jnp.maximum(m, jnp.max(s, axis=-1, keepdims=True))
        scores.append(s)
        oks.append(ok)
    m = jnp.where(m > NEG_INF, m, 0.0)
    e_new = jnp.where(ok_new, jnp.exp(s_new - m), 0.0)
    den = e_new
    acc = e_new.astype(BF16).astype(F32) * v_new.astype(F32)
    for c in range(n_chunks):
        e = jnp.where(oks[c], jnp.exp(scores[c] - m), 0.0)
        den = den + jnp.sum(e, axis=-1, keepdims=True)
        acc = acc + _dot(e.astype(BF16), vs_ref[c * ckeys:(c + 1) * ckeys, :].astype(BF16))
    o_s = acc / jnp.maximum(den, TINY)

    kw = win_ref[:, 0:KV_WIDTH].astype(BF16)
    vw = win_ref[:, KV_WIDTH:2 * KV_WIDTH].astype(BF16)
    s_w = _dot_nt(q8, kw)
    dist = win_len - lax.broadcasted_iota(I32, s_w.shape, 1)
    ok_w = (dist < WINDOW) & (pos - dist >= 0)
    s_w = jnp.where(ok_w, s_w, NEG_INF)
    s_wn = _dot_nt(q8, new_rows(4))[:, 0:1]
    m_w = jnp.maximum(jnp.max(s_w, axis=-1, keepdims=True), s_wn)
    e_w = jnp.where(ok_w, jnp.exp(s_w - m_w), 0.0)
    e_wn = jnp.exp(s_wn - m_w)
    den_w = jnp.sum(e_w, axis=-1, keepdims=True) + e_wn
    o_w = (_dot(e_w.astype(BF16), vw) + e_wn.astype(BF16).astype(F32) * new_rows(5).astype(F32)) / den_w

    gates = g_ref[...]
    out = jnp.zeros((1, D_ATT), F32)
    for t, o8 in enumerate((o_c, o_s, o_w)):
        pieces = [o8[hd:hd + 1, (hd // GQA) * HEAD_DIM:(hd // GQA + 1) * HEAD_DIM] for hd in range(N_HEADS)]
        g8 = jnp.broadcast_to(gates, (N_HEADS, LANES))
        out = out + _dot_exact(g8, eg_ref[t])[0:1] * jnp.concatenate(pieces, axis=1)
    o_ref[...] = out


def _decode_attention(q, gates, nkv, kc, vc, ks, vs, win, past):
    bsz = q.shape[0]
    n_cmp = kc.shape[1]
    n_past_blk = past // SLC_BLOCK
    cb = min(64, n_past_blk)
    n_blk_pad = -(-(n_past_blk + 1) // LANES) * LANES
    msel = _sel_matrix(n_cmp, n_blk_pad)
    e = (jnp.arange(cb * SLC_BLOCK)[None, :] // SLC_BLOCK == jnp.arange(cb)[:, None]).astype(BF16)
    eg = _gate_expand()
    win_len = win.shape[1]
    per_b = lambda b: (b, 0, 0)
    fix2 = lambda b: (0, 0)
    fix3 = lambda b: (0, 0, 0)
    once = pl.Buffered(1)
    return pl.pallas_call(
        functools.partial(_dec_attn_kernel, past=past, cb=cb),
        grid=(bsz,),
        in_specs=[pl.BlockSpec((None, 1, D_ATT), per_b), pl.BlockSpec((None, 1, LANES), per_b),
                  pl.BlockSpec((None, 1, 6 * KV_WIDTH), per_b),
                  pl.BlockSpec((None, n_cmp, KV_WIDTH), per_b), pl.BlockSpec((None, n_cmp, KV_WIDTH), per_b),
                  pl.BlockSpec((None, past, KV_WIDTH), per_b, pipeline_mode=once),
                  pl.BlockSpec((None, past, KV_WIDTH), per_b, pipeline_mode=once),
                  pl.BlockSpec((None, win_len, 2 * KV_WIDTH), per_b),
                  pl.BlockSpec((n_cmp, n_blk_pad), fix2), pl.BlockSpec((cb, cb * SLC_BLOCK), fix2),
                  pl.BlockSpec((3, LANES, D_ATT), fix3)],
        out_specs=pl.BlockSpec((None, 1, D_ATT), per_b),
        out_shape=jax.ShapeDtypeStruct((bsz, 1, D_ATT), F32),
        compiler_params=_params(("parallel",)), name="nsa_decode",
    )(q, gates, nkv, kc, vc, ks, vs, win, msel, e, eg)


def _top16(s, payload=None):
    n = s.shape[0]
    rid = lax.broadcasted_iota(I32, s.shape, 0).astype(F32)
    vals, picks = [], []
    for _ in range(PEER_TOPK):
        m = jnp.max(s, axis=0, keepdims=True)
        idx = jnp.min(jnp.where(s == m, rid, float(n)), axis=0, keepdims=True)
        hit = rid == idx
        vals.append(m)
        picks.append(idx if payload is None else jnp.sum(jnp.where(hit, payload, 0.0), axis=0, keepdims=True))
        s = jnp.where(hit, NEG_INF, s)
    return jnp.concatenate(vals, axis=0), jnp.concatenate(picks, axis=0)


def _candidate_cells():
    return [(a, b) for a in range(PEER_TOPK) for b in range(PEER_TOPK) if (a + 1) * (b + 1) <= PEER_TOPK]


def _peer_topk_kernel(h_ref, wq_ref, keys_ref, eidx_ref, gw_ref):
    q = _dot(h_ref[...].astype(BF16), wq_ref[...])
    half = PEER_DK // 2
    cells = _candidate_cells()
    for h in range(PEER_HEADS):
        tops = []
        for i in range(2):
            col = (h * 2 + i) * half
            s_t = _dot_nt(keys_ref[h * 2 + i], q[:, col:col + half].astype(BF16))
            tops.append(_top16(s_t))
        (s1, i1), (s2, i2) = tops
        cand = jnp.concatenate([s1[a:a + 1] + s2[b:b + 1] for a, b in cells], axis=0)
        cidx = jnp.concatenate([i1[a:a + 1] * N_KEYS + i2[b:b + 1] for a, b in cells], axis=0)
        top_s, eidx = _top16(cand, cidx)
        e = jnp.exp(top_s - top_s[0:1])
        gw_ref[h * PEER_TOPK:(h + 1) * PEER_TOPK, :] = e / jnp.sum(e, axis=0, keepdims=True)
        eidx_ref[h * PEER_TOPK:(h + 1) * PEER_TOPK, :] = eidx.astype(I32)


def _peer_topk(h, wq, keys, tm):
    n = h.shape[0]
    half = PEER_DK // 2
    return pl.pallas_call(
        _peer_topk_kernel, grid=(n // tm,),
        in_specs=[pl.BlockSpec((tm, D_MODEL), lambda i: (i, 0)),
                  pl.BlockSpec((D_MODEL, PEER_HEADS * PEER_DK), lambda i: (0, 0)),
                  pl.BlockSpec((PEER_HEADS * 2, N_KEYS, half), lambda i: (0, 0, 0))],
        out_specs=[pl.BlockSpec((PEER_ROWS, tm), lambda i: (0, i)),
                   pl.BlockSpec((PEER_ROWS, tm), lambda i: (0, i))],
        out_shape=[jax.ShapeDtypeStruct((PEER_ROWS, n), I32), jax.ShapeDtypeStruct((PEER_ROWS, n), F32)],
        compiler_params=_params(("parallel",)), name="peer_topk")(h, wq, keys)


def _pack_expert_tables(u_tab, v_tab):
    ub = lax.bitcast_convert_type(u_tab.astype(jnp.bfloat16), jnp.uint16).astype(jnp.uint32)
    vb = lax.bitcast_convert_type(v_tab.astype(jnp.bfloat16), jnp.uint16).astype(jnp.uint32)
    words = lax.bitcast_convert_type((vb << 16) | ub, I32)
    return words.reshape(words.shape[0] * PEER_FEAT_TILES, LANES)


def _peer_gather_kernel(idx_ref, gw_ref, x_ref, lng_ref, lnb_ref, uv_hbm, o_ref, idx_smem, *scratch):
    bufs = scratch[:PEER_SLOTS]
    ffn_ref, sem_idx, sem = scratch[PEER_SLOTS:]
    idx_copy = pltpu.make_async_copy(idx_ref, idx_smem, sem_idx)
    idx_copy.start()
    idx_copy.wait()

    tile = PEER_FEAT_TILES

    def issue(t, slot, lo=0, hi=PEER_ROWS):
        for e in range(lo, hi):
            row = pl.multiple_of(idx_smem[t, e], tile)
            copy = pltpu.make_async_copy(uv_hbm.at[pl.ds(row, tile)], bufs[slot].at[pl.ds(e * tile, tile)],
                                         sem.at[slot])
            copy.start(priority=e % 2)

    def wait(slot):
        pltpu.make_async_copy(uv_hbm.at[pl.ds(0, PEER_ROWS * tile)], bufs[slot], sem.at[slot]).wait()

    lane_t = lax.broadcasted_iota(I32, (PEER_ROWS, PEER_TB), 1)

    def feature_block(slot, c):
        return bufs[slot][pl.ds(c, PEER_ROWS, stride=tile), :]

    def token_step(t, slot, t_next, slot_next):
        per_part = PEER_ROWS // (2 * tile)

        def issue_part(k):
            if t_next is not None:
                issue(t_next, slot_next, k * per_part, (k + 1) * per_part)

        x_t = x_ref[pl.ds(t, 1), :]
        acc = None
        for c in range(tile):
            issue_part(c)
            u = lax.bitcast_convert_type(feature_block(slot, c) << 16, F32)
            part = u * x_t[:, c * LANES:(c + 1) * LANES]
            acc = part if acc is None else acc + part
        hidden = jnp.sum(acc, axis=1, keepdims=True)
        g_col = jnp.sum(jnp.where(lane_t == t, gw_ref[...], 0.0), axis=1, keepdims=True)
        coef = g_col * jax.nn.gelu(hidden)
        outs = []
        for c in range(tile):
            issue_part(tile + c)
            v = lax.bitcast_convert_type(feature_block(slot, c) & jnp.int32(-65536), F32)
            outs.append(jnp.sum(coef * v, axis=0, keepdims=True))
        ffn_ref[pl.ds(t, 1), :] = jnp.concatenate(outs, axis=1)

    ahead = PEER_SLOTS - 1
    for t0 in range(ahead):
        issue(t0, t0)

    def group(g, carry):
        for s in range(PEER_SLOTS):
            t = g * PEER_SLOTS + s
            wait(s)
            token_step(t, s, t + ahead, (s + ahead) % PEER_SLOTS)
        return carry

    n_groups = PEER_TB // PEER_SLOTS
    lax.fori_loop(0, n_groups - 1, group, 0)
    for s in range(PEER_SLOTS):
        t = (n_groups - 1) * PEER_SLOTS + s
        wait(s)
        token_step(t, s, t + ahead if t + ahead < PEER_TB else None, (s + ahead) % PEER_SLOTS)
    o_ref[...] = _layer_norm(DN_ALPHA * x_ref[...] + ffn_ref[...], lng_ref[...], lnb_ref[...])


def _peer_gather_ln(eidx, gw_t, x, uv_tab, g, b):
    n = x.shape[0]
    fix = lambda i: (0, 0)
    return pl.pallas_call(
        _peer_gather_kernel, grid=(n // PEER_TB,),
        in_specs=[pl.BlockSpec((PEER_TB, PEER_ROWS), lambda i: (i, 0)),
                  pl.BlockSpec((PEER_ROWS, PEER_TB), lambda i: (0, i)),
                  pl.BlockSpec((PEER_TB, D_MODEL), lambda i: (i, 0)),
                  pl.BlockSpec((1, D_MODEL), fix), pl.BlockSpec((1, D_MODEL), fix),
                  pl.BlockSpec(memory_space=pl.ANY)],
        out_specs=pl.BlockSpec((PEER_TB, D_MODEL), lambda i: (i, 0)),
        out_shape=jax.ShapeDtypeStruct((n, D_MODEL), F32),
        scratch_shapes=[pltpu.SMEM((PEER_TB, PEER_ROWS), I32)] +
                       [pltpu.VMEM((PEER_ROWS * PEER_FEAT_TILES, LANES), I32) for _ in range(PEER_SLOTS)] +
                       [pltpu.VMEM((PEER_TB, D_MODEL), F32),
                        pltpu.SemaphoreType.DMA(()),
                        pltpu.SemaphoreType.DMA((PEER_SLOTS,))],
        compiler_params=_params(("arbitrary",)), name="peer_gather",
    )(eidx, gw_t, x, g, b, uv_tab)


def _rope_tables(pos):
    half = HEAD_DIM // 2
    inv = ROPE_THETA ** (-jnp.arange(half, dtype=F32) / half)
    ang = pos.astype(F32)[:, None] * inv
    cos, sin = jnp.cos(ang), jnp.sin(ang)
    return jnp.tile(cos, (1, 4)), jnp.tile(jnp.concatenate([-sin, sin], axis=1), (1, 2))


def _layer_weights(p):
    w_in = p['w_in']
    w_u = w_in[:, :D_SSM].astype(BF16)
    pad = PROJ_W - (w_in.shape[1] - D_SSM)
    w_rest = jnp.pad(w_in[:, D_SSM:], ((0, 0), (0, pad))).astype(BF16)
    cmp_k = _compress_weights(p['cmp_pe'][0], p['cmp_w1'][0], p['cmp_w2'][0])
    cmp_v = _compress_weights(p['cmp_pe'][1], p['cmp_w1'][1], p['cmp_w2'][1])
    return dict(
        w_u=w_u, w_rest=w_rest, w_glu=p['w_glu'].astype(BF16), d=p['d'].reshape(1, D_SSM),
        wo_ssm=p['w_out'][:D_SSM].astype(BF16), wo_att=p['w_out'][D_SSM:].astype(BF16),
        ln1_g=p['ln1_g'].reshape(1, D_MODEL), ln1_b=p['ln1_b'].reshape(1, D_MODEL),
        ln2_g=p['ln2_g'].reshape(1, D_MODEL), ln2_b=p['ln2_b'].reshape(1, D_MODEL),
        peer_wq=p['peer_wq'].astype(BF16),
        peer_keys=p['peer_keys'].reshape(PEER_HEADS * 2, N_KEYS, PEER_DK // 2).astype(BF16),
        peer_uv=_pack_expert_tables(p['peer_u'], p['peer_v']), cmp_k=cmp_k, cmp_v=cmp_v)


def _token_tail(x, y_ssm, y_att, w, tm):
    h = _outproj_ln(y_ssm, y_att, x, w['wo_ssm'], w['wo_att'], w['ln1_g'], w['ln1_b'], tm)
    eidx_t, gw_t = _peer_topk(h, w['peer_wq'], w['peer_keys'], min(tm, 256))
    rows = eidx_t.T * PEER_FEAT_TILES
    return _peer_gather_ln(rows, gw_t, h, w['peer_uv'], w['ln2_g'], w['ln2_b'])


def _prompt_layer(x, bsz, seq, w, p):
    n = bsz * seq
    tm = min(512, seq)
    cos, sin = _rope_tables(jnp.arange(seq, dtype=I32))
    q, kv, gates = _project(x, w['w_rest'], cos, sin, tm)
    x_tb = x.reshape(bsz, seq, D_MODEL).transpose(1, 0, 2).reshape(n, D_MODEL)
    u_tb = _matmul(x_tb, w['w_u'], tm)
    wb, wc, ar, ai = _s5_weights(p['a_re'], p['a_im'], p['log_dt'], p['b_re'], p['b_im'], p['c_re'], p['c_im'], bsz)
    s0 = jnp.zeros((bsz, 2 * N_STATE), F32)
    gy_tb, s_fin = _s5_scan(u_tb, wb, wc, ar, ai, w['d'], s0, min(64, seq))
    y_ssm_tb = _glu(gy_tb, w['w_glu'], tm)
    y_ssm = y_ssm_tb.reshape(seq, bsz, D_SSM).transpose(1, 0, 2).reshape(n, D_SSM)
    n_cmp = seq // CMP_STRIDE
    xk = kv[:, 0:KV_WIDTH].reshape(bsz, n_cmp, CMP_STRIDE * KV_WIDTH)
    xv = kv[:, KV_WIDTH:2 * KV_WIDTH].reshape(bsz, n_cmp, CMP_STRIDE * KV_WIDTH)
    kc = _compress(xk, *w['cmp_k'])
    vc = _compress(xv, *w['cmp_v'])
    y_att = _attention(q, gates, kv, kc, vc, bsz, seq)
    y = _token_tail(x, y_ssm, y_att, w, tm)
    new_kv = kv[:, :4 * KV_WIDTH].reshape(bsz, seq, 4, N_KV_HEADS, HEAD_DIM)
    new_win = kv[:, 4 * KV_WIDTH:].reshape(bsz, seq, 2, N_KV_HEADS, HEAD_DIM)
    win = jnp.concatenate([jnp.zeros((bsz, WINDOW, 2, N_KV_HEADS, HEAD_DIM), F32), new_win], axis=1)[:, -WINDOW:]
    return y, new_kv, win, s_fin.reshape(bsz, 2, N_GROUPS, SSM_STATE)


def _sample_layer(x, bsz, past, cache, page_table, layer, win_past, ssm_state, w, p):
    rows = x.shape[0]
    cos, sin = _rope_tables(jnp.full((rows,), past, I32))
    q, kv, gates = _project(x, w['w_rest'], cos, sin, rows)
    u = _matmul(x, w['w_u'], rows)
    wb, wc, ar, ai = _s5_weights(p['a_re'], p['a_im'], p['log_dt'], p['b_re'], p['b_im'], p['c_re'], p['c_im'], bsz)
    gy, s_fin = _s5_scan(u[:bsz], wb, wc, ar, ai, w['d'], ssm_state.reshape(bsz, 2 * N_STATE), 1)
    y_ssm = jnp.pad(_glu(gy, w['w_glu'], bsz), ((0, rows - bsz), (0, 0)))
    xk, xv, ks, vs = _page_gather(cache, page_table, layer)
    kc = _compress(xk, *w['cmp_k'])
    vc = _compress(xv, *w['cmp_v'])
    win_len = win_past.shape[1]
    win_rows = win_past.reshape(bsz, win_len, 2 * KV_WIDTH)
    y_att = _decode_attention(q[:bsz].reshape(bsz, 1, D_ATT), gates[:bsz].reshape(bsz, 1, LANES),
                              kv[:bsz].reshape(bsz, 1, 6 * KV_WIDTH), kc, vc, ks, vs, win_rows, past)
    y_att = jnp.pad(y_att.reshape(bsz, D_ATT), ((0, rows - bsz), (0, 0)))
    y = _token_tail(x, y_ssm, y_att, w, rows)
    new_kv = kv[:bsz, :4 * KV_WIDTH].reshape(bsz, 1, 4, N_KV_HEADS, HEAD_DIM)
    new_win = kv[:bsz, 4 * KV_WIDTH:].reshape(bsz, 1, 2, N_KV_HEADS, HEAD_DIM)
    win = jnp.concatenate([win_past, new_win], axis=1)[:, -win_len:]
    return y, new_kv, win, s_fin.reshape(bsz, 2, N_GROUPS, SSM_STATE)


def kernel(x_prompt, x_sample, cache_kv, cache_win, state_ssm, page_table, w_in, ssm_a_re, ssm_a_im, ssm_log_dt, ssm_b_re, ssm_b_im, ssm_c_re, ssm_c_im, ssm_d, w_glu, cmp_pe, cmp_w1, cmp_w2, w_out, ln1_g, ln1_b, peer_wq, peer_keys, peer_u, peer_v, ln2_g, ln2_b):
    bsz, seq, _ = x_prompt.shape
    dec_bsz = x_sample.shape[0]
    depth = w_in.shape[0]
    past = page_table.shape[1] * PAGE_SIZE
    cache = jnp.transpose(cache_kv, (0, 1, 3, 4, 5, 2))
    dec_rows = -(-dec_bsz // PEER_TB) * PEER_TB
    y_p = x_prompt.reshape(bsz * seq, D_MODEL)
    y_s = jnp.pad(x_sample.reshape(dec_bsz, D_MODEL), ((0, dec_rows - dec_bsz), (0, 0)))
    outs = [[] for _ in range(6)]
    for l in range(depth):
        p = {'w_in': w_in[l], 'a_re': ssm_a_re[l], 'a_im': ssm_a_im[l], 'log_dt': ssm_log_dt[l],
             'b_re': ssm_b_re[l], 'b_im': ssm_b_im[l], 'c_re': ssm_c_re[l], 'c_im': ssm_c_im[l],
             'd': ssm_d[l], 'w_glu': w_glu[l], 'cmp_pe': cmp_pe[l], 'cmp_w1': cmp_w1[l], 'cmp_w2': cmp_w2[l],
             'w_out': w_out[l], 'ln1_g': ln1_g[l], 'ln1_b': ln1_b[l], 'peer_wq': peer_wq[l],
             'peer_keys': peer_keys[l], 'peer_u': peer_u[l], 'peer_v': peer_v[l],
             'ln2_g': ln2_g[l], 'ln2_b': ln2_b[l]}
        w = _layer_weights(p)
        y_s, kvn, winn, sn = _sample_layer(y_s, dec_bsz, past, cache, page_table, l, cache_win[l], state_ssm[l], w, p)
        outs[3].append(kvn)
        outs[4].append(winn)
        outs[5].append(sn)
        y_p, kvn, winn, sn = _prompt_layer(y_p, bsz, seq, w, p)
        outs[0].append(kvn)
        outs[1].append(winn)
        outs[2].append(sn)
    return (y_p.reshape(bsz, seq, D_MODEL), y_s[:dec_bsz].reshape(dec_bsz, 1, D_MODEL),
            jnp.stack(outs[0]), jnp.stack(outs[1]), jnp.stack(outs[2]),
            jnp.stack(outs[3]), jnp.stack(outs[4]), jnp.stack(outs[5]))
```

```python
import functools
import math

import jax
import jax.numpy as jnp
from jax import lax
from jax.experimental import pallas as pl
from jax.experimental.pallas import tpu as pltpu

F32 = jnp.float32
BF16 = jnp.bfloat16
I32 = jnp.int32

D_MODEL = 1024
D_SSM = 512
D_ATT = 512
SSM_CH = 16
N_GROUPS = 32
SSM_STATE = 64
N_STATE = N_GROUPS * SSM_STATE
HEAD_DIM = 64
N_HEADS = 8
N_KV_HEADS = 2
GQA = 4
KV_WIDTH = 128
CMP_STRIDE = 16
CMP_LEN = 32
SLC_BLOCK = 64
N_SLC = 16
WINDOW = 512
Q_BLOCK = 64
ROPE_THETA = 10000.0
PEER_HEADS = 8
PEER_DK = 256
N_KEYS = 128
PEER_TOPK = 16
PAGE_SIZE = 128
DN_ALPHA = 4 ** 0.25
LN_EPS = 1e-5
ATT_SCALE = HEAD_DIM ** -0.5
TINY = float(jnp.finfo(jnp.float32).tiny)
NEG_INF = float("-inf")
POS_INF = float("inf")

LANES = 128
PROJ_W = 512 + 6 * KV_WIDTH + LANES
VMEM_LIMIT = 56 * 1024 * 1024
PEER_TB = 256
PEER_SLOTS = 8
PEER_ROWS = PEER_HEADS * PEER_TOPK
PEER_FEAT_TILES = D_MODEL // LANES
PAGES_PER_STEP = 4


def _params(sem, **kw):
    return pltpu.CompilerParams(dimension_semantics=sem, vmem_limit_bytes=VMEM_LIMIT, **kw)


def _dot(a, b):
    return jnp.dot(a, b, preferred_element_type=F32)


def _dot_nt(a, b):
    return lax.dot_general(a, b, (((1,), (1,)), ((), ())), preferred_element_type=F32)


def _dot_exact(a, b):
    bb = b.astype(BF16)
    hi = a.astype(BF16)
    rest = a - hi.astype(F32)
    mid = rest.astype(BF16)
    lo = (rest - mid.astype(F32)).astype(BF16)
    return _dot(hi, bb) + _dot(mid, bb) + _dot(lo, bb)


def _layer_norm(x, g, b):
    mu = jnp.mean(x, axis=-1, keepdims=True)
    xc = x - mu
    var = jnp.mean(xc * xc, axis=-1, keepdims=True)
    return xc * lax.rsqrt(var + LN_EPS) * g + b


def _masked_softmax(s, valid):
    s = jnp.where(valid, s, NEG_INF)
    m = jnp.max(s, axis=-1, keepdims=True)
    m = jnp.where(m > NEG_INF, m, 0.0)
    e = jnp.exp(s - m)
    den = jnp.maximum(jnp.sum(e, axis=-1, keepdims=True), TINY)
    return e / den


def _mm_kernel(x_ref, w_ref, o_ref):
    o_ref[...] = _dot(x_ref[...].astype(BF16), w_ref[...])


def _matmul(x, w, tm):
    n, k = x.shape
    m = w.shape[1]
    return pl.pallas_call(
        _mm_kernel, grid=(n // tm,),
        in_specs=[pl.BlockSpec((tm, k), lambda i: (i, 0)), pl.BlockSpec((k, m), lambda i: (0, 0))],
        out_specs=pl.BlockSpec((tm, m), lambda i: (i, 0)),
        out_shape=jax.ShapeDtypeStruct((n, m), F32),
        compiler_params=_params(("parallel",)), name="mm")(x, w)


def _proj_kernel(x_ref, w_ref, cos_ref, sin_ref, q_ref, kv_ref, g_ref):
    acc = _dot(x_ref[...].astype(BF16), w_ref[...])
    cos = cos_ref[...]
    sin = sin_ref[...]
    lane = lax.broadcasted_iota(I32, cos.shape, 1)
    first_half = (lane % HEAD_DIM) < (HEAD_DIM // 2)

    def rope(v):
        rot = jnp.where(first_half, pltpu.roll(v, 96, 1), pltpu.roll(v, 32, 1))
        return v * cos + rot * sin

    for j in range(4):
        q_ref[:, j * LANES:(j + 1) * LANES] = rope(acc[:, j * LANES:(j + 1) * LANES])
    for j in range(6):
        blk = acc[:, 512 + j * LANES:512 + (j + 1) * LANES]
        kv_ref[:, j * LANES:(j + 1) * LANES] = rope(blk) if j % 2 == 0 else blk
    g_ref[...] = jax.nn.sigmoid(acc[:, 512 + 6 * LANES:])


def _project(x, w, cos, sin, tm):
    n = x.shape[0]
    tab_blocks = cos.shape[0] // tm
    return pl.pallas_call(
        _proj_kernel, grid=(n // tm,),
        in_specs=[pl.BlockSpec((tm, D_MODEL), lambda i: (i, 0)),
                  pl.BlockSpec((D_MODEL, PROJ_W), lambda i: (0, 0)),
                  pl.BlockSpec((tm, LANES), lambda i: (i % tab_blocks, 0)),
                  pl.BlockSpec((tm, LANES), lambda i: (i % tab_blocks, 0))],
        out_specs=[pl.BlockSpec((tm, 512), lambda i: (i, 0)),
                   pl.BlockSpec((tm, 6 * KV_WIDTH), lambda i: (i, 0)),
                   pl.BlockSpec((tm, LANES), lambda i: (i, 0))],
        out_shape=[jax.ShapeDtypeStruct((n, 512), F32),
                   jax.ShapeDtypeStruct((n, 6 * KV_WIDTH), F32),
                   jax.ShapeDtypeStruct((n, LANES), F32)],
        compiler_params=_params(("parallel",)), name="proj")(x, w, cos, sin)


def _glu_kernel(a_ref, w_ref, o_ref):
    gl = _dot(a_ref[...].astype(BF16), w_ref[...])
    o_ref[...] = gl[:, :D_SSM] * jax.nn.sigmoid(gl[:, D_SSM:])


def _glu(a, w, tm):
    n = a.shape[0]
    return pl.pallas_call(
        _glu_kernel, grid=(n // tm,),
        in_specs=[pl.BlockSpec((tm, D_SSM), lambda i: (i, 0)),
                  pl.BlockSpec((D_SSM, 2 * D_SSM), lambda i: (0, 0))],
        out_specs=pl.BlockSpec((tm, D_SSM), lambda i: (i, 0)),
        out_shape=jax.ShapeDtypeStruct((n, D_SSM), F32),
        compiler_params=_params(("parallel",)), name="glu")(a, w)


def _outproj_kernel(ys_ref, ya_ref, x_ref, w1_ref, w2_ref, g_ref, b_ref, o_ref):
    mix = _dot(ys_ref[...].astype(BF16), w1_ref[...]) + _dot(ya_ref[...].astype(BF16), w2_ref[...])
    o_ref[...] = _layer_norm(DN_ALPHA * x_ref[...] + mix, g_ref[...], b_ref[...])


def _outproj_ln(ys, ya, x, w1, w2, g, b, tm):
    n = x.shape[0]
    row = lambda i: (i, 0)
    fix = lambda i: (0, 0)
    return pl.pallas_call(
        _outproj_kernel, grid=(n // tm,),
        in_specs=[pl.BlockSpec((tm, D_SSM), row), pl.BlockSpec((tm, D_ATT), row),
                  pl.BlockSpec((tm, D_MODEL), row),
                  pl.BlockSpec((D_SSM, D_MODEL), fix), pl.BlockSpec((D_ATT, D_MODEL), fix),
                  pl.BlockSpec((1, D_MODEL), fix), pl.BlockSpec((1, D_MODEL), fix)],
        out_specs=pl.BlockSpec((tm, D_MODEL), row),
        out_shape=jax.ShapeDtypeStruct((n, D_MODEL), F32),
        compiler_params=_params(("parallel",)), name="outproj_ln")(ys, ya, x, w1, w2, g, b)


def _s5_kernel(u_ref, wb_ref, wc_ref, ar_ref, ai_ref, d_ref, s0_ref, gy_ref, sf_ref, bu_ref, st_ref,
               *, n_steps, batch):
    @pl.when(pl.program_id(0) == 0)
    def _():
        st_ref[...] = s0_ref[...]

    u = u_ref[...]
    bu_ref[...] = _dot(u.astype(BF16), wb_ref[...])
    n_chunk = 4
    cw = N_STATE // n_chunk

    def step(t, carry):
        r = pl.multiple_of(t * batch, batch)
        new = []
        for c in range(n_chunk):
            sr, si = carry[c], carry[n_chunk + c]
            ar = ar_ref[:, c * cw:(c + 1) * cw]
            ai = ai_ref[:, c * cw:(c + 1) * cw]
            nr = ar * sr - ai * si + bu_ref[pl.ds(r, batch), c * cw:(c + 1) * cw]
            ni = ar * si + ai * sr + bu_ref[pl.ds(r, batch), N_STATE + c * cw:N_STATE + (c + 1) * cw]
            bu_ref[pl.ds(r, batch), c * cw:(c + 1) * cw] = nr
            bu_ref[pl.ds(r, batch), N_STATE + c * cw:N_STATE + (c + 1) * cw] = ni
            new.append((nr, ni))
        return tuple(p[0] for p in new) + tuple(p[1] for p in new)

    init = tuple(st_ref[:, c * cw:(c + 1) * cw] for c in range(n_chunk)) + \
        tuple(st_ref[:, N_STATE + c * cw:N_STATE + (c + 1) * cw] for c in range(n_chunk))
    fin = lax.fori_loop(0, n_steps, step, init)
    for c in range(n_chunk):
        st_ref[:, c * cw:(c + 1) * cw] = fin[c]
        st_ref[:, N_STATE + c * cw:N_STATE + (c + 1) * cw] = fin[n_chunk + c]
    y = _dot(bu_ref[...].astype(BF16), wc_ref[...]) + d_ref[...] * u
    gy_ref[...] = jax.nn.gelu(y)
    sf_ref[...] = st_ref[...]


def _s5_scan(u_tb, wb, wc, ar, ai, d, s0, n_steps):
    batch = s0.shape[0]
    n = u_tb.shape[0]
    rows = n_steps * batch
    fix = lambda i: (0, 0)
    return pl.pallas_call(
        functools.partial(_s5_kernel, n_steps=n_steps, batch=batch),
        grid=(n // rows,),
        in_specs=[pl.BlockSpec((rows, D_SSM), lambda i: (i, 0)),
                  pl.BlockSpec((D_SSM, 2 * N_STATE), fix), pl.BlockSpec((2 * N_STATE, D_SSM), fix),
                  pl.BlockSpec((batch, N_STATE), fix), pl.BlockSpec((batch, N_STATE), fix),
                  pl.BlockSpec((1, D_SSM), fix), pl.BlockSpec((batch, 2 * N_STATE), fix)],
        out_specs=[pl.BlockSpec((rows, D_SSM), lambda i: (i, 0)),
                   pl.BlockSpec((batch, 2 * N_STATE), fix)],
        out_shape=[jax.ShapeDtypeStruct((n, D_SSM), F32),
                   jax.ShapeDtypeStruct((batch, 2 * N_STATE), F32)],
        scratch_shapes=[pltpu.VMEM((rows, 2 * N_STATE), F32), pltpu.VMEM((batch, 2 * N_STATE), F32)],
        compiler_params=_params(("arbitrary",)), name="s5_scan")(u_tb, wb, wc, ar, ai, d, s0)


def _s5_weights(a_re, a_im, log_dt, b_re, b_im, c_re, c_im, batch):
    dt = jnp.exp(log_dt)[:, None]
    mag = jnp.exp(dt * a_re)
    abar_re, abar_im = mag * jnp.cos(dt * a_im), mag * jnp.sin(dt * a_im)
    den = a_re * a_re + a_im * a_im
    f_re = ((abar_re - 1.0) * a_re + abar_im * a_im) / den
    f_im = (abar_im * a_re - (abar_re - 1.0) * a_im) / den
    bb_re = f_re[..., None] * b_re - f_im[..., None] * b_im
    bb_im = f_re[..., None] * b_im + f_im[..., None] * b_re
    eye = jnp.eye(N_GROUPS, dtype=F32)
    wb_re = jnp.einsum('gpc,gh->gchp', bb_re, eye).reshape(D_SSM, N_STATE)
    wb_im = jnp.einsum('gpc,gh->gchp', bb_im, eye).reshape(D_SSM, N_STATE)
    wb = jnp.concatenate([wb_re, wb_im], axis=1).astype(BF16)
    wc_re = jnp.einsum('gcp,gh->gphc', c_re, eye).reshape(N_STATE, D_SSM)
    wc_im = jnp.einsum('gcp,gh->gphc', c_im, eye).reshape(N_STATE, D_SSM)
    wc = jnp.concatenate([wc_re, -wc_im], axis=0).astype(BF16)
    ar = jnp.broadcast_to(abar_re.reshape(1, N_STATE), (batch, N_STATE))
    ai = jnp.broadcast_to(abar_im.reshape(1, N_STATE), (batch, N_STATE))
    return wb, wc, ar, ai


def _compress_kernel(x_ref, pe_ref, w1a_ref, w1b_ref, w2_ref, o_ref):
    x = x_ref[...]
    n = x.shape[0]
    first = _dot((x + pe_ref[0:1, :]).astype(BF16), w1a_ref[...])
    second = _dot((x + pe_ref[1:2, :]).astype(BF16), w1b_ref[...])
    nxt = pltpu.roll(second, n - 1, 0)
    o_ref[...] = _dot(jax.nn.gelu(first + nxt).astype(BF16), w2_ref[...])


def _compress(x, pe, w1a, w1b, w2):
    bsz, n, width = x.shape
    fix = lambda b: (0, 0)
    return pl.pallas_call(
        _compress_kernel, grid=(bsz,),
        in_specs=[pl.BlockSpec((None, n, width), lambda b: (b, 0, 0)),
                  pl.BlockSpec((2, width), fix), pl.BlockSpec((width, KV_WIDTH), fix),
                  pl.BlockSpec((width, KV_WIDTH), fix), pl.BlockSpec((KV_WIDTH, KV_WIDTH), fix)],
        out_specs=pl.BlockSpec((None, n, KV_WIDTH), lambda b: (b, 0, 0)),
        out_shape=jax.ShapeDtypeStruct((bsz, n, KV_WIDTH), F32),
        compiler_params=_params(("parallel",)), name="compress")(x, pe, w1a, w1b, w2)


def _compress_weights(pe, w1, w2):
    eye = jnp.eye(N_KV_HEADS, dtype=F32)
    w1 = w1.reshape(2, CMP_STRIDE, HEAD_DIM, HEAD_DIM)
    pe = pe.reshape(2, CMP_STRIDE, HEAD_DIM)
    w1a = jnp.einsum('jde,hk->jhdke', w1[0], eye).reshape(CMP_STRIDE * KV_WIDTH, KV_WIDTH).astype(BF16)
    w1b = jnp.einsum('jde,hk->jhdke', w1[1], eye).reshape(CMP_STRIDE * KV_WIDTH, KV_WIDTH).astype(BF16)
    w2d = jnp.einsum('de,hk->hdke', w2, eye).reshape(KV_WIDTH, KV_WIDTH).astype(BF16)
    pe2 = jnp.broadcast_to(pe[:, :, None, :], (2, CMP_STRIDE, N_KV_HEADS, HEAD_DIM)).reshape(2, CMP_STRIDE * KV_WIDTH)
    return pe2, w1a, w1b, w2d


def _sel_matrix(n_cmp_rows, n_blk_cols):
    n = jnp.arange(n_cmp_rows)[:, None]
    j = jnp.arange(n_blk_cols)[None, :]
    per = SLC_BLOCK // CMP_STRIDE
    m = (n // per == j).astype(F32) + ((n + 1) // per == j).astype(F32)
    return jnp.where(n < n_cmp_rows - 1, m, 0.0)


def _block_ranks(score, n_blk, last_live=None):
    blk = lax.broadcasted_iota(I32, score.shape, 1)
    segment = 16

    def add_segment(rank, start):
        for i in range(start, min(start + segment, n_blk)):
            col = score[:, i:i + 1]
            before = (col > score) | ((col == score) & (blk > i))
            rank = rank + before.astype(I32)
        return rank

    rank = jnp.zeros(score.shape, I32)
    for start in range(0, n_blk, segment):
        if last_live is None or start == 0:
            rank = add_segment(rank, start)
        else:
            rank = lax.cond(start <= last_live, functools.partial(add_segment, start=start), lambda r: r, rank)
    return rank


def _attn_kernel(q_ref, g_ref, kc_ref, vc_ref, ks_ref, vs_ref, kw_ref, vw_ref, msel_ref, e3_ref, eg_ref,
                 o_ref, s_ref, *, seq, ck, wk):
    qb = pl.program_id(1)
    q0 = qb * Q_BLOCK
    n_cmp = seq // CMP_STRIDE
    n_blk = seq // SLC_BLOCK
    n_sel = min(N_SLC, n_blk)
    rows = GQA * Q_BLOCK
    pos = q0 + lax.broadcasted_iota(I32, (Q_BLOCK, 1), 0)
    pos4 = jnp.concatenate([pos] * GQA, axis=0)
    lane_r = lax.broadcasted_iota(I32, (rows, LANES), 1)
    kc = kc_ref[...].astype(BF16)
    vc = vc_ref[...].astype(BF16)
    n_chunks = (q0 + Q_BLOCK + ck - 1) // ck
    kstart = pl.multiple_of(jnp.maximum(q0 + Q_BLOCK - wk, 0), Q_BLOCK)
    kw = kw_ref[pl.ds(kstart, wk), :].astype(BF16)
    vw = vw_ref[pl.ds(kstart, wk), :].astype(BF16)
    q_heads, out_c, sels = [], [], []
    for hkv in range(N_KV_HEADS):
        parts = []
        for g in range(GQA):
            hd = hkv * GQA + g
            slab = q_ref[:, (hd // 2) * LANES:(hd // 2 + 1) * LANES]
            if hd % 2 != hkv:
                slab = pltpu.roll(slab, HEAD_DIM, 1)
            parts.append(slab)
        qh = jnp.concatenate(parts, axis=0)
        qh = jnp.where(lane_r // HEAD_DIM == hkv, qh * ATT_SCALE, 0.0).astype(BF16)

        s_c = _dot_nt(qh, kc)
        n_id = lax.broadcasted_iota(I32, s_c.shape, 1)
        p_c = _masked_softmax(s_c, n_id * CMP_STRIDE + (CMP_LEN - 1) <= pos4)
        o_c = _dot(p_c.astype(BF16), vc)
        p_grp = p_c[0:Q_BLOCK]
        for g in range(1, GQA):
            p_grp = p_grp + p_c[g * Q_BLOCK:(g + 1) * Q_BLOCK]
        p_slc = _dot_exact(p_grp, msel_ref[...])

        blk = lax.broadcasted_iota(I32, p_slc.shape, 1)
        cur = pos // SLC_BLOCK
        forced = (blk == 0) | (blk == cur) | (blk == cur - 1)
        future = blk * SLC_BLOCK > pos
        score = jnp.where(future, NEG_INF, jnp.where(forced, POS_INF, p_slc))
        q_heads.append(qh)
        out_c.append(o_c)
        sels.append((_block_ranks(score, n_blk, qb) < n_sel).astype(BF16))

    n_chunk_total = seq // ck
    folds = ck // LANES

    def fold(x, op):
        out = x[:, 0:LANES]
        for f in range(1, folds):
            out = op(out, x[:, f * LANES:(f + 1) * LANES])
        return out

    def score_chunk(c, carry):
        base = pl.multiple_of(c * ck, ck)
        k = ks_ref[pl.ds(base, ck), :].astype(BF16)
        kpos = base + lax.broadcasted_iota(I32, (Q_BLOCK, ck), 1)
        causal = kpos <= pos
        new = []
        for hkv in range(N_KV_HEADS):
            s = _dot_nt(q_heads[hkv], k)
            ok = (_dot(sels[hkv], e3_ref[c]) > 0.5) & causal
            s = jnp.where(jnp.concatenate([ok] * GQA, axis=0), s, NEG_INF)
            s_ref[hkv * n_chunk_total + c] = s
            new.append(jnp.maximum(carry[hkv], fold(s, jnp.maximum)))
        return tuple(new)

    neg = jnp.full((rows, LANES), NEG_INF, F32)
    folded_max = lax.fori_loop(0, n_chunks, score_chunk, (neg, neg))
    row_max = []
    for hkv in range(N_KV_HEADS):
        m = jnp.max(folded_max[hkv], axis=-1, keepdims=True)
        row_max.append(jnp.where(m > NEG_INF, m, 0.0))

    def prob_chunk(c, carry):
        base = pl.multiple_of(c * ck, ck)
        v = vs_ref[pl.ds(base, ck), :].astype(BF16)
        new = []
        for hkv in range(N_KV_HEADS):
            l, acc = carry[hkv]
            e = jnp.exp(s_ref[hkv * n_chunk_total + c] - row_max[hkv])
            new.append((l + fold(e, jnp.add), acc + _dot(e.astype(BF16), v)))
        return tuple(new)

    zero = jnp.zeros((rows, LANES), F32)
    sel_state = lax.fori_loop(0, n_chunks, prob_chunk, ((zero, zero), (zero, zero)))

    heads_c, heads_s, heads_w = [], [], []
    for hkv in range(N_KV_HEADS):
        l_s, acc_s = sel_state[hkv]
        o_s = acc_s / jnp.maximum(jnp.sum(l_s, axis=-1, keepdims=True), TINY)
        o_c = out_c[hkv]

        s_w = _dot_nt(q_heads[hkv], kw)
        dpos = pos4 - (kstart + lax.broadcasted_iota(I32, s_w.shape, 1))
        p_w = _masked_softmax(s_w, (dpos >= 0) & (dpos < WINDOW))
        o_w = _dot(p_w.astype(BF16), vw)

        for g in range(GQA):
            sl = (slice(g * Q_BLOCK, (g + 1) * Q_BLOCK), slice(hkv * HEAD_DIM, (hkv + 1) * HEAD_DIM))
            heads_c.append(o_c[sl])
            heads_s.append(o_s[sl])
            heads_w.append(o_w[sl])

    gates = g_ref[...]
    out = jnp.zeros((Q_BLOCK, D_ATT), F32)
    for t, heads in enumerate((heads_c, heads_s, heads_w)):
        out = out + _dot_exact(gates, eg_ref[t]) * jnp.concatenate(heads, axis=1)
    o_ref[...] = out


def _gate_expand():
    c = jnp.arange(LANES)[None, :, None]
    t = jnp.arange(3)[:, None, None]
    h = (jnp.arange(D_ATT) // HEAD_DIM)[None, None, :]
    return (c == t * N_HEADS + h).astype(F32)


def _attention(q, gates, kv, kc, vc, bsz, seq):
    ck = min(512, seq)
    wk = min(WINDOW + Q_BLOCK, seq)
    n_qb = seq // Q_BLOCK
    n_cmp = seq // CMP_STRIDE
    n_blk = seq // SLC_BLOCK
    msel = _sel_matrix(n_cmp, n_blk)
    key_blk = (jnp.arange(seq) // SLC_BLOCK).reshape(seq // ck, 1, ck)
    e3 = (key_blk == jnp.arange(n_blk)[None, :, None]).astype(BF16)
    eg = _gate_expand()
    fix2 = lambda b, i: (0, 0)
    fix3 = lambda b, i: (0, 0, 0)
    kv_spec = lambda col: pl.BlockSpec((seq, KV_WIDTH), lambda b, i: (b, col))
    return pl.pallas_call(
        functools.partial(_attn_kernel, seq=seq, ck=ck, wk=wk),
        grid=(bsz, n_qb),
        in_specs=[pl.BlockSpec((Q_BLOCK, D_ATT), lambda b, i: (b * n_qb + i, 0)),
                  pl.BlockSpec((Q_BLOCK, LANES), lambda b, i: (b * n_qb + i, 0)),
                  pl.BlockSpec((None, n_cmp, KV_WIDTH), lambda b, i: (b, 0, 0)),
                  pl.BlockSpec((None, n_cmp, KV_WIDTH), lambda b, i: (b, 0, 0)),
                  kv_spec(2), kv_spec(3), kv_spec(4), kv_spec(5),
                  pl.BlockSpec((n_cmp, n_blk), fix2),
                  pl.BlockSpec((seq // ck, n_blk, ck), fix3),
                  pl.BlockSpec((3, LANES, D_ATT), fix3)],
        out_specs=pl.BlockSpec((Q_BLOCK, D_ATT), lambda b, i: (b * n_qb + i, 0)),
        out_shape=jax.ShapeDtypeStruct((bsz * seq, D_ATT), F32),
        scratch_shapes=[pltpu.VMEM((N_KV_HEADS * (seq // ck), GQA * Q_BLOCK, ck), F32)],
        compiler_params=_params(("parallel", "arbitrary")), name="nsa_prompt",
    )(q, gates, kc, vc, kv, kv, kv, kv, msel, e3, eg)


def _page_gather_kernel(pt_ref, *refs):
    del pt_ref
    pages = refs[:PAGES_PER_STEP]
    xk_ref, xv_ref, ks_ref, vs_ref, kc_tmp, vc_tmp = refs[PAGES_PER_STEP:]
    per_page = PAGE_SIZE // CMP_STRIDE
    for i, page_ref in enumerate(pages):

        def both_heads(kind):
            return jnp.concatenate([page_ref[kind, h] for h in range(N_KV_HEADS)], axis=0).T

        kc_tmp[...] = both_heads(0)
        vc_tmp[...] = both_heads(1)
        rows = slice(i * per_page, (i + 1) * per_page)
        for j in range(CMP_STRIDE):
            cols = slice(j * KV_WIDTH, (j + 1) * KV_WIDTH)
            xk_ref[rows, cols] = kc_tmp[pl.ds(j, per_page, stride=CMP_STRIDE), :]
            xv_ref[rows, cols] = vc_tmp[pl.ds(j, per_page, stride=CMP_STRIDE), :]
        ks_ref[i * PAGE_SIZE:(i + 1) * PAGE_SIZE, :] = both_heads(2)
        vs_ref[i * PAGE_SIZE:(i + 1) * PAGE_SIZE, :] = both_heads(3)


def _page_gather(cache, page_table, layer):
    bsz, n_pages = page_table.shape
    past = n_pages * PAGE_SIZE
    per_step = PAGES_PER_STEP * (PAGE_SIZE // CMP_STRIDE)
    width = CMP_STRIDE * KV_WIDTH

    def page_spec(i):
        return pl.BlockSpec((None, None, 4, N_KV_HEADS, HEAD_DIM, PAGE_SIZE),
                            lambda b, p, pt: (layer, pt[b, p * PAGES_PER_STEP + i], 0, 0, 0, 0))

    grid_spec = pltpu.PrefetchScalarGridSpec(
        num_scalar_prefetch=1, grid=(bsz, n_pages // PAGES_PER_STEP),
        in_specs=[page_spec(i) for i in range(PAGES_PER_STEP)],
        scratch_shapes=[pltpu.VMEM((PAGE_SIZE, KV_WIDTH), F32), pltpu.VMEM((PAGE_SIZE, KV_WIDTH), F32)],
        out_specs=[pl.BlockSpec((None, per_step, width), lambda b, p, pt: (b, p, 0)),
                   pl.BlockSpec((None, per_step, width), lambda b, p, pt: (b, p, 0)),
                   pl.BlockSpec((None, PAGES_PER_STEP * PAGE_SIZE, KV_WIDTH), lambda b, p, pt: (b, p, 0)),
                   pl.BlockSpec((None, PAGES_PER_STEP * PAGE_SIZE, KV_WIDTH), lambda b, p, pt: (b, p, 0))])
    return pl.pallas_call(
        _page_gather_kernel, grid_spec=grid_spec,
        out_shape=[jax.ShapeDtypeStruct((bsz, past // CMP_STRIDE, width), F32),
                   jax.ShapeDtypeStruct((bsz, past // CMP_STRIDE, width), F32),
                   jax.ShapeDtypeStruct((bsz, past, KV_WIDTH), F32),
                   jax.ShapeDtypeStruct((bsz, past, KV_WIDTH), F32)],
        compiler_params=_params(("parallel", "arbitrary")), name="page_gather",
    )(page_table, *([cache] * PAGES_PER_STEP))


def _dec_attn_kernel(q_ref, g_ref, nkv_ref, kc_ref, vc_ref, ks_ref, vs_ref, win_ref, msel_ref, e_ref, eg_ref,
                     o_ref, *, past, cb):
    pos = past
    n_blk = past // SLC_BLOCK + 1
    n_sel = min(N_SLC, n_blk)
    n_chunks = (past // SLC_BLOCK) // cb
    ckeys = cb * SLC_BLOCK
    win_len = win_ref.shape[0]
    row = lax.broadcasted_iota(I32, (N_HEADS, LANES), 0)
    lane = lax.broadcasted_iota(I32, (N_HEADS, LANES), 1)

    qrow = jnp.broadcast_to(q_ref[...], (N_HEADS, D_ATT))
    x = jnp.zeros((N_HEADS, LANES), F32)
    for j in range(4):
        x = jnp.where(row // 2 == j, qrow[:, j * LANES:(j + 1) * LANES], x)
    x = jnp.where((row % 2) != (row // GQA), pltpu.roll(x, HEAD_DIM, 1), x)
    q8 = jnp.where(lane // HEAD_DIM == row // GQA, x * ATT_SCALE, 0.0).astype(BF16)
    nkv = nkv_ref[...]

    def new_rows(col):
        return jnp.broadcast_to(nkv[:, col * KV_WIDTH:(col + 1) * KV_WIDTH], (N_HEADS, KV_WIDTH)).astype(BF16)

    s_c = _dot_nt(q8, kc_ref[...].astype(BF16))
    n_id = lax.broadcasted_iota(I32, s_c.shape, 1)
    p_c = _masked_softmax(s_c, n_id * CMP_STRIDE + (CMP_LEN - 1) <= pos)
    o_c = _dot(p_c.astype(BF16), vc_ref[...].astype(BF16))
    grp0 = jnp.sum(p_c[0:GQA], axis=0, keepdims=True)
    grp1 = jnp.sum(p_c[GQA:2 * GQA], axis=0, keepdims=True)
    p_grp = jnp.where(lax.broadcasted_iota(I32, p_c.shape, 0) < GQA, grp0, grp1)
    p_slc = _dot_exact(p_grp, msel_ref[...])
    blk = lax.broadcasted_iota(I32, p_slc.shape, 1)
    cur = pos // SLC_BLOCK
    forced = (blk == 0) | (blk == cur) | (blk == cur - 1)
    future = (blk * SLC_BLOCK > pos) | (blk >= n_blk)
    score = jnp.where(future, NEG_INF, jnp.where(forced, POS_INF, p_slc))
    sel = (_block_ranks(score, n_blk) < n_sel).astype(F32)

    k_new = new_rows(2)
    v_new = new_rows(3)
    ok_new = sel[:, cur:cur + 1] > 0.5
    s_new = jnp.where(ok_new, _dot_nt(q8, k_new)[:, 0:1], NEG_INF)
    scores, oks = [], []
    m = s_new
    for c in range(n_chunks):
        s = _dot_nt(q8, ks_ref[c * ckeys:(c + 1) * ckeys, :].astype(BF16))
        ok = _dot(sel[:, c * cb:(c + 1) * cb].astype(BF16), e_ref[...]) > 0.5
        s = jnp.where(ok, s, NEG_INF)
        m = jnp.maximum(m, jnp.max(s, axis=-1, keepdims=True))
        scores.append(s)
        oks.append(ok)
    m = jnp.where(m > NEG_INF, m, 0.0)
    e_new = jnp.where(ok_new, jnp.exp(s_new - m), 0.0)
    den = e_new
    acc = e_new.astype(BF16).astype(F32) * v_new.astype(F32)
    for c in range(n_chunks):
        e = jnp.where(oks[c], jnp.exp(scores[c] - m), 0.0)
        den = den + jnp.sum(e, axis=-1, keepdims=True)
        acc = acc + _dot(e.astype(BF16), vs_ref[c * ckeys:(c + 1) * ckeys, :].astype(BF16))
    o_s = acc / jnp.maximum(den, TINY)

    kw = win_ref[:, 0:KV_WIDTH].astype(BF16)
    vw = win_ref[:, KV_WIDTH:2 * KV_WIDTH].astype(BF16)
    s_w = _dot_nt(q8, kw)
    dist = win_len - lax.broadcasted_iota(I32, s_w.shape, 1)
    ok_w = (dist < WINDOW) & (pos - dist >= 0)
    s_w = jnp.where(ok_w, s_w, NEG_INF)
    s_wn = _dot_nt(q8, new_rows(4))[:, 0:1]
    m_w = jnp.maximum(jnp.max(s_w, axis=-1, keepdims=True), s_wn)
    e_w = jnp.where(ok_w, jnp.exp(s_w - m_w), 0.0)
    e_wn = jnp.exp(s_wn - m_w)
    den_w = jnp.sum(e_w, axis=-1, keepdims=True) + e_wn
    o_w = (_dot(e_w.astype(BF16), vw) + e_wn.astype(BF16).astype(F32) * new_rows(5).astype(F32)) / den_w

    gates = g_ref[...]
    out = jnp.zeros((1, D_ATT), F32)
    for t, o8 in enumerate((o_c, o_s, o_w)):
        pieces = [o8[hd:hd + 1, (hd // GQA) * HEAD_DIM:(hd // GQA + 1) * HEAD_DIM] for hd in range(N_HEADS)]
        g8 = jnp.broadcast_to(gates, (N_HEADS, LANES))
        out = out + _dot_exact(g8, eg_ref[t])[0:1] * jnp.concatenate(pieces, axis=1)
    o_ref[...] = out


def _decode_attention(q, gates, nkv, kc, vc, ks, vs, win, past):
    bsz = q.shape[0]
    n_cmp = kc.shape[1]
    n_past_blk = past // SLC_BLOCK
    cb = min(64, n_past_blk)
    n_blk_pad = -(-(n_past_blk + 1) // LANES) * LANES
    msel = _sel_matrix(n_cmp, n_blk_pad)
    e = (jnp.arange(cb * SLC_BLOCK)[None, :] // SLC_BLOCK == jnp.arange(cb)[:, None]).astype(BF16)
    eg = _gate_expand()
    win_len = win.shape[1]
    per_b = lambda b: (b, 0, 0)
    fix2 = lambda b: (0, 0)
    fix3 = lambda b: (0, 0, 0)
    once = pl.Buffered(1)
    return pl.pallas_call(
        functools.partial(_dec_attn_kernel, past=past, cb=cb),
        grid=(bsz,),
        in_specs=[pl.BlockSpec((None, 1, D_ATT), per_b), pl.BlockSpec((None, 1, LANES), per_b),
                  pl.BlockSpec((None, 1, 6 * KV_WIDTH), per_b),
                  pl.BlockSpec((None, n_cmp, KV_WIDTH), per_b), pl.BlockSpec((None, n_cmp, KV_WIDTH), per_b),
                  pl.BlockSpec((None, past, KV_WIDTH), per_b, pipeline_mode=once),
                  pl.BlockSpec((None, past, KV_WIDTH), per_b, pipeline_mode=once),
                  pl.BlockSpec((None, win_len, 2 * KV_WIDTH), per_b),
                  pl.BlockSpec((n_cmp, n_blk_pad), fix2), pl.BlockSpec((cb, cb * SLC_BLOCK), fix2),
                  pl.BlockSpec((3, LANES, D_ATT), fix3)],
        out_specs=pl.BlockSpec((None, 1, D_ATT), per_b),
        out_shape=jax.ShapeDtypeStruct((bsz, 1, D_ATT), F32),
        compiler_params=_params(("parallel",)), name="nsa_decode",
    )(q, gates, nkv, kc, vc, ks, vs, win, msel, e, eg)


def _top16(s, payload=None):
    n = s.shape[0]
    rid = lax.broadcasted_iota(I32, s.shape, 0).astype(F32)
    vals, picks = [], []
    for _ in range(PEER_TOPK):
        m = jnp.max(s, axis=0, keepdims=True)
        idx = jnp.min(jnp.where(s == m, rid, float(n)), axis=0, keepdims=True)
        hit = rid == idx
        vals.append(m)
        picks.append(idx if payload is None else jnp.sum(jnp.where(hit, payload, 0.0), axis=0, keepdims=True))
        s = jnp.where(hit, NEG_INF, s)
    return jnp.concatenate(vals, axis=0), jnp.concatenate(picks, axis=0)


def _candidate_cells():
    return [(a, b) for a in range(PEER_TOPK) for b in range(PEER_TOPK) if (a + 1) * (b + 1) <= PEER_TOPK]


def _peer_topk_kernel(h_ref, wq_ref, keys_ref, eidx_ref, gw_ref):
    q = _dot(h_ref[...].astype(BF16), wq_ref[...])
    half = PEER_DK // 2
    cells = _candidate_cells()
    for h in range(PEER_HEADS):
        tops = []
        for i in range(2):
            col = (h * 2 + i) * half
            s_t = _dot_nt(keys_ref[h * 2 + i], q[:, col:col + half].astype(BF16))
            tops.append(_top16(s_t))
        (s1, i1), (s2, i2) = tops
        cand = jnp.concatenate([s1[a:a + 1] + s2[b:b + 1] for a, b in cells], axis=0)
        cidx = jnp.concatenate([i1[a:a + 1] * N_KEYS + i2[b:b + 1] for a, b in cells], axis=0)
        top_s, eidx = _top16(cand, cidx)
        e = jnp.exp(top_s - top_s[0:1])
        gw_ref[h * PEER_TOPK:(h + 1) * PEER_TOPK, :] = e / jnp.sum(e, axis=0, keepdims=True)
        eidx_ref[h * PEER_TOPK:(h + 1) * PEER_TOPK, :] = eidx.astype(I32)


def _peer_topk(h, wq, keys, tm):
    n = h.shape[0]
    half = PEER_DK // 2
    return pl.pallas_call(
        _peer_topk_kernel, grid=(n // tm,),
        in_specs=[pl.BlockSpec((tm, D_MODEL), lambda i: (i, 0)),
                  pl.BlockSpec((D_MODEL, PEER_HEADS * PEER_DK), lambda i: (0, 0)),
                  pl.BlockSpec((PEER_HEADS * 2, N_KEYS, half), lambda i: (0, 0, 0))],
        out_specs=[pl.BlockSpec((PEER_ROWS, tm), lambda i: (0, i)),
                   pl.BlockSpec((PEER_ROWS, tm), lambda i: (0, i))],
        out_shape=[jax.ShapeDtypeStruct((PEER_ROWS, n), I32), jax.ShapeDtypeStruct((PEER_ROWS, n), F32)],
        compiler_params=_params(("parallel",)), name="peer_topk")(h, wq, keys)


def _pack_expert_tables(u_tab, v_tab):
    ub = lax.bitcast_convert_type(u_tab.astype(jnp.bfloat16), jnp.uint16).astype(jnp.uint32)
    vb = lax.bitcast_convert_type(v_tab.astype(jnp.bfloat16), jnp.uint16).astype(jnp.uint32)
    words = lax.bitcast_convert_type((vb << 16) | ub, I32)
    return words.reshape(words.shape[0] * PEER_FEAT_TILES, LANES)


def _peer_gather_kernel(idx_ref, gw_ref, x_ref, lng_ref, lnb_ref, uv_hbm, o_ref, idx_smem, *scratch):
    bufs = scratch[:PEER_SLOTS]
    ffn_ref, sem_idx, sem = scratch[PEER_SLOTS:]
    idx_copy = pltpu.make_async_copy(idx_ref, idx_smem, sem_idx)
    idx_copy.start()
    idx_copy.wait()

    tile = PEER_FEAT_TILES

    def issue(t, slot, lo=0, hi=PEER_ROWS):
        for e in range(lo, hi):
            row = pl.multiple_of(idx_smem[t, e], tile)
            copy = pltpu.make_async_copy(uv_hbm.at[pl.ds(row, tile)], bufs[slot].at[pl.ds(e * tile, tile)],
                                         sem.at[slot])
            copy.start(priority=e % 2)

    def wait(slot):
        pltpu.make_async_copy(uv_hbm.at[pl.ds(0, PEER_ROWS * tile)], bufs[slot], sem.at[slot]).wait()

    lane_t = lax.broadcasted_iota(I32, (PEER_ROWS, PEER_TB), 1)

    def feature_block(slot, c):
        return bufs[slot][pl.ds(c, PEER_ROWS, stride=tile), :]

    def token_step(t, slot, t_next, slot_next):
        per_part = PEER_ROWS // (2 * tile)

        def issue_part(k):
            if t_next is not None:
                issue(t_next, slot_next, k * per_part, (k + 1) * per_part)

        x_t = x_ref[pl.ds(t, 1), :]
        acc = None
        for c in range(tile):
            issue_part(c)
            u = lax.bitcast_convert_type(feature_block(slot, c) << 16, F32)
            part = u * x_t[:, c * LANES:(c + 1) * LANES]
            acc = part if acc is None else acc + part
        hidden = jnp.sum(acc, axis=1, keepdims=True)
        g_col = jnp.sum(jnp.where(lane_t == t, gw_ref[...], 0.0), axis=1, keepdims=True)
        coef = g_col * jax.nn.gelu(hidden)
        outs = []
        for c in range(tile):
            issue_part(tile + c)
            v = lax.bitcast_convert_type(feature_block(slot, c) & jnp.int32(-65536), F32)
            outs.append(jnp.sum(coef * v, axis=0, keepdims=True))
        ffn_ref[pl.ds(t, 1), :] = jnp.concatenate(outs, axis=1)

    ahead = PEER_SLOTS - 1
    for t0 in range(ahead):
        issue(t0, t0)

    def group(g, carry):
        for s in range(PEER_SLOTS):
            t = g * PEER_SLOTS + s
            wait(s)
            token_step(t, s, t + ahead, (s + ahead) % PEER_SLOTS)
        return carry

    n_groups = PEER_TB // PEER_SLOTS
    lax.fori_loop(0, n_groups - 1, group, 0)
    for s in range(PEER_SLOTS):
        t = (n_groups - 1) * PEER_SLOTS + s
        wait(s)
        token_step(t, s, t + ahead if t + ahead < PEER_TB else None, (s + ahead) % PEER_SLOTS)
    o_ref[...] = _layer_norm(DN_ALPHA * x_ref[...] + ffn_ref[...], lng_ref[...], lnb_ref[...])


def _peer_gather_ln(eidx, gw_t, x, uv_tab, g, b):
    n = x.shape[0]
    fix = lambda i: (0, 0)
    return pl.pallas_call(
        _peer_gather_kernel, grid=(n // PEER_TB,),
        in_specs=[pl.BlockSpec((PEER_TB, PEER_ROWS), lambda i: (i, 0)),
                  pl.BlockSpec((PEER_ROWS, PEER_TB), lambda i: (0, i)),
                  pl.BlockSpec((PEER_TB, D_MODEL), lambda i: (i, 0)),
                  pl.BlockSpec((1, D_MODEL), fix), pl.BlockSpec((1, D_MODEL), fix),
                  pl.BlockSpec(memory_space=pl.ANY)],
        out_specs=pl.BlockSpec((PEER_TB, D_MODEL), lambda i: (i, 0)),
        out_shape=jax.ShapeDtypeStruct((n, D_MODEL), F32),
        scratch_shapes=[pltpu.SMEM((PEER_TB, PEER_ROWS), I32)] +
                       [pltpu.VMEM((PEER_ROWS * PEER_FEAT_TILES, LANES), I32) for _ in range(PEER_SLOTS)] +
                       [pltpu.VMEM((PEER_TB, D_MODEL), F32),
                        pltpu.SemaphoreType.DMA(()),
                        pltpu.SemaphoreType.DMA((PEER_SLOTS,))],
        compiler_params=_params(("arbitrary",)), name="peer_gather",
    )(eidx, gw_t, x, g, b, uv_tab)


def _rope_tables(pos):
    half = HEAD_DIM // 2
    inv = ROPE_THETA ** (-jnp.arange(half, dtype=F32) / half)
    ang = pos.astype(F32)[:, None] * inv
    cos, sin = jnp.cos(ang), jnp.sin(ang)
    return jnp.tile(cos, (1, 4)), jnp.tile(jnp.concatenate([-sin, sin], axis=1), (1, 2))


def _layer_weights(p):
    w_in = p['w_in']
    w_u = w_in[:, :D_SSM].astype(BF16)
    pad = PROJ_W - (w_in.shape[1] - D_SSM)
    w_rest = jnp.pad(w_in[:, D_SSM:], ((0, 0), (0, pad))).astype(BF16)
    cmp_k = _compress_weights(p['cmp_pe'][0], p['cmp_w1'][0], p['cmp_w2'][0])
    cmp_v = _compress_weights(p['cmp_pe'][1], p['cmp_w1'][1], p['cmp_w2'][1])
    return dict(
        w_u=w_u, w_rest=w_rest, w_glu=p['w_glu'].astype(BF16), d=p['d'].reshape(1, D_SSM),
        wo_ssm=p['w_out'][:D_SSM].astype(BF16), wo_att=p['w_out'][D_SSM:].astype(BF16),
        ln1_g=p['ln1_g'].reshape(1, D_MODEL), ln1_b=p['ln1_b'].reshape(1, D_MODEL),
        ln2_g=p['ln2_g'].reshape(1, D_MODEL), ln2_b=p['ln2_b'].reshape(1, D_MODEL),
        peer_wq=p['peer_wq'].astype(BF16),
        peer_keys=p['peer_keys'].reshape(PEER_HEADS * 2, N_KEYS, PEER_DK // 2).astype(BF16),
        peer_uv=_pack_expert_tables(p['peer_u'], p['peer_v']), cmp_k=cmp_k, cmp_v=cmp_v)


def _token_tail(x, y_ssm, y_att, w, tm):
    h = _outproj_ln(y_ssm, y_att, x, w['wo_ssm'], w['wo_att'], w['ln1_g'], w['ln1_b'], tm)
    eidx_t, gw_t = _peer_topk(h, w['peer_wq'], w['peer_keys'], min(tm, 256))
    rows = eidx_t.T * PEER_FEAT_TILES
    return _peer_gather_ln(rows, gw_t, h, w['peer_uv'], w['ln2_g'], w['ln2_b'])


def _prompt_layer(x, bsz, seq, w, p):
    n = bsz * seq
    tm = min(512, seq)
    cos, sin = _rope_tables(jnp.arange(seq, dtype=I32))
    q, kv, gates = _project(x, w['w_rest'], cos, sin, tm)
    x_tb = x.reshape(bsz, seq, D_MODEL).transpose(1, 0, 2).reshape(n, D_MODEL)
    u_tb = _matmul(x_tb, w['w_u'], tm)
    wb, wc, ar, ai = _s5_weights(p['a_re'], p['a_im'], p['log_dt'], p['b_re'], p['b_im'], p['c_re'], p['c_im'], bsz)
    s0 = jnp.zeros((bsz, 2 * N_STATE), F32)
    gy_tb, s_fin = _s5_scan(u_tb, wb, wc, ar, ai, w['d'], s0, min(64, seq))
    y_ssm_tb = _glu(gy_tb, w['w_glu'], tm)
    y_ssm = y_ssm_tb.reshape(seq, bsz, D_SSM).transpose(1, 0, 2).reshape(n, D_SSM)
    n_cmp = seq // CMP_STRIDE
    xk = kv[:, 0:KV_WIDTH].reshape(bsz, n_cmp, CMP_STRIDE * KV_WIDTH)
    xv = kv[:, KV_WIDTH:2 * KV_WIDTH].reshape(bsz, n_cmp, CMP_STRIDE * KV_WIDTH)
    kc = _compress(xk, *w['cmp_k'])
    vc = _compress(xv, *w['cmp_v'])
    y_att = _attention(q, gates, kv, kc, vc, bsz, seq)
    y = _token_tail(x, y_ssm, y_att, w, tm)
    new_kv = kv[:, :4 * KV_WIDTH].reshape(bsz, seq, 4, N_KV_HEADS, HEAD_DIM)
    new_win = kv[:, 4 * KV_WIDTH:].reshape(bsz, seq, 2, N_KV_HEADS, HEAD_DIM)
    win = jnp.concatenate([jnp.zeros((bsz, WINDOW, 2, N_KV_HEADS, HEAD_DIM), F32), new_win], axis=1)[:, -WINDOW:]
    return y, new_kv, win, s_fin.reshape(bsz, 2, N_GROUPS, SSM_STATE)


def _sample_layer(x, bsz, past, cache, page_table, layer, win_past, ssm_state, w, p):
    rows = x.shape[0]
    cos, sin = _rope_tables(jnp.full((rows,), past, I32))
    q, kv, gates = _project(x, w['w_rest'], cos, sin, rows)
    u = _matmul(x, w['w_u'], rows)
    wb, wc, ar, ai = _s5_weights(p['a_re'], p['a_im'], p['log_dt'], p['b_re'], p['b_im'], p['c_re'], p['c_im'], bsz)
    gy, s_fin = _s5_scan(u[:bsz], wb, wc, ar, ai, w['d'], ssm_state.reshape(bsz, 2 * N_STATE), 1)
    y_ssm = jnp.pad(_glu(gy, w['w_glu'], bsz), ((0, rows - bsz), (0, 0)))
    xk, xv, ks, vs = _page_gather(cache, page_table, layer)
    kc = _compress(xk, *w['cmp_k'])
    vc = _compress(xv, *w['cmp_v'])
    win_len = win_past.shape[1]
    win_rows = win_past.reshape(bsz, win_len, 2 * KV_WIDTH)
    y_att = _decode_attention(q[:bsz].reshape(bsz, 1, D_ATT), gates[:bsz].reshape(bsz, 1, LANES),
                              kv[:bsz].reshape(bsz, 1, 6 * KV_WIDTH), kc, vc, ks, vs, win_rows, past)
    y_att = jnp.pad(y_att.reshape(bsz, D_ATT), ((0, rows - bsz), (0, 0)))
    y = _token_tail(x, y_ssm, y_att, w, rows)
    new_kv = kv[:bsz, :4 * KV_WIDTH].reshape(bsz, 1, 4, N_KV_HEADS, HEAD_DIM)
    new_win = kv[:bsz, 4 * KV_WIDTH:].reshape(bsz, 1, 2, N_KV_HEADS, HEAD_DIM)
    win = jnp.concatenate([win_past, new_win], axis=1)[:, -win_len:]
    return y, new_kv, win, s_fin.reshape(bsz, 2, N_GROUPS, SSM_STATE)


def kernel(x_prompt, x_sample, cache_kv, cache_win, state_ssm, page_table, w_in, ssm_a_re, ssm_a_im, ssm_log_dt, ssm_b_re, ssm_b_im, ssm_c_re, ssm_c_im, ssm_d, w_glu, cmp_pe, cmp_w1, cmp_w2, w_out, ln1_g, ln1_b, peer_wq, peer_keys, peer_u, peer_v, ln2_g, ln2_b):
    bsz, seq, _ = x_prompt.shape
    dec_bsz = x_sample.shape[0]
    depth = w_in.shape[0]
    past = page_table.shape[1] * PAGE_SIZE
    cache = jnp.transpose(cache_kv, (0, 1, 3, 4, 5, 2))
    dec_rows = -(-dec_bsz // PEER_TB) * PEER_TB
    y_p = x_prompt.reshape(bsz * seq, D_MODEL)
    y_s = jnp.pad(x_sample.reshape(dec_bsz, D_MODEL), ((0, dec_rows - dec_bsz), (0, 0)))
    outs = [[] for _ in range(6)]
    for l in range(depth):
        p = {'w_in': w_in[l], 'a_re': ssm_a_re[l], 'a_im': ssm_a_im[l], 'log_dt': ssm_log_dt[l],
             'b_re': ssm_b_re[l], 'b_im': ssm_b_im[l], 'c_re': ssm_c_re[l], 'c_im': ssm_c_im[l],
             'd': ssm_d[l], 'w_glu': w_glu[l], 'cmp_pe': cmp_pe[l], 'cmp_w1': cmp_w1[l], 'cmp_w2': cmp_w2[l],
             'w_out': w_out[l], 'ln1_g': ln1_g[l], 'ln1_b': ln1_b[l], 'peer_wq': peer_wq[l],
             'peer_keys': peer_keys[l], 'peer_u': peer_u[l], 'peer_v': peer_v[l],
             'ln2_g': ln2_g[l], 'ln2_b': ln2_b[l]}
        w = _layer_weights(p)
        y_s, kvn, winn, sn = _sample_layer(y_s, dec_bsz, past, cache, page_table, l, cache_win[l], state_ssm[l], w, p)
        outs[3].append(kvn)
        outs[4].append(winn)
        outs[5].append(sn)
        y_p, kvn, winn, sn = _prompt_layer(y_p, bsz, seq, w, p)
        outs[0].append(kvn)
        outs[1].append(winn)
        outs[2].append(sn)
    return (y_p.reshape(bsz, seq, D_MODEL), y_s[:dec_bsz].reshape(dec_bsz, 1, D_MODEL),
            jnp.stack(outs[0]), jnp.stack(outs[1]), jnp.stack(outs[2]),
            jnp.stack(outs[3]), jnp.stack(outs[4]), jnp.stack(outs[5]))
```
